```python
import jax, jax.numpy as jnp
from jax import lax
import numpy as np

D_MODEL = 1024
BATCH = 8
SEQ = 2048
DEPTH = 4

N_MIXERS = 4
Q_BLOCK = 128
GATHER_CHUNK = 16
MAX_POS_OFFSET = 4096

MLA_HEADS = 8
MLA_NOPE = 128
MLA_ROPE = 64
MLA_V = 128
MLA_Q_RANK = 256
MLA_KV_RANK = 256
ROPE_BASE = 10000.0

MOBA_HEADS = 8
MOBA_HD = 128
MOBA_BLOCK = 256
MOBA_TOPK = 3

NSA_HEADS = 8
NSA_GROUPS = 2
NSA_HD = 128
NSA_CMP_LEN = 32
NSA_CMP_STRIDE = 16
NSA_SEL_BLOCK = 64
NSA_SEL_TOPN = 16
NSA_WINDOW = 512
NSA_FORCE_BONUS = 100.0

SB_HEADS = 8
SB_HD = 128

D_FF = 2816
CONV_W = 3
LN_EPS = 1e-5
RMS_EPS = 1e-6
NEG = -1e30
TINY = 1e-30
DEEPNORM_ALPHA = (2 * DEPTH) ** 0.25
DEEPNORM_BETA = (8 * DEPTH) ** -0.25

kernel_name = 'hybrid_mla_moba_nsa_stickbreaking_trunk'


def _count(kind):
    return len(range(kind, DEPTH, N_MIXERS))


def layer_norm(x, g, b):
    xf = x.astype(jnp.float32)
    mu = jnp.mean(xf, -1, keepdims=True)
    var = jnp.mean(jnp.square(xf - mu), -1, keepdims=True)
    return ((xf - mu) * lax.rsqrt(var + LN_EPS) * g.astype(jnp.float32) + b.astype(jnp.float32)).astype(x.dtype)


def rms_norm(x, g):
    xf = x.astype(jnp.float32)
    return (xf * lax.rsqrt(jnp.mean(jnp.square(xf), -1, keepdims=True) + RMS_EPS) * g.astype(jnp.float32)).astype(x.dtype)


def apply_rope(x, cos, sin):
    half = x.shape[-1] // 2
    x1 = x[..., :half].astype(jnp.float32)
    x2 = x[..., half:].astype(jnp.float32)
    return jnp.concatenate([x1 * cos - x2 * sin, x2 * cos + x1 * sin], -1).astype(x.dtype)


def alibi_slopes(n):
    return 2.0 ** (-8.0 * jnp.arange(1, n + 1, dtype=jnp.float32) / n)


def masked_softmax(s, mask):
    s = jnp.where(mask, s, NEG)
    p = jnp.where(mask, jnp.exp(s - jnp.max(s, -1, keepdims=True)), 0.0)
    return p / jnp.maximum(jnp.sum(p, -1, keepdims=True), TINY)


def split_heads3(proj, n_heads, hd):
    B, S, _ = proj.shape
    t = proj.reshape(B, S, 3, n_heads, hd).transpose(2, 0, 3, 1, 4)
    return t[0], t[1], t[2]


def merge_heads(o):
    B, H, S, D = o.shape
    return o.transpose(0, 2, 1, 3).reshape(B, S, H * D)


def causal_softmax_attention(q, k, v, scale):
    B, H, S, Dk = q.shape
    nqb = S // Q_BLOCK
    q_blocks = q.reshape(B, H, nqb, Q_BLOCK, Dk).transpose(2, 0, 1, 3, 4)
    k_idx = jnp.arange(S)

    def block(args):
        q_b, bi = args
        s = jnp.einsum('bhqd,bhkd->bhqk', q_b, k, preferred_element_type=jnp.float32) * scale
        q_idx = bi * Q_BLOCK + jnp.arange(Q_BLOCK)
        p = masked_softmax(s, k_idx[None, :] <= q_idx[:, None])
        return jnp.einsum('bhqk,bhkd->bhqd', p.astype(v.dtype), v)

    o = lax.map(block, (q_blocks, jnp.arange(nqb)))
    return o.transpose(1, 2, 0, 3, 4).reshape(B, H, S, v.shape[-1])


def mla_mixer(h, cos, sin, w_in, q_norm, w_uq, kv_norm, w_ukv, w_o):
    B, S, _ = h.shape
    H, DN, DR, DV = MLA_HEADS, MLA_NOPE, MLA_ROPE, MLA_V
    proj = h @ w_in
    c_q = rms_norm(proj[..., :MLA_Q_RANK], q_norm)
    c_kv = rms_norm(proj[..., MLA_Q_RANK:MLA_Q_RANK + MLA_KV_RANK], kv_norm)
    k_rope = apply_rope(proj[..., MLA_Q_RANK + MLA_KV_RANK:], cos, sin)
    q = (c_q @ w_uq).reshape(B, S, H, DN + DR).transpose(0, 2, 1, 3)
    q = jnp.concatenate([q[..., :DN], apply_rope(q[..., DN:], cos[:, None], sin[:, None])], -1)
    kv = (c_kv @ w_ukv).reshape(B, S, H, DN + DV).transpose(0, 2, 1, 3)
    k = jnp.concatenate([kv[..., :DN], jnp.broadcast_to(k_rope[:, None], (B, H, S, DR))], -1)
    v = kv[..., DN:]
    o = causal_softmax_attention(q, k, v, (DN + DR) ** -0.5)
    return merge_heads(o) @ w_o


def moba_mixer(h, positions, w_in, w_o):
    B, S, _ = h.shape
    H, D, BLK, C = MOBA_HEADS, MOBA_HD, MOBA_BLOCK, GATHER_CHUNK
    q, k, v = split_heads3(h @ w_in, H, D)
    nb = -(-S // BLK)
    pad = nb * BLK - S
    posf = positions.astype(jnp.float32)
    k_p = jnp.pad(k, ((0, 0), (0, 0), (0, pad), (0, 0)))
    v_p = jnp.pad(v, ((0, 0), (0, 0), (0, pad), (0, 0)))
    pos_p = jnp.pad(posf, ((0, 0), (0, pad)))
    k_blk = k_p.reshape(B, H, nb, BLK, D)
    v_blk = v_p.reshape(B, H, nb, BLK, D)
    pos_blk = pos_p.reshape(B, nb, BLK)
    k_mean = jnp.mean(k_blk.astype(jnp.float32), axis=3)
    slope = alibi_slopes(H)[None, :, None, None]
    kk = min(MOBA_TOPK, nb)
    nch = S // C
    scale = D ** -0.5
    b_ix = jnp.arange(B)[:, None, None, None]
    h_ix = jnp.arange(H)[None, :, None, None]
    blk_off = jnp.arange(BLK)
    sel_rank = jnp.arange(kk)

    def chunk(args):
        q_c, pq, ci = args
        start = ci * C
        q_idx = start + jnp.arange(C)
        own = start // BLK
        gate = jnp.einsum('bhqd,bhnd->bhqn', q_c.astype(jnp.float32), k_mean)
        gate = jnp.where(jnp.arange(nb) < own, gate, NEG)
        _, sel = lax.top_k(gate, kk)
        sel_valid = sel_rank < own
        k_sel = k_blk[b_ix, h_ix, sel]
        v_sel = v_blk[b_ix, h_ix, sel]
        kpos_sel = pos_blk[b_ix, sel]
        s_sel = jnp.einsum('bhqd,bhqnkd->bhqnk', q_c, k_sel, preferred_element_type=jnp.float32) * scale
        s_sel = s_sel - slope[..., None] * jnp.abs(pq[:, None, :, None, None] - kpos_sel)
        k_own = lax.dynamic_slice_in_dim(k_p, own * BLK, BLK, axis=2)
        v_own = lax.dynamic_slice_in_dim(v_p, own * BLK, BLK, axis=2)
        kpos_own = lax.dynamic_slice_in_dim(pos_p, own * BLK, BLK, axis=1)
        s_own = jnp.einsum('bhqd,bhkd->bhqk', q_c, k_own, preferred_element_type=jnp.float32) * scale
        s_own = s_own - slope * jnp.abs(pq[:, None, :, None] - kpos_own[:, None, None, :])
        mask_own = (own * BLK + blk_off)[None, :] <= q_idx[:, None]
        mask_sel = jnp.broadcast_to(jnp.repeat(sel_valid, BLK)[None, :], (C, kk * BLK))
        s = jnp.concatenate([s_sel.reshape(B, H, C, kk * BLK), s_own], -1)
        p = masked_softmax(s, jnp.concatenate([mask_sel, mask_own], -1))
        p_sel = p[..., :kk * BLK].reshape(B, H, C, kk, BLK).astype(v.dtype)
        p_own = p[..., kk * BLK:].astype(v.dtype)
        return (jnp.einsum('bhqnk,bhqnkd->bhqd', p_sel, v_sel)
                + jnp.einsum('bhqk,bhkd->bhqd', p_own, v_own))

    q_chunks = q.reshape(B, H, nch, C, D).transpose(2, 0, 1, 3, 4)
    pq_chunks = posf.reshape(B, nch, C).transpose(1, 0, 2)
    o = lax.map(chunk, (q_chunks, pq_chunks, jnp.arange(nch)))
    o = o.transpose(1, 2, 0, 3, 4).reshape(B, H, S, D)
    return merge_heads(o) @ w_o


def nsa_mixer(h, positions, w_in, cmp_pos, cmp_w1, cmp_w2, w_o):
    B, S, _ = h.shape
    H, G, D = NSA_HEADS, NSA_GROUPS, NSA_HD
    R = H // G
    L, ST, SB, W = NSA_CMP_LEN, NSA_CMP_STRIDE, NSA_SEL_BLOCK, NSA_WINDOW
    proj = h @ w_in
    q = proj[..., :H * D].reshape(B, S, G, R, D).transpose(0, 2, 3, 1, 4)
    kv = proj[..., H * D:H * D + 6 * G * D].reshape(B, S, 6, G, D).transpose(2, 0, 3, 1, 4)
    k_cmp_raw, v_cmp_raw, k_slc, v_slc, k_win, v_win = kv[0], kv[1], kv[2], kv[3], kv[4], kv[5]
    gates = jax.nn.sigmoid(proj[..., H * D + 6 * G * D:].astype(jnp.float32))
    gates = gates.reshape(B, S, 3, G, R).transpose(2, 0, 3, 4, 1)[..., None]
    posf = positions.astype(jnp.float32)
    slope = alibi_slopes(H).reshape(G, R)[None, :, :, None, None]
    scale = D ** -0.5
    q_idx_all = jnp.arange(S)

    nc = (S - L) // ST + 1
    c_start = jnp.arange(nc) * ST
    c_idx = c_start[:, None] + jnp.arange(L)[None, :]
    c_end = c_start + L - 1

    def compress(t, pe, w1, w2):
        blocks = t[:, :, c_idx] + pe
        return jax.nn.gelu(blocks.reshape(B, G, nc, L * D) @ w1) @ w2

    k_c = compress(k_cmp_raw, cmp_pos[0], cmp_w1[0], cmp_w2[0])
    v_c = compress(v_cmp_raw, cmp_pos[1], cmp_w1[1], cmp_w2[1])
    s_c = jnp.einsum('bgrqd,bgnd->bgrqn', q, k_c, preferred_element_type=jnp.float32) * scale
    s_c = s_c - slope * jnp.abs(posf[:, None, None, :, None] - posf[:, c_end][:, None, None, None, :])
    p_c = masked_softmax(s_c, c_end[None, :] <= q_idx_all[:, None])
    o_cmp = jnp.einsum('bgrqn,bgnd->bgrqd', p_c.astype(v_c.dtype), v_c)

    ns = S // SB
    j_start = jnp.arange(ns) * SB
    overlap = ((c_start[:, None] < j_start[None, :] + SB) & (c_end[:, None] >= j_start[None, :])).astype(jnp.float32)
    imp = jnp.einsum('bgrqn,nj->bgqj', p_c, overlap)
    q_blk = q_idx_all // SB
    jj = jnp.arange(ns)
    forced = ((jj[None, :] == 0) | (jj[None, :] == q_blk[:, None]) | (jj[None, :] == q_blk[:, None] - 1)).astype(jnp.float32)
    imp = jnp.where(jj[None, :] <= q_blk[:, None], imp + NSA_FORCE_BONUS * forced, NEG)
    topn = min(NSA_SEL_TOPN, ns)
    _, sel = lax.top_k(imp, topn)
    sel_valid = jnp.arange(topn)[None, :] <= q_blk[:, None]

    k_sb = k_slc.reshape(B, G, ns, SB, D)
    v_sb = v_slc.reshape(B, G, ns, SB, D)
    pos_sb = posf.reshape(B, ns, SB)
    C = GATHER_CHUNK
    nch = S // C
    b_ix = jnp.arange(B)[:, None, None, None]
    g_ix = jnp.arange(G)[None, :, None, None]
    sb_off = jnp.arange(SB)

    def sel_chunk(args):
        q_c, pq, sel_c, valid_c, ci = args
        q_i = ci * C + jnp.arange(C)
        k_g = k_sb[b_ix, g_ix, sel_c]
        v_g = v_sb[b_ix, g_ix, sel_c]
        kpos = pos_sb[b_ix, sel_c]
        k_i = sel_c[..., None] * SB + sb_off
        s = jnp.einsum('bgrqd,bgqnkd->bgrqnk', q_c, k_g, preferred_element_type=jnp.float32) * scale
        s = s - slope[..., None] * jnp.abs(pq[:, None, None, :, None, None] - kpos[:, :, None])
        mask = valid_c[None, None, :, :, None] & (k_i <= q_i[None, None, :, None, None])
        p = masked_softmax(s.reshape(B, G, R, C, topn * SB), mask.reshape(B, G, 1, C, topn * SB))
        return jnp.einsum('bgrqnk,bgqnkd->bgrqd', p.reshape(B, G, R, C, topn, SB).astype(v_g.dtype), v_g)

    o_sel = lax.map(sel_chunk, (
        q.reshape(B, G, R, nch, C, D).transpose(3, 0, 1, 2, 4, 5),
        posf.reshape(B, nch, C).transpose(1, 0, 2),
        sel.reshape(B, G, nch, C, topn).transpose(2, 0, 1, 3, 4),
        sel_valid.reshape(nch, C, topn),
        jnp.arange(nch)))
    o_sel = o_sel.transpose(1, 2, 3, 0, 4, 5).reshape(B, G, R, S, D)

    QB = Q_BLOCK
    nqb = S // QB
    span = W + QB
    band = jnp.arange(nqb)[:, None] * QB + jnp.arange(span)[None, :]
    k_w = jnp.pad(k_win, ((0, 0), (0, 0), (W, 0), (0, 0)))[:, :, band]
    v_w = jnp.pad(v_win, ((0, 0), (0, 0), (W, 0), (0, 0)))[:, :, band]
    kpos_w = jnp.pad(posf, ((0, 0), (W, 0)))[:, band]
    k_i = (band - W)[:, None, :]
    q_i = (jnp.arange(nqb)[:, None] * QB + jnp.arange(QB)[None, :])[:, :, None]
    mask_w = (k_i >= 0) & (k_i <= q_i) & (q_i - k_i < W)
    q_w = q.reshape(B, G, R, nqb, QB, D)
    s_w = jnp.einsum('bgriqd,bgikd->bgriqk', q_w, k_w, preferred_element_type=jnp.float32) * scale
    s_w = s_w - slope[..., None] * jnp.abs(posf.reshape(B, nqb, QB)[:, None, None, :, :, None] - kpos_w[:, None, None, :, None, :])
    p_w = masked_softmax(s_w, mask_w)
    o_win = jnp.einsum('bgriqk,bgikd->bgriqd', p_w.astype(v_w.dtype), v_w).reshape(B, G, R, S, D)

    o = gates[0] * o_cmp + gates[1] * o_sel + gates[2] * o_win
    o = o.transpose(0, 3, 1, 2, 4).reshape(B, S, H * D).astype(h.dtype)
    return o @ w_o


def stick_breaking_mixer(h, w_in, w_o):
    B, S, _ = h.shape
    H, D = SB_HEADS, SB_HD
    q, k, v = split_heads3(h @ w_in, H, D)
    scale = D ** -0.5
    nqb = S // Q_BLOCK
    k_idx = jnp.arange(S)

    def block(args):
        q_b, bi = args
        z = jnp.einsum('bhqd,bhkd->bhqk', q_b, k, preferred_element_type=jnp.float32) * scale
        q_idx = bi * Q_BLOCK + jnp.arange(Q_BLOCK)
        mask = k_idx[None, :] < q_idx[:, None]
        log_beta = jax.nn.log_sigmoid(z)
        log_1mb = jnp.where(mask, jax.nn.log_sigmoid(-z), 0.0)
        tail = lax.cumsum(log_1mb, axis=3, reverse=True) - log_1mb
        a = jnp.where(mask, jnp.exp(log_beta + tail), 0.0)
        return jnp.einsum('bhqk,bhkd->bhqd', a.astype(v.dtype), v)

    q_blocks = q.reshape(B, H, nqb, Q_BLOCK, D).transpose(2, 0, 1, 3, 4)
    o = lax.map(block, (q_blocks, jnp.arange(nqb)))
    o = o.transpose(1, 2, 0, 3, 4).reshape(B, H, S, D)
    return merge_heads(o) @ w_o


def conv_ffn(h, w_in, conv_w, conv_b, w_out):
    u = h @ w_in
    a, b = u[..., :D_FF], u[..., D_FF:]
    a = lax.conv_general_dilated(a, conv_w[:, None, :].astype(a.dtype), window_strides=(1,),
                                 padding=[(CONV_W - 1, 0)], dimension_numbers=('NWC', 'WIO', 'NWC'),
                                 feature_group_count=D_FF) + conv_b
    return (jax.nn.gelu(a) * b) @ w_out


def _w(key, shape, fan_in, scale=1.0):
    return jax.random.normal(key, shape, jnp.float32) * (scale * fan_in ** -0.5)


def _gain(key, shape):
    return 1.0 + 0.02 * jax.random.normal(key, shape, jnp.float32)


def _small(key, shape):
    return 0.02 * jax.random.normal(key, shape, jnp.float32)


def setup_inputs(seed: int = 0) -> dict:
    key = jax.random.key(seed)
    ks = jax.random.split(key, 32)
    D = D_MODEL
    nA, nB, nC, nD = _count(0), _count(1), _count(2), _count(3)
    mla_out = MLA_Q_RANK + MLA_KV_RANK + MLA_ROPE
    nsa_out = NSA_HEADS * NSA_HD + 6 * NSA_GROUPS * NSA_HD + 3 * NSA_HEADS
    mod_base = jnp.repeat(jnp.array([0.0, 0.0, 1.0, 0.0, 0.0, 1.0], jnp.float32), D)
    offset = jax.random.randint(ks[2], (BATCH, 1), 0, MAX_POS_OFFSET, dtype=jnp.int32)
    return {
        'x': jax.random.normal(ks[0], (BATCH, SEQ, D), jnp.float32),
        'c': jax.random.normal(ks[1], (BATCH, D), jnp.float32),
        'positions': offset + jnp.arange(SEQ, dtype=jnp.int32)[None, :],
        'mod_w': _w(ks[3], (DEPTH, D, 6 * D), D, 0.1),
        'mod_b': mod_base + _small(ks[4], (DEPTH, 6 * D)),
        'ln_g': _gain(ks[5], (DEPTH, 2, D)),
        'ln_b': _small(ks[6], (DEPTH, 2, D)),
        'ffn_w_in': _w(ks[7], (DEPTH, D, 2 * D_FF), D),
        'ffn_conv_w': _w(ks[8], (DEPTH, CONV_W, D_FF), CONV_W),
        'ffn_conv_b': _small(ks[9], (DEPTH, D_FF)),
        'ffn_w_out': _w(ks[10], (DEPTH, D_FF, D), D_FF, DEEPNORM_BETA),
        'mla_w_in': _w(ks[11], (nA, D, mla_out), D),
        'mla_q_norm': _gain(ks[12], (nA, MLA_Q_RANK)),
        'mla_w_uq': _w(ks[13], (nA, MLA_Q_RANK, MLA_HEADS * (MLA_NOPE + MLA_ROPE)), MLA_Q_RANK),
        'mla_kv_norm': _gain(ks[14], (nA, MLA_KV_RANK)),
        'mla_w_ukv': _w(ks[15], (nA, MLA_KV_RANK, MLA_HEADS * (MLA_NOPE + MLA_V)), MLA_KV_RANK),
        'mla_w_o': _w(ks[16], (nA, MLA_HEADS * MLA_V, D), MLA_HEADS * MLA_V, DEEPNORM_BETA),
        'moba_w_in': _w(ks[17], (nB, D, 3 * MOBA_HEADS * MOBA_HD), D),
        'moba_w_o': _w(ks[18], (nB, MOBA_HEADS * MOBA_HD, D), MOBA_HEADS * MOBA_HD, DEEPNORM_BETA),
        'nsa_w_in': _w(ks[19], (nC, D, nsa_out), D),
        'nsa_cmp_pos': _small(ks[20], (nC, 2, NSA_CMP_LEN, NSA_HD)),
        'nsa_cmp_w1': _w(ks[21], (nC, 2, NSA_CMP_LEN * NSA_HD, NSA_HD), NSA_CMP_LEN * NSA_HD),
        'nsa_cmp_w2': _w(ks[22], (nC, 2, NSA_HD, NSA_HD), NSA_HD),
        'nsa_w_o': _w(ks[23], (nC, NSA_HEADS * NSA_HD, D), NSA_HEADS * NSA_HD, DEEPNORM_BETA),
        'sb_w_in': _w(ks[24], (nD, D, 3 * SB_HEADS * SB_HD), D),
        'sb_w_o': _w(ks[25], (nD, SB_HEADS * SB_HD, D), SB_HEADS * SB_HD, DEEPNORM_BETA),
    }


def reference(x, c, positions, mod_w, mod_b, ln_g, ln_b, ffn_w_in, ffn_conv_w, ffn_conv_b, ffn_w_out,
              mla_w_in, mla_q_norm, mla_w_uq, mla_kv_norm, mla_w_ukv, mla_w_o,
              moba_w_in, moba_w_o, nsa_w_in, nsa_cmp_pos, nsa_cmp_w1, nsa_cmp_w2, nsa_w_o,
              sb_w_in, sb_w_o):
    B, S, D = x.shape
    half = MLA_ROPE // 2
    inv_freq = ROPE_BASE ** (-jnp.arange(half, dtype=jnp.float32) / half)
    ang = positions.astype(jnp.float32)[..., None] * inv_freq
    cos, sin = jnp.cos(ang), jnp.sin(ang)
    c_act = jax.nn.silu(c)
    for i in range(DEPTH):
        kind, j = i % N_MIXERS, i // N_MIXERS
        mod = (c_act @ mod_w[i] + mod_b[i]).reshape(B, 6, D)[:, :, None, :]
        hin = x * (1.0 + mod[:, 1]) + mod[:, 0]
        if kind == 0:
            y = mla_mixer(hin, cos, sin, mla_w_in[j], mla_q_norm[j], mla_w_uq[j], mla_kv_norm[j], mla_w_ukv[j], mla_w_o[j])
        elif kind == 1:
            y = moba_mixer(hin, positions, moba_w_in[j], moba_w_o[j])
        elif kind == 2:
            y = nsa_mixer(hin, positions, nsa_w_in[j], nsa_cmp_pos[j], nsa_cmp_w1[j], nsa_cmp_w2[j], nsa_w_o[j])
        else:
            y = stick_breaking_mixer(hin, sb_w_in[j], sb_w_o[j])
        x = layer_norm(DEEPNORM_ALPHA * x + mod[:, 2] * y, ln_g[i, 0], ln_b[i, 0])
        hin = x * (1.0 + mod[:, 4]) + mod[:, 3]
        y = conv_ffn(hin, ffn_w_in[i], ffn_conv_w[i], ffn_conv_b[i], ffn_w_out[i])
        x = layer_norm(DEEPNORM_ALPHA * x + mod[:, 5] * y, ln_g[i, 1], ln_b[i, 1])
    return x
```

```python
import functools

import jax
import jax.numpy as jnp
from jax import lax
from jax.experimental import pallas as pl
from jax.experimental.pallas import tpu as pltpu

F32 = jnp.float32
BF16 = jnp.bfloat16

D_MODEL = 1024
N_MIXERS = 4
MLA_HEADS, MLA_NOPE, MLA_ROPE, MLA_V = 8, 128, 64, 128
MLA_Q_RANK, MLA_KV_RANK = 256, 256
ROPE_BASE = 10000.0
MOBA_HEADS, MOBA_HD, MOBA_BLOCK, MOBA_TOPK = 8, 128, 256, 3
NSA_HEADS, NSA_GROUPS, NSA_HD = 8, 2, 128
NSA_R = NSA_HEADS // NSA_GROUPS
NSA_CMP_LEN, NSA_CMP_STRIDE, NSA_SEL_BLOCK, NSA_SEL_TOPN, NSA_WINDOW = 32, 16, 64, 16, 512
NSA_FORCE_BONUS = 100.0
SB_HEADS, SB_HD = 8, 128
D_FF = 2816
LN_EPS = 1e-5
RMS_EPS = 1e-6
NEG = -1e30
M_INIT = -1e29
LOW = -3e38
TINY = 1e-30

LANES = 128
SUBLANES = 8
VMEM_LIMIT_BYTES = 56 * 1024 * 1024

TOKEN_TILE = 512
ATT_TQ = 256
ATT_TK = 256
NSA_TQ = 128
NSA_TK = 256
FF_CHUNK = 256
PROJ_CHUNK = 512


def _cparams(n_grid):
    return pltpu.CompilerParams(dimension_semantics=("arbitrary",) * n_grid,
                                vmem_limit_bytes=VMEM_LIMIT_BYTES)


def _dot(a, b):
    return jnp.dot(a, b, preferred_element_type=F32)


def _dot_nt(a, b):
    return lax.dot_general(a, b, (((1,), (1,)), ((), ())), preferred_element_type=F32)


def _layer_norm(z, g, b):
    mu = jnp.mean(z, axis=-1, keepdims=True)
    d = z - mu
    var = jnp.mean(d * d, axis=-1, keepdims=True)
    return d * lax.rsqrt(var + LN_EPS) * g + b


def _rms_norm(z, g):
    return z * lax.rsqrt(jnp.mean(z * z, axis=-1, keepdims=True) + RMS_EPS) * g


def _gelu_tanh(x):
    return 0.5 * x * (1.0 + jnp.tanh(0.7978845608028654 * (x + 0.044715 * (x * x * x))))


def _softmax_update(s, m, l, acc, v):
    m_new = jnp.maximum(m, jnp.max(s, axis=1, keepdims=True))
    alpha = jnp.exp(m - m_new)
    p = jnp.exp(s - m_new)
    l = alpha * l + jnp.sum(p, axis=1, keepdims=True)
    acc = alpha * acc + _dot(p.astype(BF16), v)
    return m_new, l, acc


def _mod_kernel(c_ref, w_ref, b_ref, o_ref):
    c = c_ref[...]
    c_act = c * jax.nn.sigmoid(c)
    o_ref[...] = jnp.dot(c_act, w_ref[...], preferred_element_type=F32,
                         precision=lax.Precision.HIGHEST) + b_ref[...]


def _modulation(c, mod_w, mod_b):
    depth, d, n = mod_w.shape
    bsz = c.shape[0]
    tn = 1536
    out = pl.pallas_call(
        _mod_kernel,
        grid=(depth, n // tn),
        in_specs=[pl.BlockSpec((bsz, d), lambda l, j: (0, 0)),
                  pl.BlockSpec((None, d, tn), lambda l, j: (l, 0, j)),
                  pl.BlockSpec((None, 1, tn), lambda l, j: (l, 0, j))],
        out_specs=pl.BlockSpec((None, bsz, tn), lambda l, j: (l, 0, j)),
        out_shape=jax.ShapeDtypeStruct((depth, bsz, n), F32),
        compiler_params=_cparams(2),
        name="modulation",
    )(c, mod_w, mod_b.reshape(depth, 1, n))
    return out.reshape(depth, bsz, 6, d)


def _proj_kernel(x_ref, mod_ref, w_ref, cs_ref, *out_refs, n_main, n_total):
    h = (x_ref[...] * (1.0 + mod_ref[1:2, :]) + mod_ref[0:1, :]).astype(BF16)
    for n0 in range(0, n_total, PROJ_CHUNK):
        n1 = min(n0 + PROJ_CHUNK, n_total)
        y = _dot(h, w_ref[:, n0:n1]) * cs_ref[:, n0:n1]
        if n0 < n_main:
            out_refs[0][:, n0:n1] = y.astype(BF16)
        else:
            out_refs[1][:, n0 - n_main:n1 - n_main] = y


def _project(x, mod, w, col_scale, n_main):
    bsz, seq, d = x.shape
    n_total = w.shape[1]
    assert n_main % PROJ_CHUNK == 0 and n_total % LANES == 0
    tm = TOKEN_TILE
    out_shape = [jax.ShapeDtypeStruct((bsz, seq, n_main), BF16)]
    out_specs = [pl.BlockSpec((None, tm, n_main), lambda b, i: (b, i, 0))]
    if n_main < n_total:
        out_shape.append(jax.ShapeDtypeStruct((bsz, seq, n_total - n_main), F32))
        out_specs.append(pl.BlockSpec((None, tm, n_total - n_main), lambda b, i: (b, i, 0)))
    return pl.pallas_call(
        functools.partial(_proj_kernel, n_main=n_main, n_total=n_total),
        grid=(bsz, seq // tm),
        in_specs=[pl.BlockSpec((None, tm, d), lambda b, i: (b, i, 0)),
                  pl.BlockSpec((None, 6, d), lambda b, i: (b, 0, 0)),
                  pl.BlockSpec((d, n_total), lambda b, i: (0, 0)),
                  pl.BlockSpec((1, n_total), lambda b, i: (0, 0))],
        out_specs=out_specs,
        out_shape=out_shape,
        compiler_params=_cparams(2),
        name="modulate_project",
    )(x, mod, w, col_scale)


def _outproj_ln_kernel(o_ref, w_ref, x_ref, mod_ref, g_ref, b_ref, out_ref, *, alpha):
    y = _dot(o_ref[...], w_ref[...])
    z = alpha * x_ref[...] + mod_ref[2:3, :] * y
    out_ref[...] = _layer_norm(z, g_ref[...], b_ref[...])


def _outproj_ln(o, w_o, x, mod, ln_g, ln_b, alpha):
    bsz, seq, d = x.shape
    k = o.shape[-1]
    tm = TOKEN_TILE
    return pl.pallas_call(
        functools.partial(_outproj_ln_kernel, alpha=alpha),
        grid=(bsz, seq // tm),
        in_specs=[pl.BlockSpec((None, tm, k), lambda b, i: (b, i, 0)),
                  pl.BlockSpec((k, d), lambda b, i: (0, 0)),
                  pl.BlockSpec((None, tm, d), lambda b, i: (b, i, 0)),
                  pl.BlockSpec((None, 6, d), lambda b, i: (b, 0, 0)),
                  pl.BlockSpec((1, d), lambda b, i: (0, 0)),
                  pl.BlockSpec((1, d), lambda b, i: (0, 0))],
        out_specs=pl.BlockSpec((None, tm, d), lambda b, i: (b, i, 0)),
        out_shape=jax.ShapeDtypeStruct((bsz, seq, d), F32),
        compiler_params=_cparams(2),
        name="outproj_residual_ln",
    )(o, w_o, x, mod, ln_g.reshape(1, d), ln_b.reshape(1, d))


def _ffn_kernel(x_ref, halo_ref, mod_ref, wa_ref, wb_ref, cw_ref, cb_ref, wo_ref, g_ref, b_ref,
                out_ref, acc_ref, *, alpha):
    tm = x_ref.shape[0]
    halo_rows_to_zero = jnp.where(pl.program_id(1) == 0, SUBLANES, 0)
    shift, scale, gate = mod_ref[3:4, :], mod_ref[4:5, :], mod_ref[5:6, :]
    x = x_ref[...]
    h = (x * (1.0 + scale) + shift).astype(BF16)
    h_halo = (halo_ref[...] * (1.0 + scale) + shift).astype(BF16)
    h_ext = jnp.concatenate([h_halo, h], axis=0)
    row = lax.broadcasted_iota(jnp.int32, (SUBLANES + tm, 1), 0)
    keep = row >= halo_rows_to_zero
    acc_ref[...] = jnp.zeros_like(acc_ref)
    for c0 in range(0, D_FF, FF_CHUNK):
        c1 = c0 + FF_CHUNK
        a_ext = jnp.where(keep, _dot(h_ext, wa_ref[:, c0:c1]), 0.0)
        a_m1 = pltpu.roll(a_ext, 1, axis=0)[SUBLANES:, :]
        a_m2 = pltpu.roll(a_ext, 2, axis=0)[SUBLANES:, :]
        a = a_ext[SUBLANES:, :]
        conv = (a * cw_ref[2:3, c0:c1] + a_m1 * cw_ref[1:2, c0:c1] + a_m2 * cw_ref[0:1, c0:c1]
                + cb_ref[:, c0:c1])
        b = _dot(h, wb_ref[:, c0:c1])
        g = (_gelu_tanh(conv) * b).astype(BF16)
        acc_ref[...] += _dot(g, wo_ref[c0:c1, :])
    z = alpha * x + gate * acc_ref[...]
    out_ref[...] = _layer_norm(z, g_ref[...], b_ref[...])


def _ffn_ln(x, mod, w_a, w_b, conv_w, conv_b, w_out, ln_g, ln_b, alpha):
    bsz, seq, d = x.shape
    tm = TOKEN_TILE
    halo_blocks = tm // SUBLANES
    const = lambda b, i: (0, 0)
    return pl.pallas_call(
        functools.partial(_ffn_kernel, alpha=alpha),
        grid=(bsz, seq // tm),
        in_specs=[pl.BlockSpec((None, tm, d), lambda b, i: (b, i, 0)),
                  pl.BlockSpec((None, SUBLANES, d),
                               lambda b, i: (b, jnp.maximum(i * halo_blocks - 1, 0), 0)),
                  pl.BlockSpec((None, 6, d), lambda b, i: (b, 0, 0)),
                  pl.BlockSpec((d, D_FF), const),
                  pl.BlockSpec((d, D_FF), const),
                  pl.BlockSpec((3, D_FF), const),
                  pl.BlockSpec((1, D_FF), const),
                  pl.BlockSpec((D_FF, d), const),
                  pl.BlockSpec((1, d), const),
                  pl.BlockSpec((1, d), const)],
        out_specs=pl.BlockSpec((None, tm, d), lambda b, i: (b, i, 0)),
        out_shape=jax.ShapeDtypeStruct((bsz, seq, d), F32),
        scratch_shapes=[pltpu.VMEM((tm, d), F32)],
        compiler_params=_cparams(2),
        name="conv_ffn_residual_ln",
    )(x, x, mod, w_a, w_b, conv_w, conv_b.reshape(1, D_FF), w_out,
      ln_g.reshape(1, d), ln_b.reshape(1, d))


def _mla_proj_kernel(x_ref, mod_ref, pos_ref, invf_ref, win_ref, qn_ref, kvn_ref, wuq_ref,
                     wuqr_ref, wukv_ref, q_ref, kn_ref, kr_ref, v_ref, *, scale):
    h = (x_ref[...] * (1.0 + mod_ref[1:2, :]) + mod_ref[0:1, :]).astype(BF16)
    proj = _dot(h, win_ref[...])
    c_q = _rms_norm(proj[:, :MLA_Q_RANK], qn_ref[...]).astype(BF16)
    c_kv = _rms_norm(proj[:, MLA_Q_RANK:MLA_Q_RANK + MLA_KV_RANK], kvn_ref[...]).astype(BF16)
    ang = pos_ref[...] * invf_ref[...]
    cos, sin = jnp.cos(ang), jnp.sin(ang)
    r0 = MLA_Q_RANK + MLA_KV_RANK
    kr_ref[...] = (proj[:, r0:r0 + LANES] * cos + proj[:, r0 + LANES:r0 + 2 * LANES] * sin).astype(BF16)
    for hd in range(MLA_HEADS):
        qa = _dot(c_q, wuq_ref[:, hd * 2 * LANES:(hd + 1) * 2 * LANES])
        qr = _dot(c_q, wuqr_ref[:, hd * LANES:(hd + 1) * LANES])
        q_ref[:, hd * 2 * LANES:hd * 2 * LANES + LANES] = (qa[:, :LANES] * scale).astype(BF16)
        q_ref[:, hd * 2 * LANES + LANES:(hd + 1) * 2 * LANES] = (
            (qa[:, LANES:] * cos + qr * sin) * scale).astype(BF16)
    kv = _dot(c_kv, wukv_ref[...])
    nk = MLA_HEADS * MLA_NOPE
    kn_ref[...] = kv[:, :nk].astype(BF16)
    v_ref[...] = kv[:, nk:].astype(BF16)


def _mla_attn_kernel(q_ref, kn_ref, kr_ref, v_ref, o_ref):
    qi = pl.program_id(2)
    tq = q_ref.shape[0]
    q = q_ref[...]

    def step(j, carry, masked):
        ks = pl.multiple_of(j * ATT_TK, ATT_TK)
        k = jnp.concatenate([kn_ref[pl.ds(ks, ATT_TK), :], kr_ref[pl.ds(ks, ATT_TK), :]], axis=1)
        s = _dot_nt(q, k)
        if masked:
            row = lax.broadcasted_iota(jnp.int32, s.shape, 0)
            col = lax.broadcasted_iota(jnp.int32, s.shape, 1)
            s = jnp.where(col <= row, s, NEG)
        return _softmax_update(s, *carry, v_ref[pl.ds(ks, ATT_TK), :])

    init = (jnp.full((tq, 1), M_INIT, F32), jnp.zeros((tq, 1), F32), jnp.zeros((tq, MLA_V), F32))
    carry = lax.fori_loop(0, qi, lambda j, c: step(j, c, False), init)
    _, l, acc = step(qi, carry, True)
    o_ref[...] = (acc / jnp.maximum(l, TINY)).astype(BF16)


def _mla_mixer(x, mod, pos_col, w_in, q_norm, w_uq, kv_norm, w_ukv):
    bsz, seq, d = x.shape
    hh, dn, dr, dv = MLA_HEADS, MLA_NOPE, MLA_ROPE, MLA_V
    half = dr // 2
    r0 = MLA_Q_RANK + MLA_KV_RANK
    zpad = lambda rows, n: jnp.zeros((rows, n), F32)
    x1, x2 = w_in[:, r0:r0 + half], w_in[:, r0 + half:r0 + dr]
    w_in_ext = jnp.concatenate([w_in[:, :r0], x1, x2, zpad(d, LANES - dr),
                                -x2, x1, zpad(d, LANES - dr)], axis=1).astype(BF16)
    wq = w_uq.reshape(MLA_Q_RANK, hh, dn + dr)
    q1, q2 = wq[:, :, dn:dn + half], wq[:, :, dn + half:]
    zq = jnp.zeros((MLA_Q_RANK, hh, LANES - dr), F32)
    w_uq_main = jnp.concatenate([wq[:, :, :dn], q1, q2, zq], axis=2).reshape(MLA_Q_RANK, hh * 2 * LANES)
    w_uq_rot = jnp.concatenate([-q2, q1, zq], axis=2).reshape(MLA_Q_RANK, hh * LANES)
    wkv = w_ukv.reshape(MLA_KV_RANK, hh, dn + dv)
    w_ukv_perm = jnp.concatenate([wkv[:, :, :dn].reshape(MLA_KV_RANK, hh * dn),
                                  wkv[:, :, dn:].reshape(MLA_KV_RANK, hh * dv)], axis=1)
    inv_freq = ROPE_BASE ** (-jnp.arange(half, dtype=F32) / half)
    invf = jnp.concatenate([inv_freq, inv_freq, jnp.zeros((LANES - dr,), F32)]).reshape(1, LANES)

    tm = TOKEN_TILE
    const = lambda b, i: (0, 0)
    tok = lambda n: pl.BlockSpec((None, tm, n), lambda b, i: (b, i, 0))
    q, kn, kr, v = pl.pallas_call(
        functools.partial(_mla_proj_kernel, scale=float((dn + dr) ** -0.5)),
        grid=(bsz, seq // tm),
        in_specs=[tok(d),
                  pl.BlockSpec((None, 6, d), lambda b, i: (b, 0, 0)),
                  tok(1),
                  pl.BlockSpec((1, LANES), const),
                  pl.BlockSpec(w_in_ext.shape, const),
                  pl.BlockSpec((1, MLA_Q_RANK), const),
                  pl.BlockSpec((1, MLA_KV_RANK), const),
                  pl.BlockSpec(w_uq_main.shape, const),
                  pl.BlockSpec(w_uq_rot.shape, const),
                  pl.BlockSpec(w_ukv_perm.shape, const)],
        out_specs=[tok(hh * 2 * LANES), tok(hh * dn), tok(LANES), tok(hh * dv)],
        out_shape=[jax.ShapeDtypeStruct((bsz, seq, hh * 2 * LANES), BF16),
                   jax.ShapeDtypeStruct((bsz, seq, hh * dn), BF16),
                   jax.ShapeDtypeStruct((bsz, seq, LANES), BF16),
                   jax.ShapeDtypeStruct((bsz, seq, hh * dv), BF16)],
        compiler_params=_cparams(2),
        name="mla_project",
    )(x, mod, pos_col, invf, w_in_ext, q_norm.reshape(1, -1), kv_norm.reshape(1, -1),
      w_uq_main.astype(BF16), w_uq_rot.astype(BF16), w_ukv_perm.astype(BF16))

    tq = ATT_TQ
    return pl.pallas_call(
        _mla_attn_kernel,
        grid=(bsz, hh, seq // tq),
        in_specs=[pl.BlockSpec((None, tq, 2 * LANES), lambda b, h, i: (b, i, h)),
                  pl.BlockSpec((None, seq, dn), lambda b, h, i: (b, 0, h)),
                  pl.BlockSpec((None, seq, LANES), lambda b, h, i: (b, 0, 0)),
                  pl.BlockSpec((None, seq, dv), lambda b, h, i: (b, 0, h))],
        out_specs=pl.BlockSpec((None, tq, dv), lambda b, h, i: (b, i, h)),
        out_shape=jax.ShapeDtypeStruct((bsz, seq, hh * dv), BF16),
        compiler_params=_cparams(3),
        name="mla_attention",
    )(q, kn, kr, v)


def _moba_kernel(slope_ref, q_ref, k_ref, v_ref, pc_ref, pr_ref, o_ref, kmean_ref, *, n_blocks):
    blk = MOBA_BLOCK
    hd = pl.program_id(1)
    qi = pl.program_id(2)

    @pl.when(qi == 0)
    def _():
        kmean_ref[...] = jnp.zeros_like(kmean_ref)
        for n in range(n_blocks):
            kmean_ref[n:n + 1, :] = jnp.mean(k_ref[n * blk:(n + 1) * blk, :].astype(F32),
                                             axis=0, keepdims=True)

    q = q_ref[...]
    km = kmean_ref[...]
    km_hi = km.astype(BF16)
    km_lo = (km - km_hi.astype(F32)).astype(BF16)
    gate = _dot_nt(q, km_hi) + _dot_nt(q, km_lo)
    lane = lax.broadcasted_iota(jnp.int32, gate.shape, 1)
    past = lane < qi
    g = jnp.where(past, gate, LOW)
    rank = jnp.zeros(gate.shape, jnp.int32)
    for dlt in range(1, n_blocks):
        lower = pltpu.roll(g, dlt, axis=1)
        upper = pltpu.roll(g, LANES - dlt, axis=1)
        rank = rank + (lower >= g).astype(jnp.int32) + (upper > g).astype(jnp.int32)
    chosen = jnp.logical_or(jnp.logical_and(rank < MOBA_TOPK, past), lane == qi)
    q_aug = jnp.concatenate([q, jnp.where(chosen, 0.0, NEG).astype(BF16)], axis=1)

    slope = slope_ref[hd]
    pq = pc_ref[...]
    klane = lax.broadcasted_iota(jnp.int32, (blk, LANES), 1)

    def step(n, carry, masked):
        ks = pl.multiple_of(n * blk, blk)
        onehot = jnp.where(klane == n, 1.0, 0.0).astype(BF16)
        k_aug = jnp.concatenate([k_ref[pl.ds(ks, blk), :], onehot], axis=1)
        s = _dot_nt(q_aug, k_aug) - slope * jnp.abs(pq - pr_ref[n])
        if masked:
            row = lax.broadcasted_iota(jnp.int32, s.shape, 0)
            col = lax.broadcasted_iota(jnp.int32, s.shape, 1)
            s = jnp.where(col <= row, s, NEG)
        return _softmax_update(s, *carry, v_ref[pl.ds(ks, blk), :])

    init = (jnp.full((blk, 1), M_INIT, F32), jnp.zeros((blk, 1), F32), jnp.zeros((blk, MOBA_HD), F32))
    carry = lax.fori_loop(0, qi, lambda n, c: step(n, c, False), init)
    _, l, acc = step(qi, carry, True)
    o_ref[...] = (acc / jnp.maximum(l, TINY)).astype(BF16)


def _alibi_slopes(n):
    return 2.0 ** (-8.0 * jnp.arange(1, n + 1, dtype=F32) / n)


def _moba_mixer(x, mod, pos_col, pos_f, w_in):
    bsz, seq, d = x.shape
    hh, hd, blk = MOBA_HEADS, MOBA_HD, MOBA_BLOCK
    assert seq % blk == 0 and seq // blk + MOBA_TOPK < LANES
    n_blocks = seq // blk
    col_scale = jnp.concatenate([jnp.full((hh * hd,), hd ** -0.5, F32),
                                 jnp.ones((2 * hh * hd,), F32)]).reshape(1, -1)
    (qkv,) = _project(x, mod, w_in.astype(BF16), col_scale, 3 * hh * hd)
    pos_row = pos_f.reshape(bsz, n_blocks, 1, blk)
    return pl.pallas_call(
        functools.partial(_moba_kernel, n_blocks=n_blocks),
        grid=(bsz, hh, n_blocks),
        in_specs=[pl.BlockSpec(memory_space=pltpu.SMEM),
                  pl.BlockSpec((None, blk, hd), lambda b, h, i: (b, i, h)),
                  pl.BlockSpec((None, seq, hd), lambda b, h, i: (b, 0, hh + h)),
                  pl.BlockSpec((None, seq, hd), lambda b, h, i: (b, 0, 2 * hh + h)),
                  pl.BlockSpec((None, blk, 1), lambda b, h, i: (b, i, 0)),
                  pl.BlockSpec((None, n_blocks, 1, blk), lambda b, h, i: (b, 0, 0, 0))],
        out_specs=pl.BlockSpec((None, blk, hd), lambda b, h, i: (b, i, h)),
        out_shape=jax.ShapeDtypeStruct((bsz, seq, hh * hd), BF16),
        scratch_shapes=[pltpu.VMEM((LANES, hd), F32)],
        compiler_params=_cparams(3),
        name="moba_attention",
    )(_alibi_slopes(hh), qkv, qkv, qkv, pos_col, pos_row)


def _sb_kernel(q_ref, k_ref, v_ref, o_ref):
    qi = pl.program_id(2)
    tq, tk = q_ref.shape[0], ATT_TK
    q = q_ref[...]
    rj = lax.broadcasted_iota(jnp.int32, (tk, tk), 0)
    cs = lax.broadcasted_iota(jnp.int32, (tk, tk), 1)
    later = jnp.where(rj > cs, 1.0, 0.0).astype(BF16)

    def step(j, carry, masked):
        c, acc = carry
        ks = pl.multiple_of(j * tk, tk)
        z = _dot_nt(q, k_ref[pl.ds(ks, tk), :])
        log_beta = jnp.minimum(z, 0.0) - jnp.log1p(jnp.exp(-jnp.abs(z)))
        log_1mb = log_beta - z
        if masked:
            row = lax.broadcasted_iota(jnp.int32, z.shape, 0)
            col = lax.broadcasted_iota(jnp.int32, z.shape, 1)
            mask = col < row
            log_1mb = jnp.where(mask, log_1mb, 0.0)
        hi = log_1mb.astype(BF16)
        lo = (log_1mb - hi.astype(F32)).astype(BF16)
        tail = _dot(hi, later) + _dot(lo, later)
        a = jnp.exp(log_beta + tail + c)
        if masked:
            a = jnp.where(mask, a, 0.0)
        acc = acc + _dot(a.astype(BF16), v_ref[pl.ds(ks, tk), :])
        c = c + jnp.sum(log_1mb, axis=1, keepdims=True)
        return c, acc

    carry = step(qi, (jnp.zeros((tq, 1), F32), jnp.zeros((tq, SB_HD), F32)), True)
    _, acc = lax.fori_loop(0, qi, lambda t, cr: step(qi - 1 - t, cr, False), carry)
    o_ref[...] = acc.astype(BF16)


def _sb_mixer(x, mod, w_in):
    bsz, seq, d = x.shape
    hh, hd = SB_HEADS, SB_HD
    col_scale = jnp.concatenate([jnp.full((hh * hd,), hd ** -0.5, F32),
                                 jnp.ones((2 * hh * hd,), F32)]).reshape(1, -1)
    (qkv,) = _project(x, mod, w_in.astype(BF16), col_scale, 3 * hh * hd)
    tq = ATT_TQ
    assert tq == ATT_TK
    return pl.pallas_call(
        _sb_kernel,
        grid=(bsz, hh, seq // tq),
        in_specs=[pl.BlockSpec((None, tq, hd), lambda b, h, i: (b, i, h)),
                  pl.BlockSpec((None, seq, hd), lambda b, h, i: (b, 0, hh + h)),
                  pl.BlockSpec((None, seq, hd), lambda b, h, i: (b, 0, 2 * hh + h))],
        out_specs=pl.BlockSpec((None, tq, hd), lambda b, h, i: (b, i, h)),
        out_shape=jax.ShapeDtypeStruct((bsz, seq, hh * hd), BF16),
        compiler_params=_cparams(3),
        name="stick_breaking_attention",
    )(qkv, qkv, qkv)


def _nsa_compress_kernel(t_ref, pe_ref, w1_ref, w2_ref, o_ref, *, n_cmp):
    st, half = NSA_CMP_STRIDE, NSA_CMP_LEN // NSA_CMP_STRIDE
    assert half == 2
    rows = t_ref.shape[0] // st
    xa, xb = [], []
    for r in range(st):
        xr = t_ref[pl.ds(r, rows, stride=st), :]
        xa.append((xr + pe_ref[r:r + 1, :]).astype(BF16))
        xb.append((xr + pe_ref[st + r:st + r + 1, :]).astype(BF16))
    k_half = st * NSA_HD
    pre_a = _dot(jnp.concatenate(xa, axis=1), w1_ref[:k_half, :])
    pre_b = _dot(jnp.concatenate(xb, axis=1), w1_ref[k_half:, :])
    pre = pre_a + pltpu.roll(pre_b, rows - 1, axis=0)
    out = _dot(_gelu_tanh(pre).astype(BF16), w2_ref[...])
    row = lax.broadcasted_iota(jnp.int32, out.shape, 0)
    o_ref[...] = jnp.where(row < n_cmp, out, 0.0).astype(BF16)


def _nsa_cmp_select_kernel(slope_ref, q_ref, kc_ref, vc_ref, gt_ref, pc_ref, pe_ref, ovl_ref,
                           acc_ref, sel_ref, *, n_cmp, n_sel):
    g = pl.program_id(1)
    qi = pl.program_id(2)
    tq = q_ref.shape[0]
    shape = (tq, LANES)
    lane = lax.broadcasted_iota(jnp.int32, shape, 1)
    qidx = qi * tq + lax.broadcasted_iota(jnp.int32, shape, 0)
    c_end = lane * NSA_CMP_STRIDE + (NSA_CMP_LEN - 1)
    mask_c = jnp.logical_and(c_end <= qidx, lane < n_cmp)
    dist = jnp.abs(pc_ref[...] - pe_ref[...])
    sig = jax.nn.sigmoid(gt_ref[...])
    kc, vc, ovl = kc_ref[...], vc_ref[...], ovl_ref[...]
    imp = jnp.zeros(shape, F32)
    for r in range(NSA_R):
        s = _dot_nt(q_ref[:, r * NSA_HD:(r + 1) * NSA_HD], kc) - slope_ref[g * NSA_R + r] * dist
        s = jnp.where(mask_c, s, NEG)
        p = jnp.where(mask_c, jnp.exp(s - jnp.max(s, axis=1, keepdims=True)), 0.0)
        p = (p / jnp.maximum(jnp.sum(p, axis=1, keepdims=True), TINY)).astype(BF16)
        imp = imp + _dot(p, ovl)
        gate = jnp.sum(jnp.where(lane == g * NSA_R + r, sig, 0.0), axis=1, keepdims=True)
        acc_ref[:, r * NSA_HD:(r + 1) * NSA_HD] = gate * _dot(p, vc)

    q_blk = qidx >> 6
    assert NSA_SEL_BLOCK == 64
    forced = jnp.logical_or(lane == 0, jnp.logical_or(lane == q_blk, lane == q_blk - 1))
    cand = lane <= q_blk
    score = jnp.where(cand, imp + jnp.where(forced, NSA_FORCE_BONUS, 0.0), NEG)
    score = jnp.where(lane < n_sel, score, LOW)
    rank = jnp.zeros(shape, jnp.int32)
    for dlt in range(1, n_sel):
        lower = pltpu.roll(score, dlt, axis=1)
        upper = pltpu.roll(score, LANES - dlt, axis=1)
        rank = rank + (lower >= score).astype(jnp.int32) + (upper > score).astype(jnp.int32)
    chosen = jnp.logical_and(rank < NSA_SEL_TOPN, cand)
    sel_ref[...] = jnp.where(chosen, 0.0, NEG).astype(BF16)


def _nsa_group_rows(ref, tq):
    return jnp.concatenate([ref[:, r * NSA_HD:(r + 1) * NSA_HD] for r in range(NSA_R)], axis=0)


def _nsa_slope_col(slope_ref, g, tq):
    row = lax.broadcasted_iota(jnp.int32, (NSA_R * tq, 1), 0)
    col = jnp.zeros((NSA_R * tq, 1), F32)
    for r in range(NSA_R):
        col = jnp.where(jnp.logical_and(row >= r * tq, row < (r + 1) * tq), slope_ref[g * NSA_R + r], col)
    return col


def _nsa_gate_cols(gt_ref, branch, g, tq):
    sig = jax.nn.sigmoid(gt_ref[...])
    lane = lax.broadcasted_iota(jnp.int32, sig.shape, 1)
    cols = [jnp.sum(jnp.where(lane == branch * NSA_HEADS + g * NSA_R + r, sig, 0.0), axis=1, keepdims=True)
            for r in range(NSA_R)]
    return jnp.concatenate(cols, axis=0)


def _nsa_select_kernel(slope_ref, q_ref, sel_ref, k_ref, v_ref, gt_ref, pc_ref, pr_ref, prev_ref,
                       out_ref):
    g = pl.program_id(1)
    qi = pl.program_id(2)
    tq, tk = q_ref.shape[0], NSA_TK
    rows = NSA_R * tq
    q4 = _nsa_group_rows(q_ref, tq)
    q_aug = jnp.concatenate([q4, jnp.concatenate([sel_ref[...]] * NSA_R, axis=0)], axis=1)
    pq4 = jnp.concatenate([pc_ref[...]] * NSA_R, axis=0)
    slope = _nsa_slope_col(slope_ref, g, tq)
    klane = lax.broadcasted_iota(jnp.int32, (tk, LANES), 1)
    krow = lax.broadcasted_iota(jnp.int32, (tk, LANES), 0)
    per_tile = tk // NSA_SEL_BLOCK
    qrow = lax.broadcasted_iota(jnp.int32, (rows, tk), 0)
    qidx = qi * tq + (qrow & (tq - 1))
    kcol = lax.broadcasted_iota(jnp.int32, (rows, tk), 1)

    def step(j, carry, masked):
        ks = pl.multiple_of(j * tk, tk)
        onehot = jnp.where(klane == j * per_tile + (krow >> 6), 1.0, 0.0).astype(BF16)
        k_aug = jnp.concatenate([k_ref[pl.ds(ks, tk), :], onehot], axis=1)
        s = _dot_nt(q_aug, k_aug) - slope * jnp.abs(pq4 - pr_ref[j])
        if masked:
            s = jnp.where(j * tk + kcol <= qidx, s, NEG)
        return _softmax_update(s, *carry, v_ref[pl.ds(ks, tk), :])

    init = (jnp.full((rows, 1), M_INIT, F32), jnp.zeros((rows, 1), F32), jnp.zeros((rows, NSA_HD), F32))
    diag = (qi * tq) // tk
    carry = lax.fori_loop(0, diag, lambda j, c: step(j, c, False), init)
    _, l, acc = step(diag, carry, True)
    o4 = _nsa_gate_cols(gt_ref, 1, g, tq) * (acc / jnp.maximum(l, TINY))
    for r in range(NSA_R):
        out_ref[:, r * NSA_HD:(r + 1) * NSA_HD] = (prev_ref[:, r * NSA_HD:(r + 1) * NSA_HD]
                                                    + o4[r * tq:(r + 1) * tq, :])


def _nsa_window_kernel(slope_ref, q_ref, k_ref, v_ref, gt_ref, pc_ref, pr_ref, prev_ref, out_ref):
    g = pl.program_id(1)
    qi = pl.program_id(2)
    tq = q_ref.shape[0]
    rows = NSA_R * tq
    span = NSA_WINDOW + tq
    start = pl.multiple_of(jnp.maximum(qi * tq - NSA_WINDOW, 0), tq)
    q4 = _nsa_group_rows(q_ref, tq)
    pq4 = jnp.concatenate([pc_ref[...]] * NSA_R, axis=0)
    kpos = pr_ref[:, pl.ds(start, span)]
    s = _dot_nt(q4, k_ref[pl.ds(start, span), :])
    s = s - _nsa_slope_col(slope_ref, g, tq) * jnp.abs(pq4 - kpos)
    qidx = qi * tq + (lax.broadcasted_iota(jnp.int32, (rows, span), 0) & (tq - 1))
    kidx = start + lax.broadcasted_iota(jnp.int32, (rows, span), 1)
    mask = jnp.logical_and(kidx <= qidx, qidx - kidx < NSA_WINDOW)
    s = jnp.where(mask, s, NEG)
    p = jnp.where(mask, jnp.exp(s - jnp.max(s, axis=1, keepdims=True)), 0.0)
    p = p / jnp.maximum(jnp.sum(p, axis=1, keepdims=True), TINY)
    o4 = _nsa_gate_cols(gt_ref, 2, g, tq) * _dot(p.astype(BF16), v_ref[pl.ds(start, span), :])
    for r in range(NSA_R):
        out_ref[:, r * NSA_HD:(r + 1) * NSA_HD] = (
            prev_ref[:, r * NSA_HD:(r + 1) * NSA_HD] + o4[r * tq:(r + 1) * tq, :]).astype(BF16)


def _nsa_mixer(x, mod, pos_col, pos_f, w_in, cmp_pos, cmp_w1, cmp_w2):
    bsz, seq, d = x.shape
    hh, gg, hd, rr = NSA_HEADS, NSA_GROUPS, NSA_HD, NSA_R
    n_cmp = (seq - NSA_CMP_LEN) // NSA_CMP_STRIDE + 1
    n_sel = seq // NSA_SEL_BLOCK
    assert seq % NSA_TK == 0 and seq // NSA_CMP_STRIDE == LANES and n_sel + n_sel <= LANES
    assert NSA_SEL_TOPN <= n_sel and seq >= NSA_WINDOW + NSA_TQ
    nq, nkv = hh * hd, gg * hd
    kv = lambda t: w_in[:, nq + t * nkv:nq + (t + 1) * nkv]
    n_gate = 3 * hh
    w_perm = jnp.concatenate([w_in[:, :nq], kv(2), kv(3), kv(4), kv(5), kv(0), kv(1),
                              w_in[:, nq + 6 * nkv:], jnp.zeros((d, LANES - n_gate), F32)], axis=1)
    n_main = nq + 4 * nkv
    col_scale = jnp.concatenate([jnp.full((nq,), hd ** -0.5, F32),
                                 jnp.ones((w_perm.shape[1] - nq,), F32)]).reshape(1, -1)
    main, tail = _project(x, mod, w_perm.astype(BF16), col_scale, n_main)
    slopes = _alibi_slopes(hh)
    gate_blk = 2 * nkv // LANES

    cmp_kv = pl.pallas_call(
        functools.partial(_nsa_compress_kernel, n_cmp=n_cmp),
        grid=(bsz, 2, gg),
        in_specs=[pl.BlockSpec((None, seq, hd), lambda b, w, g: (b, 0, w * gg + g)),
                  pl.BlockSpec((None, NSA_CMP_LEN, hd), lambda b, w, g: (w, 0, 0)),
                  pl.BlockSpec((None, NSA_CMP_LEN * hd, hd), lambda b, w, g: (w, 0, 0)),
                  pl.BlockSpec((None, hd, hd), lambda b, w, g: (w, 0, 0))],
        out_specs=pl.BlockSpec((None, None, LANES, hd), lambda b, w, g: (b, w * gg + g, 0, 0)),
        out_shape=jax.ShapeDtypeStruct((bsz, 2 * gg, LANES, hd), BF16),
        compiler_params=_cparams(3),
        name="nsa_compress",
    )(tail, cmp_pos, cmp_w1.astype(BF16), cmp_w2.astype(BF16))

    pos_cend = jnp.pad(pos_f[:, NSA_CMP_LEN - 1::NSA_CMP_STRIDE], ((0, 0), (0, LANES - n_cmp)))
    pos_cend = pos_cend.reshape(bsz, 1, LANES)
    c_start = jnp.arange(LANES) * NSA_CMP_STRIDE
    j_start = jnp.arange(LANES) * NSA_SEL_BLOCK
    overlap = ((c_start[:, None] < j_start[None, :] + NSA_SEL_BLOCK)
               & (c_start[:, None] + NSA_CMP_LEN - 1 >= j_start[None, :])
               & (jnp.arange(LANES)[:, None] < n_cmp) & (jnp.arange(LANES)[None, :] < n_sel))
    overlap = overlap.astype(BF16)

    tq = NSA_TQ
    smem = pl.BlockSpec(memory_space=pltpu.SMEM)
    q_spec = pl.BlockSpec((None, tq, rr * hd), lambda b, g, i: (b, i, g))
    gate_spec = pl.BlockSpec((None, tq, LANES), lambda b, g, i: (b, i, gate_blk))
    pc_spec = pl.BlockSpec((None, tq, 1), lambda b, g, i: (b, i, 0))
    acc_spec = pl.BlockSpec((None, tq, rr * hd), lambda b, g, i: (b, i, g))
    kv_spec = lambda t: pl.BlockSpec((None, seq, hd), lambda b, g, i: (b, 0, nq // hd + t * gg + g))
    grid = (bsz, gg, seq // tq)

    acc0, sel_bias = pl.pallas_call(
        functools.partial(_nsa_cmp_select_kernel, n_cmp=n_cmp, n_sel=n_sel),
        grid=grid,
        in_specs=[smem, q_spec,
                  pl.BlockSpec((None, None, LANES, hd), lambda b, g, i: (b, g, 0, 0)),
                  pl.BlockSpec((None, None, LANES, hd), lambda b, g, i: (b, gg + g, 0, 0)),
                  gate_spec, pc_spec,
                  pl.BlockSpec((None, 1, LANES), lambda b, g, i: (b, 0, 0)),
                  pl.BlockSpec((LANES, LANES), lambda b, g, i: (0, 0))],
        out_specs=[acc_spec, pl.BlockSpec((None, None, tq, LANES), lambda b, g, i: (b, g, i, 0))],
        out_shape=[jax.ShapeDtypeStruct((bsz, seq, hh * hd), F32),
                   jax.ShapeDtypeStruct((bsz, gg, seq, LANES), BF16)],
        compiler_params=_cparams(3),
        name="nsa_compressed_attention_select",
    )(slopes, main, cmp_kv, cmp_kv, tail, pos_col, pos_cend, overlap)

    pos_tiles = pos_f.reshape(bsz, seq // NSA_TK, 1, NSA_TK)
    acc1 = pl.pallas_call(
        _nsa_select_kernel,
        grid=grid,
        in_specs=[smem, q_spec,
                  pl.BlockSpec((None, None, tq, LANES), lambda b, g, i: (b, g, i, 0)),
                  kv_spec(0), kv_spec(1), gate_spec, pc_spec,
                  pl.BlockSpec((None, seq // NSA_TK, 1, NSA_TK), lambda b, g, i: (b, 0, 0, 0)),
                  acc_spec],
        out_specs=acc_spec,
        out_shape=jax.ShapeDtypeStruct((bsz, seq, hh * hd), F32),
        compiler_params=_cparams(3),
        name="nsa_selected_attention",
    )(slopes, main, sel_bias, main, main, tail, pos_col, pos_tiles, acc0)

    return pl.pallas_call(
        _nsa_window_kernel,
        grid=grid,
        in_specs=[smem, q_spec, kv_spec(2), kv_spec(3), gate_spec, pc_spec,
                  pl.BlockSpec((None, 1, seq), lambda b, g, i: (b, 0, 0)),
                  acc_spec],
        out_specs=acc_spec,
        out_shape=jax.ShapeDtypeStruct((bsz, seq, hh * hd), BF16),
        compiler_params=_cparams(3),
        name="nsa_window_attention",
    )(slopes, main, main, main, tail, pos_col, pos_f.reshape(bsz, 1, seq), acc1)


def kernel(x, c, positions, mod_w, mod_b, ln_g, ln_b, ffn_w_in, ffn_conv_w, ffn_conv_b, ffn_w_out,
           mla_w_in, mla_q_norm, mla_w_uq, mla_kv_norm, mla_w_ukv, mla_w_o,
           moba_w_in, moba_w_o, nsa_w_in, nsa_cmp_pos, nsa_cmp_w1, nsa_cmp_w2, nsa_w_o,
           sb_w_in, sb_w_o):
    bsz, seq, d = x.shape
    depth = mod_w.shape[0]
    assert d == D_MODEL and seq % TOKEN_TILE == 0 and seq % ATT_TQ == 0
    alpha = float((2 * depth) ** 0.25)
    pos_f = positions.astype(F32)
    pos_col = pos_f.reshape(bsz, seq, 1)
    mod_all = _modulation(c, mod_w, mod_b)
    for i in range(depth):
        kind, j = i % N_MIXERS, i // N_MIXERS
        mod = mod_all[i]
        if kind == 0:
            o = _mla_mixer(x, mod, pos_col, mla_w_in[j], mla_q_norm[j], mla_w_uq[j], mla_kv_norm[j],
                           mla_w_ukv[j])
            w_o = mla_w_o[j]
        elif kind == 1:
            o = _moba_mixer(x, mod, pos_col, pos_f, moba_w_in[j])
            w_o = moba_w_o[j]
        elif kind == 2:
            o = _nsa_mixer(x, mod, pos_col, pos_f, nsa_w_in[j], nsa_cmp_pos[j], nsa_cmp_w1[j],
                           nsa_cmp_w2[j])
            w_o = nsa_w_o[j]
        else:
            o = _sb_mixer(x, mod, sb_w_in[j])
            w_o = sb_w_o[j]
        x = _outproj_ln(o, w_o.astype(BF16), x, mod, ln_g[i, 0], ln_b[i, 0], alpha)
        w_ffn = ffn_w_in[i].astype(BF16)
        x = _ffn_ln(x, mod, w_ffn[:, :D_FF], w_ffn[:, D_FF:], ffn_conv_w[i], ffn_conv_b[i],
                    ffn_w_out[i].astype(BF16), ln_g[i, 1], ln_b[i, 1], alpha)
    return x
```

```python
import functools

import jax
import jax.numpy as jnp
from jax import lax
from jax.experimental import pallas as pl
from jax.experimental.pallas import tpu as pltpu

F32 = jnp.float32
BF16 = jnp.bfloat16

D_MODEL = 1024
N_MIXERS = 4
MLA_HEADS, MLA_NOPE, MLA_ROPE, MLA_V = 8, 128, 64, 128
MLA_Q_RANK, MLA_KV_RANK = 256, 256
ROPE_BASE = 10000.0
MOBA_HEADS, MOBA_HD, MOBA_BLOCK, MOBA_TOPK = 8, 128, 256, 3
NSA_HEADS, NSA_GROUPS, NSA_HD = 8, 2, 128
NSA_R = NSA_HEADS // NSA_GROUPS
NSA_CMP_LEN, NSA_CMP_STRIDE, NSA_SEL_BLOCK, NSA_SEL_TOPN, NSA_WINDOW = 32, 16, 64, 16, 512
NSA_FORCE_BONUS = 100.0
SB_HEADS, SB_HD = 8, 128
D_FF = 2816
LN_EPS = 1e-5
RMS_EPS = 1e-6
NEG = -1e30
M_INIT = -1e29
LOW = -3e38
TINY = 1e-30
LOG2E = 1.4426950408889634

LANES = 128
SUBLANES = 8
VMEM_LIMIT_BYTES = 56 * 1024 * 1024

TOKEN_TILE = 512
ATT_TQ = 256
ATT_TK = 256
HEADS_PER_STEP = 8
NSA_TQ = 128
NSA_TK = 256
FF_CHUNK = 256
PROJ_CHUNK = 512


def _cparams(n_grid):
    return pltpu.CompilerParams(dimension_semantics=("arbitrary",) * n_grid,
                                vmem_limit_bytes=VMEM_LIMIT_BYTES)


def _dot(a, b):
    return jnp.dot(a, b, preferred_element_type=F32)


def _dot_nt(a, b):
    return lax.dot_general(a, b, (((1,), (1,)), ((), ())), preferred_element_type=F32)


def _layer_norm(z, g, b):
    mu = jnp.mean(z, axis=-1, keepdims=True)
    d = z - mu
    var = jnp.mean(d * d, axis=-1, keepdims=True)
    return d * lax.rsqrt(var + LN_EPS) * g + b


def _rms_norm(z, g):
    return z * lax.rsqrt(jnp.mean(z * z, axis=-1, keepdims=True) + RMS_EPS) * g


def _gelu_tanh(x):
    return 0.5 * x * (1.0 + jnp.tanh(0.7978845608028654 * (x + 0.044715 * (x * x * x))))


def _softmax_update(s, m, l, acc, v):
    m_new = jnp.maximum(m, jnp.max(s, axis=1, keepdims=True))
    alpha = jnp.exp(m - m_new)
    p = jnp.exp(s - m_new)
    l = alpha * l + jnp.sum(p, axis=1, keepdims=True)
    acc = alpha * acc + _dot(p.astype(BF16), v)
    return m_new, l, acc


def _softmax_stats_t(s_t, m, l):
    m_new = jnp.maximum(m, jnp.max(s_t, axis=0, keepdims=True))
    alpha = jnp.exp2(m - m_new)
    p = jnp.exp2(s_t - m_new)
    l = alpha * l + jnp.sum(p, axis=0, keepdims=True)
    return m_new, l, alpha, p.astype(BF16)


def _mod_kernel(c_ref, w_ref, b_ref, o_ref):
    c = c_ref[...]
    c_act = c * jax.nn.sigmoid(c)
    o_ref[...] = jnp.dot(c_act, w_ref[...], preferred_element_type=F32,
                         precision=lax.Precision.HIGHEST) + b_ref[...]


def _modulation(c, mod_w, mod_b):
    depth, d, n = mod_w.shape
    bsz = c.shape[0]
    tn = 1536
    out = pl.pallas_call(
        _mod_kernel,
        grid=(depth, n // tn),
        in_specs=[pl.BlockSpec((bsz, d), lambda l, j: (0, 0)),
                  pl.BlockSpec((None, d, tn), lambda l, j: (l, 0, j)),
                  pl.BlockSpec((None, 1, tn), lambda l, j: (l, 0, j))],
        out_specs=pl.BlockSpec((None, bsz, tn), lambda l, j: (l, 0, j)),
        out_shape=jax.ShapeDtypeStruct((depth, bsz, n), F32),
        compiler_params=_cparams(2),
        name="modulation",
    )(c, mod_w, mod_b.reshape(depth, 1, n))
    return out.reshape(depth, bsz, 6, d)


def _proj_kernel(x_ref, mod_ref, w_ref, cs_ref, wt_ref, *out_refs, n_main, n_total):
    h = (x_ref[...] * (1.0 + mod_ref[1:2, :]) + mod_ref[0:1, :]).astype(BF16)
    for n0 in range(0, n_total, PROJ_CHUNK):
        n1 = min(n0 + PROJ_CHUNK, n_total)
        y = _dot(h, w_ref[:, n0:n1]) * cs_ref[:, n0:n1]
        if n0 < n_main:
            out_refs[0][:, n0:n1] = y.astype(BF16)
        else:
            out_refs[2][:, n0 - n_main:n1 - n_main] = y
    vt_ref = out_refs[1]
    for n0 in range(0, wt_ref.shape[0], PROJ_CHUNK):
        y_t = _dot_nt(wt_ref[n0:n0 + PROJ_CHUNK, :], h).astype(BF16)
        for t in range(vt_ref.shape[0]):
            vt_ref[t, n0:n0 + PROJ_CHUNK, :] = y_t[:, t * ATT_TK:(t + 1) * ATT_TK]


def _project(x, mod, w, col_scale, n_main, w_t):
    bsz, seq, d = x.shape
    n_total, n_t = w.shape[1], w_t.shape[0]
    assert n_main % PROJ_CHUNK == 0 and n_total % LANES == 0 and n_t % PROJ_CHUNK == 0
    tm = TOKEN_TILE
    out_shape = [jax.ShapeDtypeStruct((bsz, seq, n_main), BF16),
                 jax.ShapeDtypeStruct((bsz, seq // ATT_TK, n_t, ATT_TK), BF16)]
    out_specs = [pl.BlockSpec((None, tm, n_main), lambda b, i: (b, i, 0)),
                 pl.BlockSpec((None, tm // ATT_TK, n_t, ATT_TK), lambda b, i: (b, i, 0, 0))]
    if n_main < n_total:
        out_shape.append(jax.ShapeDtypeStruct((bsz, seq, n_total - n_main), F32))
        out_specs.append(pl.BlockSpec((None, tm, n_total - n_main), lambda b, i: (b, i, 0)))
    return pl.pallas_call(
        functools.partial(_proj_kernel, n_main=n_main, n_total=n_total),
        grid=(bsz, seq // tm),
        in_specs=[pl.BlockSpec((None, tm, d), lambda b, i: (b, i, 0)),
                  pl.BlockSpec((None, 6, d), lambda b, i: (b, 0, 0)),
                  pl.BlockSpec((d, n_total), lambda b, i: (0, 0)),
                  pl.BlockSpec((1, n_total), lambda b, i: (0, 0)),
                  pl.BlockSpec((n_t, d), lambda b, i: (0, 0))],
        out_specs=out_specs,
        out_shape=out_shape,
        compiler_params=_cparams(2),
        name="modulate_project",
    )(x, mod, w, col_scale, w_t)


def _outproj_ln_kernel(o_ref, w_ref, x_ref, mod_ref, g_ref, b_ref, out_ref, *, alpha):
    y = _dot(o_ref[...], w_ref[...])
    z = alpha * x_ref[...] + mod_ref[2:3, :] * y
    out_ref[...] = _layer_norm(z, g_ref[...], b_ref[...])


def _outproj_ln(o, w_o, x, mod, ln_g, ln_b, alpha):
    bsz, seq, d = x.shape
    k = o.shape[-1]
    tm = TOKEN_TILE
    return pl.pallas_call(
        functools.partial(_outproj_ln_kernel, alpha=alpha),
        grid=(bsz, seq // tm),
        in_specs=[pl.BlockSpec((None, tm, k), lambda b, i: (b, i, 0)),
                  pl.BlockSpec((k, d), lambda b, i: (0, 0)),
                  pl.BlockSpec((None, tm, d), lambda b, i: (b, i, 0)),
                  pl.BlockSpec((None, 6, d), lambda b, i: (b, 0, 0)),
                  pl.BlockSpec((1, d), lambda b, i: (0, 0)),
                  pl.BlockSpec((1, d), lambda b, i: (0, 0))],
        out_specs=pl.BlockSpec((None, tm, d), lambda b, i: (b, i, 0)),
        out_shape=jax.ShapeDtypeStruct((bsz, seq, d), F32),
        compiler_params=_cparams(2),
        name="outproj_residual_ln",
    )(o, w_o, x, mod, ln_g.reshape(1, d), ln_b.reshape(1, d))


def _ffn_kernel(x_ref, halo_ref, mod_ref, wa_ref, wb_ref, cw_ref, cb_ref, wo_ref, g_ref, b_ref,
                out_ref, acc_ref, *, alpha):
    tm = x_ref.shape[0]
    halo_rows_to_zero = jnp.where(pl.program_id(1) == 0, SUBLANES, 0)
    shift, scale, gate = mod_ref[3:4, :], mod_ref[4:5, :], mod_ref[5:6, :]
    x = x_ref[...]
    h = (x * (1.0 + scale) + shift).astype(BF16)
    h_halo = (halo_ref[...] * (1.0 + scale) + shift).astype(BF16)
    h_ext = jnp.concatenate([h_halo, h], axis=0)
    row = lax.broadcasted_iota(jnp.int32, (SUBLANES + tm, 1), 0)
    keep = row >= halo_rows_to_zero
    acc_ref[...] = jnp.zeros_like(acc_ref)
    for c0 in range(0, D_FF, FF_CHUNK):
        c1 = c0 + FF_CHUNK
        a_ext = jnp.where(keep, _dot(h_ext, wa_ref[:, c0:c1]), 0.0)
        a_m1 = pltpu.roll(a_ext, 1, axis=0)[SUBLANES:, :]
        a_m2 = pltpu.roll(a_ext, 2, axis=0)[SUBLANES:, :]
        a = a_ext[SUBLANES:, :]
        conv = (a * cw_ref[2:3, c0:c1] + a_m1 * cw_ref[1:2, c0:c1] + a_m2 * cw_ref[0:1, c0:c1]
                + cb_ref[:, c0:c1])
        b = _dot(h, wb_ref[:, c0:c1])
        g = (_gelu_tanh(conv) * b).astype(BF16)
        acc_ref[...] += _dot(g, wo_ref[c0:c1, :])
    z = alpha * x + gate * acc_ref[...]
    out_ref[...] = _layer_norm(z, g_ref[...], b_ref[...])


def _ffn_ln(x, mod, w_a, w_b, conv_w, conv_b, w_out, ln_g, ln_b, alpha):
    bsz, seq, d = x.shape
    tm = TOKEN_TILE
    halo_blocks = tm // SUBLANES
    const = lambda b, i: (0, 0)
    return pl.pallas_call(
        functools.partial(_ffn_kernel, alpha=alpha),
        grid=(bsz, seq // tm),
        in_specs=[pl.BlockSpec((None, tm, d), lambda b, i: (b, i, 0)),
                  pl.BlockSpec((None, SUBLANES, d),
                               lambda b, i: (b, jnp.maximum(i * halo_blocks - 1, 0), 0)),
                  pl.BlockSpec((None, 6, d), lambda b, i: (b, 0, 0)),
                  pl.BlockSpec((d, D_FF), const),
                  pl.BlockSpec((d, D_FF), const),
                  pl.BlockSpec((3, D_FF), const),
                  pl.BlockSpec((1, D_FF), const),
                  pl.BlockSpec((D_FF, d), const),
                  pl.BlockSpec((1, d), const),
                  pl.BlockSpec((1, d), const)],
        out_specs=pl.BlockSpec((None, tm, d), lambda b, i: (b, i, 0)),
        out_shape=jax.ShapeDtypeStruct((bsz, seq, d), F32),
        scratch_shapes=[pltpu.VMEM((tm, d), F32)],
        compiler_params=_cparams(2),
        name="conv_ffn_residual_ln",
    )(x, x, mod, w_a, w_b, conv_w, conv_b.reshape(1, D_FF), w_out,
      ln_g.reshape(1, d), ln_b.reshape(1, d))


def _mla_proj_kernel(x_ref, mod_ref, pos_ref, invf_ref, win_ref, qn_ref, kvn_ref, wuq_ref,
                     wuqr_ref, wuk_ref, wuvt_ref, q_ref, kn_ref, kr_ref, vt_ref, *, scale):
    h = (x_ref[...] * (1.0 + mod_ref[1:2, :]) + mod_ref[0:1, :]).astype(BF16)
    proj = _dot(h, win_ref[...])
    c_q = _rms_norm(proj[:, :MLA_Q_RANK], qn_ref[...]).astype(BF16)
    c_kv = _rms_norm(proj[:, MLA_Q_RANK:MLA_Q_RANK + MLA_KV_RANK], kvn_ref[...]).astype(BF16)
    ang = pos_ref[...] * invf_ref[...]
    cos, sin = jnp.cos(ang), jnp.sin(ang)
    r0 = MLA_Q_RANK + MLA_KV_RANK
    kr_ref[...] = (proj[:, r0:r0 + LANES] * cos + proj[:, r0 + LANES:r0 + 2 * LANES] * sin).astype(BF16)
    for hd in range(MLA_HEADS):
        qa = _dot(c_q, wuq_ref[:, hd * 2 * LANES:(hd + 1) * 2 * LANES])
        qr = _dot(c_q, wuqr_ref[:, hd * LANES:(hd + 1) * LANES])
        q_ref[:, hd * 2 * LANES:hd * 2 * LANES + LANES] = (qa[:, :LANES] * scale).astype(BF16)
        q_ref[:, hd * 2 * LANES + LANES:(hd + 1) * 2 * LANES] = (
            (qa[:, LANES:] * cos + qr * sin) * scale).astype(BF16)
    kn_ref[...] = _dot(c_kv, wuk_ref[...]).astype(BF16)
    v_t = _dot_nt(wuvt_ref[...], c_kv).astype(BF16)
    for t in range(vt_ref.shape[0]):
        vt_ref[t] = v_t[:, t * ATT_TK:(t + 1) * ATT_TK]


def _mla_attn_kernel(q_ref, kn_ref, kr_ref, vt_ref, o_ref):
    qi = pl.program_id(2)
    tq = q_ref.shape[0]
    hp = HEADS_PER_STEP

    def step(j, carry, masked):
        ks = pl.multiple_of(j * ATT_TK, ATT_TK)
        kr = kr_ref[pl.ds(ks, ATT_TK), :]
        def scores(h):
            k = jnp.concatenate([kn_ref[pl.ds(ks, ATT_TK), h * MLA_NOPE:(h + 1) * MLA_NOPE], kr], axis=1)
            s_t = _dot_nt(k, q_ref[:, h * 2 * LANES:(h + 1) * 2 * LANES])
            if masked:
                key = lax.broadcasted_iota(jnp.int32, s_t.shape, 0)
                qry = lax.broadcasted_iota(jnp.int32, s_t.shape, 1)
                s_t = jnp.where(key <= qry, s_t, NEG)
            return s_t

        s_all = [scores(h) for h in range(hp)]
        stats = [_softmax_stats_t(s_all[h], carry[h][0], carry[h][1]) for h in range(hp)]
        out = []
        for h in range(hp):
            m_new, l, alpha, p = stats[h]
            out.append((m_new, l, alpha * carry[h][2] + _dot(vt_ref[j, h * MLA_V:(h + 1) * MLA_V, :], p)))
        return tuple(out)

    init = tuple((jnp.full((1, tq), M_INIT, F32), jnp.zeros((1, tq), F32), jnp.zeros((MLA_V, tq), F32))
                 for _ in range(hp))
    carry = lax.fori_loop(0, qi, lambda j, c: step(j, c, False), init)
    carry = step(qi, carry, True)
    for h in range(hp):
        _, l, acc_t = carry[h]
        o_ref[:, h * MLA_V:(h + 1) * MLA_V] = (acc_t / jnp.maximum(l, TINY)).T.astype(BF16)


def _mla_mixer(x, mod, pos_col, w_in, q_norm, w_uq, kv_norm, w_ukv):
    bsz, seq, d = x.shape
    hh, dn, dr, dv = MLA_HEADS, MLA_NOPE, MLA_ROPE, MLA_V
    half = dr // 2
    r0 = MLA_Q_RANK + MLA_KV_RANK
    zpad = lambda rows, n: jnp.zeros((rows, n), F32)
    x1, x2 = w_in[:, r0:r0 + half], w_in[:, r0 + half:r0 + dr]
    w_in_ext = jnp.concatenate([w_in[:, :r0], x1, x2, zpad(d, LANES - dr),
                                -x2, x1, zpad(d, LANES - dr)], axis=1).astype(BF16)
    wq = w_uq.reshape(MLA_Q_RANK, hh, dn + dr)
    q1, q2 = wq[:, :, dn:dn + half], wq[:, :, dn + half:]
    zq = jnp.zeros((MLA_Q_RANK, hh, LANES - dr), F32)
    w_uq_main = jnp.concatenate([wq[:, :, :dn], q1, q2, zq], axis=2).reshape(MLA_Q_RANK, hh * 2 * LANES)
    w_uq_rot = jnp.concatenate([-q2, q1, zq], axis=2).reshape(MLA_Q_RANK, hh * LANES)
    wkv = w_ukv.reshape(MLA_KV_RANK, hh, dn + dv)
    w_uk = wkv[:, :, :dn].reshape(MLA_KV_RANK, hh * dn)
    w_uv_t = wkv[:, :, dn:].reshape(MLA_KV_RANK, hh * dv).T
    inv_freq = ROPE_BASE ** (-jnp.arange(half, dtype=F32) / half)
    invf = jnp.concatenate([inv_freq, inv_freq, jnp.zeros((LANES - dr,), F32)]).reshape(1, LANES)

    tm = TOKEN_TILE
    const = lambda b, i: (0, 0)
    tok = lambda n: pl.BlockSpec((None, tm, n), lambda b, i: (b, i, 0))
    q, kn, kr, v_t = pl.pallas_call(
        functools.partial(_mla_proj_kernel, scale=float((dn + dr) ** -0.5) * LOG2E),
        grid=(bsz, seq // tm),
        in_specs=[tok(d),
                  pl.BlockSpec((None, 6, d), lambda b, i: (b, 0, 0)),
                  tok(1),
                  pl.BlockSpec((1, LANES), const),
                  pl.BlockSpec(w_in_ext.shape, const),
                  pl.BlockSpec((1, MLA_Q_RANK), const),
                  pl.BlockSpec((1, MLA_KV_RANK), const),
                  pl.BlockSpec(w_uq_main.shape, const),
                  pl.BlockSpec(w_uq_rot.shape, const),
                  pl.BlockSpec(w_uk.shape, const),
                  pl.BlockSpec(w_uv_t.shape, const)],
        out_specs=[tok(hh * 2 * LANES), tok(hh * dn), tok(LANES),
                   pl.BlockSpec((None, tm // ATT_TK, hh * dv, ATT_TK), lambda b, i: (b, i, 0, 0))],
        out_shape=[jax.ShapeDtypeStruct((bsz, seq, hh * 2 * LANES), BF16),
                   jax.ShapeDtypeStruct((bsz, seq, hh * dn), BF16),
                   jax.ShapeDtypeStruct((bsz, seq, LANES), BF16),
                   jax.ShapeDtypeStruct((bsz, seq // ATT_TK, hh * dv, ATT_TK), BF16)],
        compiler_params=_cparams(2),
        name="mla_project",
    )(x, mod, pos_col, invf, w_in_ext, q_norm.reshape(1, -1), kv_norm.reshape(1, -1),
      w_uq_main.astype(BF16), w_uq_rot.astype(BF16), w_uk.astype(BF16), w_uv_t.astype(BF16))

    tq, hp = ATT_TQ, HEADS_PER_STEP
    return pl.pallas_call(
        _mla_attn_kernel,
        grid=(bsz, hh // hp, seq // tq),
        in_specs=[pl.BlockSpec((None, tq, hp * 2 * LANES), lambda b, h, i: (b, i, h)),
                  pl.BlockSpec((None, seq, hp * dn), lambda b, h, i: (b, 0, h)),
                  pl.BlockSpec((None, seq, LANES), lambda b, h, i: (b, 0, 0)),
                  pl.BlockSpec((None, seq // ATT_TK, hp * dv, ATT_TK), lambda b, h, i: (b, 0, h, 0))],
        out_specs=pl.BlockSpec((None, tq, hp * dv), lambda b, h, i: (b, i, h)),
        out_shape=jax.ShapeDtypeStruct((bsz, seq, hh * dv), BF16),
        compiler_params=_cparams(3),
        name="mla_attention",
    )(q, kn, kr, v_t)


def _moba_kernel(slope_ref, q_ref, k_ref, vt_ref, pk_ref, pq_ref, o_ref, kmean_ref, *, n_blocks):
    blk, hp, hd = MOBA_BLOCK, HEADS_PER_STEP, MOBA_HD
    hg = pl.program_id(1)
    qi = pl.program_id(2)

    @pl.when(qi == 0)
    def _():
        kmean_ref[...] = jnp.zeros_like(kmean_ref)
        for h in range(hp):
            for n in range(n_blocks):
                kmean_ref[h, n:n + 1, :] = jnp.mean(
                    k_ref[n * blk:(n + 1) * blk, h * hd:(h + 1) * hd].astype(F32), axis=0, keepdims=True)

    lane = lax.broadcasted_iota(jnp.int32, (blk, LANES), 1)
    nb_pad = -(-n_blocks // SUBLANES) * SUBLANES
    blk_row = lax.broadcasted_iota(jnp.int32, (nb_pad, blk), 0)
    past = blk_row < qi
    q_aug = []
    for h in range(hp):
        q = q_ref[:, h * hd:(h + 1) * hd]
        km = kmean_ref[h]
        km_hi = km.astype(BF16)
        km_lo = (km - km_hi.astype(F32)).astype(BF16)
        gate_t = (_dot_nt(km_hi, q) + _dot_nt(km_lo, q))[:nb_pad, :]
        g = jnp.where(past, gate_t, LOW)
        rank = jnp.zeros(g.shape, jnp.int32)
        for dlt in range(1, nb_pad):
            other = pltpu.roll(g, dlt, axis=0)
            rank = rank + jnp.where(blk_row >= dlt, (other >= g).astype(jnp.int32),
                                    (other > g).astype(jnp.int32))
        chosen_t = jnp.logical_or(jnp.logical_and(rank < MOBA_TOPK, past), blk_row == qi)
        bias_t = jnp.concatenate([jnp.where(chosen_t, 0.0, NEG), jnp.full((LANES - nb_pad, blk), NEG, F32)],
                                 axis=0)
        q_aug.append(jnp.concatenate([q, bias_t.T.astype(BF16)], axis=1))

    pq = pq_ref[...]

    def step(n, carry, masked):
        ks = pl.multiple_of(n * blk, blk)
        onehot = jnp.where(lane == n, 1.0, 0.0).astype(BF16)
        dist_t = jnp.abs(pk_ref[pl.ds(ks, blk), :] - pq)
        s_all = []
        for h in range(hp):
            k_aug = jnp.concatenate([k_ref[pl.ds(ks, blk), h * hd:(h + 1) * hd], onehot], axis=1)
            s_t = _dot_nt(k_aug, q_aug[h]) - slope_ref[hg * hp + h] * dist_t
            if masked:
                key = lax.broadcasted_iota(jnp.int32, s_t.shape, 0)
                qry = lax.broadcasted_iota(jnp.int32, s_t.shape, 1)
                s_t = jnp.where(key <= qry, s_t, NEG)
            s_all.append(s_t)
        stats = [_softmax_stats_t(s_all[h], carry[h][0], carry[h][1]) for h in range(hp)]
        out = []
        for h in range(hp):
            m_new, l, alpha, p = stats[h]
            out.append((m_new, l, alpha * carry[h][2] + _dot(vt_ref[n, h * hd:(h + 1) * hd, :], p)))
        return tuple(out)

    init = tuple((jnp.full((1, blk), M_INIT, F32), jnp.zeros((1, blk), F32), jnp.zeros((hd, blk), F32))
                 for _ in range(hp))
    carry = lax.fori_loop(0, qi, lambda n, c: step(n, c, False), init)
    carry = step(qi, carry, True)
    for h in range(hp):
        _, l, acc_t = carry[h]
        o_ref[:, h * hd:(h + 1) * hd] = (acc_t / jnp.maximum(l, TINY)).T.astype(BF16)


def _alibi_slopes(n):
    return 2.0 ** (-8.0 * jnp.arange(1, n + 1, dtype=F32) / n)


def _moba_mixer(x, mod, pos_col, pos_f, w_in):
    bsz, seq, d = x.shape
    hh, hd, blk = MOBA_HEADS, MOBA_HD, MOBA_BLOCK
    assert seq % blk == 0 and seq // blk + MOBA_TOPK < LANES and blk == ATT_TK
    n_blocks = seq // blk
    hp = HEADS_PER_STEP
    nqk = 2 * hh * hd
    col_scale = jnp.concatenate([jnp.full((hh * hd,), hd ** -0.5 * LOG2E, F32),
                                 jnp.ones((hh * hd,), F32)]).reshape(1, -1)
    w_bf = w_in.astype(BF16)
    qk, v_t = _project(x, mod, w_bf[:, :nqk], col_scale, nqk, w_bf[:, nqk:].T)
    pos_row = pos_f.reshape(bsz, n_blocks, 1, blk)
    return pl.pallas_call(
        functools.partial(_moba_kernel, n_blocks=n_blocks),
        grid=(bsz, hh // hp, n_blocks),
        in_specs=[pl.BlockSpec(memory_space=pltpu.SMEM),
                  pl.BlockSpec((None, blk, hp * hd), lambda b, h, i: (b, i, h)),
                  pl.BlockSpec((None, seq, hp * hd), lambda b, h, i: (b, 0, hh // hp + h)),
                  pl.BlockSpec((None, n_blocks, hp * hd, blk), lambda b, h, i: (b, 0, h, 0)),
                  pl.BlockSpec((None, seq, 1), lambda b, h, i: (b, 0, 0)),
                  pl.BlockSpec((None, None, 1, blk), lambda b, h, i: (b, i, 0, 0))],
        out_specs=pl.BlockSpec((None, blk, hp * hd), lambda b, h, i: (b, i, h)),
        out_shape=jax.ShapeDtypeStruct((bsz, seq, hh * hd), BF16),
        scratch_shapes=[pltpu.VMEM((hp, LANES, hd), F32)],
        compiler_params=_cparams(3),
        name="moba_attention",
    )(_alibi_slopes(hh) * LOG2E, qk, qk, v_t, pos_col, pos_row)


def _sb_kernel(q_ref, k_ref, vt_ref, o_ref):
    qi = pl.program_id(2)
    tq, tk, hp, hd = q_ref.shape[0], ATT_TK, HEADS_PER_STEP, SB_HD
    rs = lax.broadcasted_iota(jnp.int32, (tk, tk), 0)
    cj = lax.broadcasted_iota(jnp.int32, (tk, tk), 1)
    after = jnp.where(cj > rs, 1.0, 0.0).astype(BF16)

    def step(j, carry, masked):
        ks = pl.multiple_of(j * tk, tk)
        z_all = [_dot_nt(k_ref[pl.ds(ks, tk), h * hd:(h + 1) * hd], q_ref[:, h * hd:(h + 1) * hd])
                 for h in range(hp)]
        if masked:
            key = lax.broadcasted_iota(jnp.int32, (tk, tq), 0)
            qry = lax.broadcasted_iota(jnp.int32, (tk, tq), 1)
            mask = key < qry
        parts = []
        for h in range(hp):
            z2 = z_all[h]
            log_beta = jnp.minimum(z2, 0.0) - jnp.log2(1.0 + jnp.exp2(-jnp.abs(z2)))
            log_1mb = log_beta - z2
            if masked:
                log_1mb = jnp.where(mask, log_1mb, 0.0)
            hi = log_1mb.astype(BF16)
            lo = (log_1mb - hi.astype(F32)).astype(BF16)
            parts.append((log_beta, log_1mb, hi, lo))
        tails = [_dot(after, parts[h][2]) + _dot(after, parts[h][3]) for h in range(hp)]
        out = []
        for h in range(hp):
            c, acc_t = carry[h]
            log_beta, log_1mb = parts[h][0], parts[h][1]
            a = jnp.exp2(log_beta + tails[h] + c)
            if masked:
                a = jnp.where(mask, a, 0.0)
            acc_t = acc_t + _dot(vt_ref[j, h * hd:(h + 1) * hd, :], a.astype(BF16))
            out.append((c + jnp.sum(log_1mb, axis=0, keepdims=True), acc_t))
        return tuple(out)

    init = tuple((jnp.zeros((1, tq), F32), jnp.zeros((hd, tq), F32)) for _ in range(hp))
    carry = step(qi, init, True)
    carry = lax.fori_loop(0, qi, lambda t, cr: step(qi - 1 - t, cr, False), carry)
    for h in range(hp):
        o_ref[:, h * hd:(h + 1) * hd] = carry[h][1].T.astype(BF16)


def _sb_mixer(x, mod, w_in):
    bsz, seq, d = x.shape
    hh, hd, hp = SB_HEADS, SB_HD, HEADS_PER_STEP
    nqk = 2 * hh * hd
    col_scale = jnp.concatenate([jnp.full((hh * hd,), hd ** -0.5 * LOG2E, F32),
                                 jnp.ones((hh * hd,), F32)]).reshape(1, -1)
    w_bf = w_in.astype(BF16)
    qk, v_t = _project(x, mod, w_bf[:, :nqk], col_scale, nqk, w_bf[:, nqk:].T)
    tq = ATT_TQ
    assert tq == ATT_TK
    return pl.pallas_call(
        _sb_kernel,
        grid=(bsz, hh // hp, seq // tq),
        in_specs=[pl.BlockSpec((None, tq, hp * hd), lambda b, h, i: (b, i, h)),
                  pl.BlockSpec((None, seq, hp * hd), lambda b, h, i: (b, 0, hh // hp + h)),
                  pl.BlockSpec((None, seq // ATT_TK, hp * hd, ATT_TK), lambda b, h, i: (b, 0, h, 0))],
        out_specs=pl.BlockSpec((None, tq, hp * hd), lambda b, h, i: (b, i, h)),
        out_shape=jax.ShapeDtypeStruct((bsz, seq, hh * hd), BF16),
        compiler_params=_cparams(3),
        name="stick_breaking_attention",
    )(qk, qk, v_t)


def _nsa_compress_kernel(t_ref, pe_ref, w1_ref, w2_ref, o_ref, *, n_cmp):
    st, half = NSA_CMP_STRIDE, NSA_CMP_LEN // NSA_CMP_STRIDE
    assert half == 2
    rows = t_ref.shape[0] // st
    xa, xb = [], []
    for r in range(st):
        xr = t_ref[pl.ds(r, rows, stride=st), :]
        xa.append((xr + pe_ref[r:r + 1, :]).astype(BF16))
        xb.append((xr + pe_ref[st + r:st + r + 1, :]).astype(BF16))
    k_half = st * NSA_HD
    pre_a = _dot(jnp.concatenate(xa, axis=1), w1_ref[:k_half, :])
    pre_b = _dot(jnp.concatenate(xb, axis=1), w1_ref[k_half:, :])
    pre = pre_a + pltpu.roll(pre_b, rows - 1, axis=0)
    out = _dot(_gelu_tanh(pre).astype(BF16), w2_ref[...])
    row = lax.broadcasted_iota(jnp.int32, out.shape, 0)
    o_ref[...] = jnp.where(row < n_cmp, out, 0.0).astype(BF16)


def _nsa_cmp_select_kernel(slope_ref, q_ref, kc_ref, vc_ref, gt_ref, pc_ref, pe_ref, ovl_ref,
                           acc_ref, sel_ref, *, n_cmp, n_sel):
    g = pl.program_id(1)
    qi = pl.program_id(2)
    tq = q_ref.shape[0]
    shape = (tq, LANES)
    lane = lax.broadcasted_iota(jnp.int32, shape, 1)
    qidx = qi * tq + lax.broadcasted_iota(jnp.int32, shape, 0)
    c_end = lane * NSA_CMP_STRIDE + (NSA_CMP_LEN - 1)
    mask_c = jnp.logical_and(c_end <= qidx, lane < n_cmp)
    dist = jnp.abs(pc_ref[...] - pe_ref[...])
    sig = jax.nn.sigmoid(gt_ref[...])
    kc, vc, ovl = kc_ref[...], vc_ref[...], ovl_ref[...]
    imp = jnp.zeros(shape, F32)
    for r in range(NSA_R):
        s = _dot_nt(q_ref[:, r * NSA_HD:(r + 1) * NSA_HD], kc) - slope_ref[g * NSA_R + r] * dist
        s = jnp.where(mask_c, s, NEG)
        p = jnp.where(mask_c, jnp.exp(s - jnp.max(s, axis=1, keepdims=True)), 0.0)
        p = (p / jnp.maximum(jnp.sum(p, axis=1, keepdims=True), TINY)).astype(BF16)
        imp = imp + _dot(p, ovl)
        gate = jnp.sum(jnp.where(lane == g * NSA_R + r, sig, 0.0), axis=1, keepdims=True)
        acc_ref[:, r * NSA_HD:(r + 1) * NSA_HD] = gate * _dot(p, vc)

    q_blk = qidx >> 6
    assert NSA_SEL_BLOCK == 64
    forced = jnp.logical_or(lane == 0, jnp.logical_or(lane == q_blk, lane == q_blk - 1))
    cand = lane <= q_blk
    score = jnp.where(cand, imp + jnp.where(forced, NSA_FORCE_BONUS, 0.0), NEG)
    score = jnp.where(lane < n_sel, score, LOW)
    rank = jnp.zeros(shape, jnp.int32)
    for dlt in range(1, n_sel):
        lower = pltpu.roll(score, dlt, axis=1)
        upper = pltpu.roll(score, LANES - dlt, axis=1)
        rank = rank + (lower >= score).astype(jnp.int32) + (upper > score).astype(jnp.int32)
    chosen = jnp.logical_and(rank < NSA_SEL_TOPN, cand)
    sel_ref[...] = jnp.where(chosen, 0.0, NEG).astype(BF16)


def _nsa_group_rows(ref, tq):
    return jnp.concatenate([ref[:, r * NSA_HD:(r + 1) * NSA_HD] for r in range(NSA_R)], axis=0)


def _nsa_slope_col(slope_ref, g, tq):
    row = lax.broadcasted_iota(jnp.int32, (NSA_R * tq, 1), 0)
    col = jnp.zeros((NSA_R * tq, 1), F32)
    for r in range(NSA_R):
        col = jnp.where(jnp.logical_and(row >= r * tq, row < (r + 1) * tq), slope_ref[g * NSA_R + r], col)
    return col


def _nsa_gate_cols(gt_ref, branch, g, tq):
    sig = jax.nn.sigmoid(gt_ref[...])
    lane = lax.broadcasted_iota(jnp.int32, sig.shape, 1)
    cols = [jnp.sum(jnp.where(lane == branch * NSA_HEADS + g * NSA_R + r, sig, 0.0), axis=1, keepdims=True)
            for r in range(NSA_R)]
    return jnp.concatenate(cols, axis=0)


def _nsa_select_kernel(slope_ref, q_ref, sel_ref, k_ref, v_ref, gt_ref, pc_ref, pr_ref, prev_ref,
                       out_ref):
    g = pl.program_id(1)
    qi = pl.program_id(2)
    tq, tk = q_ref.shape[0], NSA_TK
    rows = NSA_R * tq
    q4 = _nsa_group_rows(q_ref, tq)
    q_aug = jnp.concatenate([q4, jnp.concatenate([sel_ref[...]] * NSA_R, axis=0)], axis=1)
    pq4 = jnp.concatenate([pc_ref[...]] * NSA_R, axis=0)
    slope = _nsa_slope_col(slope_ref, g, tq)
    klane = lax.broadcasted_iota(jnp.int32, (tk, LANES), 1)
    krow = lax.broadcasted_iota(jnp.int32, (tk, LANES), 0)
    per_tile = tk // NSA_SEL_BLOCK
    qrow = lax.broadcasted_iota(jnp.int32, (rows, tk), 0)
    qidx = qi * tq + (qrow & (tq - 1))
    kcol = lax.broadcasted_iota(jnp.int32, (rows, tk), 1)

    def step(j, carry, masked):
        ks = pl.multiple_of(j * tk, tk)
        onehot = jnp.where(klane == j * per_tile + (krow >> 6), 1.0, 0.0).astype(BF16)
        k_aug = jnp.concatenate([k_ref[pl.ds(ks, tk), :], onehot], axis=1)
        s = _dot_nt(q_aug, k_aug) - slope * jnp.abs(pq4 - pr_ref[j])
        if masked:
            s = jnp.where(j * tk + kcol <= qidx, s, NEG)
        return _softmax_update(s, *carry, v_ref[pl.ds(ks, tk), :])

    init = (jnp.full((rows, 1), M_INIT, F32), jnp.zeros((rows, 1), F32), jnp.zeros((rows, NSA_HD), F32))
    diag = (qi * tq) // tk
    carry = lax.fori_loop(0, diag, lambda j, c: step(j, c, False), init)
    _, l, acc = step(diag, carry, True)
    o4 = _nsa_gate_cols(gt_ref, 1, g, tq) * (acc / jnp.maximum(l, TINY))
    for r in range(NSA_R):
        out_ref[:, r * NSA_HD:(r + 1) * NSA_HD] = (prev_ref[:, r * NSA_HD:(r + 1) * NSA_HD]
                                                    + o4[r * tq:(r + 1) * tq, :])


def _nsa_window_kernel(slope_ref, q_ref, k_ref, v_ref, gt_ref, pc_ref, pr_ref, prev_ref, out_ref):
    g = pl.program_id(1)
    qi = pl.program_id(2)
    tq = q_ref.shape[0]
    rows = NSA_R * tq
    span = NSA_WINDOW + tq
    start = pl.multiple_of(jnp.maximum(qi * tq - NSA_WINDOW, 0), tq)
    q4 = _nsa_group_rows(q_ref, tq)
    pq4 = jnp.concatenate([pc_ref[...]] * NSA_R, axis=0)
    kpos = pr_ref[:, pl.ds(start, span)]
    s = _dot_nt(q4, k_ref[pl.ds(start, span), :])
    s = s - _nsa_slope_col(slope_ref, g, tq) * jnp.abs(pq4 - kpos)
    qidx = qi * tq + (lax.broadcasted_iota(jnp.int32, (rows, span), 0) & (tq - 1))
    kidx = start + lax.broadcasted_iota(jnp.int32, (rows, span), 1)
    mask = jnp.logical_and(kidx <= qidx, qidx - kidx < NSA_WINDOW)
    s = jnp.where(mask, s, NEG)
    p = jnp.where(mask, jnp.exp(s - jnp.max(s, axis=1, keepdims=True)), 0.0)
    p = p / jnp.maximum(jnp.sum(p, axis=1, keepdims=True), TINY)
    o4 = _nsa_gate_cols(gt_ref, 2, g, tq) * _dot(p.astype(BF16), v_ref[pl.ds(start, span), :])
    for r in range(NSA_R):
        out_ref[:, r * NSA_HD:(r + 1) * NSA_HD] = (
            prev_ref[:, r * NSA_HD:(r + 1) * NSA_HD] + o4[r * tq:(r + 1) * tq, :]).astype(BF16)


def _nsa_mixer(x, mod, pos_col, pos_f, w_in, cmp_pos, cmp_w1, cmp_w2):
    bsz, seq, d = x.shape
    hh, gg, hd, rr = NSA_HEADS, NSA_GROUPS, NSA_HD, NSA_R
    n_cmp = (seq - NSA_CMP_LEN) // NSA_CMP_STRIDE + 1
    n_sel = seq // NSA_SEL_BLOCK
    assert seq % NSA_TK == 0 and seq // NSA_CMP_STRIDE == LANES and n_sel + n_sel <= LANES
    assert NSA_SEL_TOPN <= n_sel and seq >= NSA_WINDOW + NSA_TQ
    nq, nkv = hh * hd, gg * hd
    kv = lambda t: w_in[:, nq + t * nkv:nq + (t + 1) * nkv]
    n_gate = 3 * hh
    w_perm = jnp.concatenate([w_in[:, :nq], kv(2), kv(3), kv(4), kv(5), kv(0), kv(1),
                              w_in[:, nq + 6 * nkv:], jnp.zeros((d, LANES - n_gate), F32)], axis=1)
    n_main = nq + 4 * nkv
    col_scale = jnp.concatenate([jnp.full((nq,), hd ** -0.5, F32),
                                 jnp.ones((w_perm.shape[1] - nq,), F32)]).reshape(1, -1)
    w_vt = jnp.concatenate([kv(3), kv(5)], axis=1).T.astype(BF16)
    main, _, tail = _project(x, mod, w_perm.astype(BF16), col_scale, n_main, w_vt)
    slopes = _alibi_slopes(hh)
    gate_blk = 2 * nkv // LANES

    cmp_kv = pl.pallas_call(
        functools.partial(_nsa_compress_kernel, n_cmp=n_cmp),
        grid=(bsz, 2, gg),
        in_specs=[pl.BlockSpec((None, seq, hd), lambda b, w, g: (b, 0, w * gg + g)),
                  pl.BlockSpec((None, NSA_CMP_LEN, hd), lambda b, w, g: (w, 0, 0)),
                  pl.BlockSpec((None, NSA_CMP_LEN * hd, hd), lambda b, w, g: (w, 0, 0)),
                  pl.BlockSpec((None, hd, hd), lambda b, w, g: (w, 0, 0))],
        out_specs=pl.BlockSpec((None, None, LANES, hd), lambda b, w, g: (b, w * gg + g, 0, 0)),
        out_shape=jax.ShapeDtypeStruct((bsz, 2 * gg, LANES, hd), BF16),
        compiler_params=_cparams(3),
        name="nsa_compress",
    )(tail, cmp_pos, cmp_w1.astype(BF16), cmp_w2.astype(BF16))

    pos_cend = jnp.pad(pos_f[:, NSA_CMP_LEN - 1::NSA_CMP_STRIDE], ((0, 0), (0, LANES - n_cmp)))
    pos_cend = pos_cend.reshape(bsz, 1, LANES)
    c_start = jnp.arange(LANES) * NSA_CMP_STRIDE
    j_start = jnp.arange(LANES) * NSA_SEL_BLOCK
    overlap = ((c_start[:, None] < j_start[None, :] + NSA_SEL_BLOCK)
               & (c_start[:, None] + NSA_CMP_LEN - 1 >= j_start[None, :])
               & (jnp.arange(LANES)[:, None] < n_cmp) & (jnp.arange(LANES)[None, :] < n_sel))
    overlap = overlap.astype(BF16)

    tq = NSA_TQ
    smem = pl.BlockSpec(memory_space=pltpu.SMEM)
    q_spec = pl.BlockSpec((None, tq, rr * hd), lambda b, g, i: (b, i, g))
    gate_spec = pl.BlockSpec((None, tq, LANES), lambda b, g, i: (b, i, gate_blk))
    pc_spec = pl.BlockSpec((None, tq, 1), lambda b, g, i: (b, i, 0))
    acc_spec = pl.BlockSpec((None, tq, rr * hd), lambda b, g, i: (b, i, g))
    kv_spec = lambda t: pl.BlockSpec((None, seq, hd), lambda b, g, i: (b, 0, nq // hd + t * gg + g))
    grid = (bsz, gg, seq // tq)

    acc0, sel_bias = pl.pallas_call(
        functools.partial(_nsa_cmp_select_kernel, n_cmp=n_cmp, n_sel=n_sel),
        grid=grid,
        in_specs=[smem, q_spec,
                  pl.BlockSpec((None, None, LANES, hd), lambda b, g, i: (b, g, 0, 0)),
                  pl.BlockSpec((None, None, LANES, hd), lambda b, g, i: (b, gg + g, 0, 0)),
                  gate_spec, pc_spec,
                  pl.BlockSpec((None, 1, LANES), lambda b, g, i: (b, 0, 0)),
                  pl.BlockSpec((LANES, LANES), lambda b, g, i: (0, 0))],
        out_specs=[acc_spec, pl.BlockSpec((None, None, tq, LANES), lambda b, g, i: (b, g, i, 0))],
        out_shape=[jax.ShapeDtypeStruct((bsz, seq, hh * hd), F32),
                   jax.ShapeDtypeStruct((bsz, gg, seq, LANES), BF16)],
        compiler_params=_cparams(3),
        name="nsa_compressed_attention_select",
    )(slopes, main, cmp_kv, cmp_kv, tail, pos_col, pos_cend, overlap)

    pos_tiles = pos_f.reshape(bsz, seq // NSA_TK, 1, NSA_TK)
    acc1 = pl.pallas_call(
        _nsa_select_kernel,
        grid=grid,
        in_specs=[smem, q_spec,
                  pl.BlockSpec((None, None, tq, LANES), lambda b, g, i: (b, g, i, 0)),
                  kv_spec(0), kv_spec(1), gate_spec, pc_spec,
                  pl.BlockSpec((None, seq // NSA_TK, 1, NSA_TK), lambda b, g, i: (b, 0, 0, 0)),
                  acc_spec],
        out_specs=acc_spec,
        out_shape=jax.ShapeDtypeStruct((bsz, seq, hh * hd), F32),
        compiler_params=_cparams(3),
        name="nsa_selected_attention",
    )(slopes, main, sel_bias, main, main, tail, pos_col, pos_tiles, acc0)

    return pl.pallas_call(
        _nsa_window_kernel,
        grid=grid,
        in_specs=[smem, q_spec, kv_spec(2), kv_spec(3), gate_spec, pc_spec,
                  pl.BlockSpec((None, 1, seq), lambda b, g, i: (b, 0, 0)),
                  acc_spec],
        out_specs=acc_spec,
        out_shape=jax.ShapeDtypeStruct((bsz, seq, hh * hd), BF16),
        compiler_params=_cparams(3),
        name="nsa_window_attention",
    )(slopes, main, main, main, tail, pos_col, pos_f.reshape(bsz, 1, seq), acc1)


def kernel(x, c, positions, mod_w, mod_b, ln_g, ln_b, ffn_w_in, ffn_conv_w, ffn_conv_b, ffn_w_out,
           mla_w_in, mla_q_norm, mla_w_uq, mla_kv_norm, mla_w_ukv, mla_w_o,
           moba_w_in, moba_w_o, nsa_w_in, nsa_cmp_pos, nsa_cmp_w1, nsa_cmp_w2, nsa_w_o,
           sb_w_in, sb_w_o):
    bsz, seq, d = x.shape
    depth = mod_w.shape[0]
    assert d == D_MODEL and seq % TOKEN_TILE == 0 and seq % ATT_TQ == 0
    alpha = float((2 * depth) ** 0.25)
    pos_f = positions.astype(F32)
    pos_col = pos_f.reshape(bsz, seq, 1)
    mod_all = _modulation(c, mod_w, mod_b)
    for i in range(depth):
        kind, j = i % N_MIXERS, i // N_MIXERS
        mod = mod_all[i]
        if kind == 0:
            o = _mla_mixer(x, mod, pos_col, mla_w_in[j], mla_q_norm[j], mla_w_uq[j], mla_kv_norm[j],
                           mla_w_ukv[j])
            w_o = mla_w_o[j]
        elif kind == 1:
            o = _moba_mixer(x, mod, pos_col, pos_f, moba_w_in[j])
            w_o = moba_w_o[j]
        elif kind == 2:
            o = _nsa_mixer(x, mod, pos_col, pos_f, nsa_w_in[j], nsa_cmp_pos[j], nsa_cmp_w1[j],
                           nsa_cmp_w2[j])
            w_o = nsa_w_o[j]
        else:
            o = _sb_mixer(x, mod, sb_w_in[j])
            w_o = sb_w_o[j]
        x = _outproj_ln(o, w_o.astype(BF16), x, mod, ln_g[i, 0], ln_b[i, 0], alpha)
        w_ffn = ffn_w_in[i].astype(BF16)
        x = _ffn_ln(x, mod, w_ffn[:, :D_FF], w_ffn[:, D_FF:], ffn_conv_w[i], ffn_conv_b[i],
                    ffn_w_out[i].astype(BF16), ln_g[i, 1], ln_b[i, 1], alpha)
    return x
```

```python
import functools

import jax
import jax.numpy as jnp
from jax import lax
from jax.experimental import pallas as pl
from jax.experimental.pallas import tpu as pltpu

F32 = jnp.float32
BF16 = jnp.bfloat16

D_MODEL = 1024
N_MIXERS = 4
MLA_HEADS, MLA_NOPE, MLA_ROPE, MLA_V = 8, 128, 64, 128
MLA_Q_RANK, MLA_KV_RANK = 256, 256
ROPE_BASE = 10000.0
MOBA_HEADS, MOBA_HD, MOBA_BLOCK, MOBA_TOPK = 8, 128, 256, 3
NSA_HEADS, NSA_GROUPS, NSA_HD = 8, 2, 128
NSA_R = NSA_HEADS // NSA_GROUPS
NSA_CMP_LEN, NSA_CMP_STRIDE, NSA_SEL_BLOCK, NSA_SEL_TOPN, NSA_WINDOW = 32, 16, 64, 16, 512
NSA_FORCE_BONUS = 100.0
SB_HEADS, SB_HD = 8, 128
D_FF = 2816
LN_EPS = 1e-5
RMS_EPS = 1e-6
NEG = -1e30
M_INIT = -1e29
LOW = -3e38
TINY = 1e-30
LOG2E = 1.4426950408889634

LANES = 128
SUBLANES = 8
VMEM_LIMIT_BYTES = 56 * 1024 * 1024

TOKEN_TILE = 512
ATT_TQ = 256
ATT_TK = 256
HEADS_PER_STEP = 8
FF_CHUNK = 256
PROJ_CHUNK = 512


def _cparams(n_grid):
    return pltpu.CompilerParams(dimension_semantics=("arbitrary",) * n_grid,
                                vmem_limit_bytes=VMEM_LIMIT_BYTES)


def _dot(a, b):
    return jnp.dot(a, b, preferred_element_type=F32)


def _dot_nt(a, b):
    return lax.dot_general(a, b, (((1,), (1,)), ((), ())), preferred_element_type=F32)


def _layer_norm(z, g, b):
    mu = jnp.mean(z, axis=-1, keepdims=True)
    d = z - mu
    var = jnp.mean(d * d, axis=-1, keepdims=True)
    return d * lax.rsqrt(var + LN_EPS) * g + b


def _rms_norm(z, g):
    return z * lax.rsqrt(jnp.mean(z * z, axis=-1, keepdims=True) + RMS_EPS) * g


def _gelu_tanh(x):
    return 0.5 * x * (1.0 + jnp.tanh(0.7978845608028654 * (x + 0.044715 * (x * x * x))))


def _softmax_stats_t(s_t, m, l):
    m_new = jnp.maximum(m, jnp.max(s_t, axis=0, keepdims=True))
    alpha = jnp.exp2(m - m_new)
    p = jnp.exp2(s_t - m_new)
    l = alpha * l + jnp.sum(p, axis=0, keepdims=True)
    return m_new, l, alpha, p.astype(BF16)


def _mod_kernel(c_ref, w_ref, b_ref, o_ref):
    c = c_ref[...]
    c_act = c * jax.nn.sigmoid(c)
    o_ref[...] = jnp.dot(c_act, w_ref[...], preferred_element_type=F32,
                         precision=lax.Precision.HIGHEST) + b_ref[...]


def _modulation(c, mod_w, mod_b):
    depth, d, n = mod_w.shape
    bsz = c.shape[0]
    tn = 1536
    out = pl.pallas_call(
        _mod_kernel,
        grid=(depth, n // tn),
        in_specs=[pl.BlockSpec((bsz, d), lambda l, j: (0, 0)),
                  pl.BlockSpec((None, d, tn), lambda l, j: (l, 0, j)),
                  pl.BlockSpec((None, 1, tn), lambda l, j: (l, 0, j))],
        out_specs=pl.BlockSpec((None, bsz, tn), lambda l, j: (l, 0, j)),
        out_shape=jax.ShapeDtypeStruct((depth, bsz, n), F32),
        compiler_params=_cparams(2),
        name="modulation",
    )(c, mod_w, mod_b.reshape(depth, 1, n))
    return out.reshape(depth, bsz, 6, d)


def _proj_kernel(x_ref, mod_ref, w_ref, cs_ref, wt_ref, *out_refs, n_main, n_total):
    h = (x_ref[...] * (1.0 + mod_ref[1:2, :]) + mod_ref[0:1, :]).astype(BF16)
    for n0 in range(0, n_total, PROJ_CHUNK):
        n1 = min(n0 + PROJ_CHUNK, n_total)
        y = _dot(h, w_ref[:, n0:n1]) * cs_ref[:, n0:n1]
        if n0 < n_main:
            out_refs[0][:, n0:n1] = y.astype(BF16)
        else:
            out_refs[2][:, n0 - n_main:n1 - n_main] = y
    vt_ref = out_refs[1]
    for n0 in range(0, wt_ref.shape[0], PROJ_CHUNK):
        y_t = _dot_nt(wt_ref[n0:n0 + PROJ_CHUNK, :], h).astype(BF16)
        for t in range(vt_ref.shape[0]):
            vt_ref[t, n0:n0 + PROJ_CHUNK, :] = y_t[:, t * ATT_TK:(t + 1) * ATT_TK]


def _project(x, mod, w, col_scale, n_main, w_t):
    bsz, seq, d = x.shape
    n_total, n_t = w.shape[1], w_t.shape[0]
    assert n_main % PROJ_CHUNK == 0 and n_total % LANES == 0 and n_t % PROJ_CHUNK == 0
    tm = TOKEN_TILE
    out_shape = [jax.ShapeDtypeStruct((bsz, seq, n_main), BF16),
                 jax.ShapeDtypeStruct((bsz, seq // ATT_TK, n_t, ATT_TK), BF16)]
    out_specs = [pl.BlockSpec((None, tm, n_main), lambda b, i: (b, i, 0)),
                 pl.BlockSpec((None, tm // ATT_TK, n_t, ATT_TK), lambda b, i: (b, i, 0, 0))]
    if n_main < n_total:
        out_shape.append(jax.ShapeDtypeStruct((bsz, seq, n_total - n_main), F32))
        out_specs.append(pl.BlockSpec((None, tm, n_total - n_main), lambda b, i: (b, i, 0)))
    return pl.pallas_call(
        functools.partial(_proj_kernel, n_main=n_main, n_total=n_total),
        grid=(bsz, seq // tm),
        in_specs=[pl.BlockSpec((None, tm, d), lambda b, i: (b, i, 0)),
                  pl.BlockSpec((None, 6, d), lambda b, i: (b, 0, 0)),
                  pl.BlockSpec((d, n_total), lambda b, i: (0, 0)),
                  pl.BlockSpec((1, n_total), lambda b, i: (0, 0)),
                  pl.BlockSpec((n_t, d), lambda b, i: (0, 0))],
        out_specs=out_specs,
        out_shape=out_shape,
        compiler_params=_cparams(2),
        name="modulate_project",
    )(x, mod, w, col_scale, w_t)


def _outproj_ln_kernel(o_ref, w_ref, x_ref, mod_ref, g_ref, b_ref, out_ref, *, alpha):
    y = _dot(o_ref[...], w_ref[...])
    z = alpha * x_ref[...] + mod_ref[2:3, :] * y
    out_ref[...] = _layer_norm(z, g_ref[...], b_ref[...])


def _outproj_ln(o, w_o, x, mod, ln_g, ln_b, alpha):
    bsz, seq, d = x.shape
    k = o.shape[-1]
    tm = TOKEN_TILE
    return pl.pallas_call(
        functools.partial(_outproj_ln_kernel, alpha=alpha),
        grid=(bsz, seq // tm),
        in_specs=[pl.BlockSpec((None, tm, k), lambda b, i: (b, i, 0)),
                  pl.BlockSpec((k, d), lambda b, i: (0, 0)),
                  pl.BlockSpec((None, tm, d), lambda b, i: (b, i, 0)),
                  pl.BlockSpec((None, 6, d), lambda b, i: (b, 0, 0)),
                  pl.BlockSpec((1, d), lambda b, i: (0, 0)),
                  pl.BlockSpec((1, d), lambda b, i: (0, 0))],
        out_specs=pl.BlockSpec((None, tm, d), lambda b, i: (b, i, 0)),
        out_shape=jax.ShapeDtypeStruct((bsz, seq, d), F32),
        compiler_params=_cparams(2),
        name="outproj_residual_ln",
    )(o, w_o, x, mod, ln_g.reshape(1, d), ln_b.reshape(1, d))


def _ffn_kernel(x_ref, halo_ref, mod_ref, wa_ref, wb_ref, cw_ref, cb_ref, wo_ref, g_ref, b_ref,
                out_ref, a_scr, g_scr, *, alpha):
    tm = x_ref.shape[0]
    halo = halo_ref.shape[0]
    shift, scale, gate = mod_ref[3:4, :], mod_ref[4:5, :], mod_ref[5:6, :]
    x = x_ref[...]
    h = (x * (1.0 + scale) + shift).astype(BF16)
    rows_to_zero = jnp.where(pl.program_id(1) == 0, halo, 0)
    row = lax.broadcasted_iota(jnp.int32, (halo, 1), 0)
    h_halo = jnp.where(row >= rows_to_zero, halo_ref[...] * (1.0 + scale) + shift, 0.0).astype(BF16)
    h_ext = jnp.concatenate([h_halo, h], axis=0)
    chunks = [(c0, min(c0 + FF_CHUNK, D_FF)) for c0 in range(0, D_FF, FF_CHUNK)]

    def matmuls(i):
        c0, c1 = chunks[i]
        a_scr[i % 2, :, :c1 - c0] = _dot(h_ext, wa_ref[:, c0:c1])
        return _dot(h, wb_ref[:, c0:c1])

    b_next = matmuls(0)
    for i, (c0, c1) in enumerate(chunks):
        b = b_next
        if i + 1 < len(chunks):
            b_next = matmuls(i + 1)
        w = c1 - c0
        a = a_scr[i % 2, halo:halo + tm, :w]
        a_m1 = a_scr[i % 2, halo - 1:halo - 1 + tm, :w]
        a_m2 = a_scr[i % 2, halo - 2:halo - 2 + tm, :w]
        conv = (a * cw_ref[2:3, c0:c1] + a_m1 * cw_ref[1:2, c0:c1] + a_m2 * cw_ref[0:1, c0:c1]
                + cb_ref[:, c0:c1])
        g_scr[:, c0:c1] = (_gelu_tanh(conv) * b).astype(BF16)
    y = _dot(g_scr[...], wo_ref[...])
    out_ref[...] = _layer_norm(alpha * x + gate * y, g_ref[...], b_ref[...])


def _ffn_ln(x, mod, w_a, w_b, conv_w, conv_b, w_out, ln_g, ln_b, alpha):
    bsz, seq, d = x.shape
    tm = TOKEN_TILE
    halo = 2 * SUBLANES
    halo_blocks = tm // halo
    const = lambda b, i: (0, 0)
    return pl.pallas_call(
        functools.partial(_ffn_kernel, alpha=alpha),
        grid=(bsz, seq // tm),
        in_specs=[pl.BlockSpec((None, tm, d), lambda b, i: (b, i, 0)),
                  pl.BlockSpec((None, halo, d),
                               lambda b, i: (b, jnp.maximum(i * halo_blocks - 1, 0), 0)),
                  pl.BlockSpec((None, 6, d), lambda b, i: (b, 0, 0)),
                  pl.BlockSpec((d, D_FF), const),
                  pl.BlockSpec((d, D_FF), const),
                  pl.BlockSpec((3, D_FF), const),
                  pl.BlockSpec((1, D_FF), const),
                  pl.BlockSpec((D_FF, d), const),
                  pl.BlockSpec((1, d), const),
                  pl.BlockSpec((1, d), const)],
        out_specs=pl.BlockSpec((None, tm, d), lambda b, i: (b, i, 0)),
        out_shape=jax.ShapeDtypeStruct((bsz, seq, d), F32),
        scratch_shapes=[pltpu.VMEM((2, halo + tm, FF_CHUNK), F32), pltpu.VMEM((tm, D_FF), BF16)],
        compiler_params=_cparams(2),
        name="conv_ffn_residual_ln",
    )(x, x, mod, w_a, w_b, conv_w, conv_b.reshape(1, D_FF), w_out,
      ln_g.reshape(1, d), ln_b.reshape(1, d))


def _mla_proj_kernel(x_ref, mod_ref, pos_ref, invf_ref, win_ref, qn_ref, kvn_ref, wuq_ref,
                     wuqr_ref, wuk_ref, wuvt_ref, q_ref, kn_ref, kr_ref, vt_ref, *, scale):
    h = (x_ref[...] * (1.0 + mod_ref[1:2, :]) + mod_ref[0:1, :]).astype(BF16)
    proj = _dot(h, win_ref[...])
    c_q = _rms_norm(proj[:, :MLA_Q_RANK], qn_ref[...]).astype(BF16)
    c_kv = _rms_norm(proj[:, MLA_Q_RANK:MLA_Q_RANK + MLA_KV_RANK], kvn_ref[...]).astype(BF16)
    ang = pos_ref[...] * invf_ref[...]
    cos, sin = jnp.cos(ang), jnp.sin(ang)
    r0 = MLA_Q_RANK + MLA_KV_RANK
    kr_ref[...] = (proj[:, r0:r0 + LANES] * cos + proj[:, r0 + LANES:r0 + 2 * LANES] * sin).astype(BF16)
    for hd in range(MLA_HEADS):
        qa = _dot(c_q, wuq_ref[:, hd * 2 * LANES:(hd + 1) * 2 * LANES])
        qr = _dot(c_q, wuqr_ref[:, hd * LANES:(hd + 1) * LANES])
        q_ref[:, hd * 2 * LANES:hd * 2 * LANES + LANES] = (qa[:, :LANES] * scale).astype(BF16)
        q_ref[:, hd * 2 * LANES + LANES:(hd + 1) * 2 * LANES] = (
            (qa[:, LANES:] * cos + qr * sin) * scale).astype(BF16)
    kn_ref[...] = _dot(c_kv, wuk_ref[...]).astype(BF16)
    v_t = _dot_nt(wuvt_ref[...], c_kv).astype(BF16)
    for t in range(vt_ref.shape[0]):
        vt_ref[t] = v_t[:, t * ATT_TK:(t + 1) * ATT_TK]


def _mla_attn_kernel(q_ref, kn_ref, kr_ref, vt_ref, o_ref):
    qi = pl.program_id(2)
    tq = q_ref.shape[0]
    hp = HEADS_PER_STEP

    def step(j, carry, masked):
        ks = pl.multiple_of(j * ATT_TK, ATT_TK)
        kr = kr_ref[pl.ds(ks, ATT_TK), :]
        def scores(h):
            k = jnp.concatenate([kn_ref[pl.ds(ks, ATT_TK), h * MLA_NOPE:(h + 1) * MLA_NOPE], kr], axis=1)
            s_t = _dot_nt(k, q_ref[:, h * 2 * LANES:(h + 1) * 2 * LANES])
            if masked:
                key = lax.broadcasted_iota(jnp.int32, s_t.shape, 0)
                qry = lax.broadcasted_iota(jnp.int32, s_t.shape, 1)
                s_t = jnp.where(key <= qry, s_t, NEG)
            return s_t

        s_all = [scores(h) for h in range(hp)]
        stats = [_softmax_stats_t(s_all[h], carry[h][0], carry[h][1]) for h in range(hp)]
        out = []
        for h in range(hp):
            m_new, l, alpha, p = stats[h]
            out.append((m_new, l, alpha * carry[h][2] + _dot(vt_ref[j, h * MLA_V:(h + 1) * MLA_V, :], p)))
        return tuple(out)

    init = tuple((jnp.full((1, tq), M_INIT, F32), jnp.zeros((1, tq), F32), jnp.zeros((MLA_V, tq), F32))
                 for _ in range(hp))
    carry = lax.fori_loop(0, qi, lambda j, c: step(j, c, False), init)
    carry = step(qi, carry, True)
    for h in range(hp):
        _, l, acc_t = carry[h]
        o_ref[:, h * MLA_V:(h + 1) * MLA_V] = (acc_t / jnp.maximum(l, TINY)).T.astype(BF16)


def _mla_mixer(x, mod, pos_col, w_in, q_norm, w_uq, kv_norm, w_ukv):
    bsz, seq, d = x.shape
    hh, dn, dr, dv = MLA_HEADS, MLA_NOPE, MLA_ROPE, MLA_V
    half = dr // 2
    r0 = MLA_Q_RANK + MLA_KV_RANK
    zpad = lambda rows, n: jnp.zeros((rows, n), F32)
    x1, x2 = w_in[:, r0:r0 + half], w_in[:, r0 + half:r0 + dr]
    w_in_ext = jnp.concatenate([w_in[:, :r0], x1, x2, zpad(d, LANES - dr),
                                -x2, x1, zpad(d, LANES - dr)], axis=1).astype(BF16)
    wq = w_uq.reshape(MLA_Q_RANK, hh, dn + dr)
    q1, q2 = wq[:, :, dn:dn + half], wq[:, :, dn + half:]
    zq = jnp.zeros((MLA_Q_RANK, hh, LANES - dr), F32)
    w_uq_main = jnp.concatenate([wq[:, :, :dn], q1, q2, zq], axis=2).reshape(MLA_Q_RANK, hh * 2 * LANES)
    w_uq_rot = jnp.concatenate([-q2, q1, zq], axis=2).reshape(MLA_Q_RANK, hh * LANES)
    wkv = w_ukv.reshape(MLA_KV_RANK, hh, dn + dv)
    w_uk = wkv[:, :, :dn].reshape(MLA_KV_RANK, hh * dn)
    w_uv_t = wkv[:, :, dn:].reshape(MLA_KV_RANK, hh * dv).T
    inv_freq = ROPE_BASE ** (-jnp.arange(half, dtype=F32) / half)
    invf = jnp.concatenate([inv_freq, inv_freq, jnp.zeros((LANES - dr,), F32)]).reshape(1, LANES)

    tm = TOKEN_TILE
    const = lambda b, i: (0, 0)
    tok = lambda n: pl.BlockSpec((None, tm, n), lambda b, i: (b, i, 0))
    q, kn, kr, v_t = pl.pallas_call(
        functools.partial(_mla_proj_kernel, scale=float((dn + dr) ** -0.5) * LOG2E),
        grid=(bsz, seq // tm),
        in_specs=[tok(d),
                  pl.BlockSpec((None, 6, d), lambda b, i: (b, 0, 0)),
                  tok(1),
                  pl.BlockSpec((1, LANES), const),
                  pl.BlockSpec(w_in_ext.shape, const),
                  pl.BlockSpec((1, MLA_Q_RANK), const),
                  pl.BlockSpec((1, MLA_KV_RANK), const),
                  pl.BlockSpec(w_uq_main.shape, const),
                  pl.BlockSpec(w_uq_rot.shape, const),
                  pl.BlockSpec(w_uk.shape, const),
                  pl.BlockSpec(w_uv_t.shape, const)],
        out_specs=[tok(hh * 2 * LANES), tok(hh * dn), tok(LANES),
                   pl.BlockSpec((None, tm // ATT_TK, hh * dv, ATT_TK), lambda b, i: (b, i, 0, 0))],
        out_shape=[jax.ShapeDtypeStruct((bsz, seq, hh * 2 * LANES), BF16),
                   jax.ShapeDtypeStruct((bsz, seq, hh * dn), BF16),
                   jax.ShapeDtypeStruct((bsz, seq, LANES), BF16),
                   jax.ShapeDtypeStruct((bsz, seq // ATT_TK, hh * dv, ATT_TK), BF16)],
        compiler_params=_cparams(2),
        name="mla_project",
    )(x, mod, pos_col, invf, w_in_ext, q_norm.reshape(1, -1), kv_norm.reshape(1, -1),
      w_uq_main.astype(BF16), w_uq_rot.astype(BF16), w_uk.astype(BF16), w_uv_t.astype(BF16))

    tq, hp = ATT_TQ, HEADS_PER_STEP
    return pl.pallas_call(
        _mla_attn_kernel,
        grid=(bsz, hh // hp, seq // tq),
        in_specs=[pl.BlockSpec((None, tq, hp * 2 * LANES), lambda b, h, i: (b, i, h)),
                  pl.BlockSpec((None, seq, hp * dn), lambda b, h, i: (b, 0, h)),
                  pl.BlockSpec((None, seq, LANES), lambda b, h, i: (b, 0, 0)),
                  pl.BlockSpec((None, seq // ATT_TK, hp * dv, ATT_TK), lambda b, h, i: (b, 0, h, 0))],
        out_specs=pl.BlockSpec((None, tq, hp * dv), lambda b, h, i: (b, i, h)),
        out_shape=jax.ShapeDtypeStruct((bsz, seq, hh * dv), BF16),
        compiler_params=_cparams(3),
        name="mla_attention",
    )(q, kn, kr, v_t)


def _moba_kernel(slope_ref, q_ref, k_ref, vt_ref, pk_ref, pq_ref, o_ref, kmean_ref, *, n_blocks):
    blk, hp, hd = MOBA_BLOCK, HEADS_PER_STEP, MOBA_HD
    hg = pl.program_id(1)
    qi = pl.program_id(2)

    @pl.when(qi == 0)
    def _():
        kmean_ref[...] = jnp.zeros_like(kmean_ref)
        for h in range(hp):
            for n in range(n_blocks):
                kmean_ref[h, n:n + 1, :] = jnp.mean(
                    k_ref[n * blk:(n + 1) * blk, h * hd:(h + 1) * hd].astype(F32), axis=0, keepdims=True)

    lane = lax.broadcasted_iota(jnp.int32, (blk, LANES), 1)
    nb_pad = -(-n_blocks // SUBLANES) * SUBLANES
    blk_row = lax.broadcasted_iota(jnp.int32, (nb_pad, blk), 0)
    past = blk_row < qi
    q_aug = []
    for h in range(hp):
        q = q_ref[:, h * hd:(h + 1) * hd]
        km = kmean_ref[h]
        km_hi = km.astype(BF16)
        km_lo = (km - km_hi.astype(F32)).astype(BF16)
        gate_t = (_dot_nt(km_hi, q) + _dot_nt(km_lo, q))[:nb_pad, :]
        g = jnp.where(past, gate_t, LOW)
        rank = jnp.zeros(g.shape, jnp.int32)
        for dlt in range(1, nb_pad):
            other = pltpu.roll(g, dlt, axis=0)
            rank = rank + jnp.where(blk_row >= dlt, (other >= g).astype(jnp.int32),
                                    (other > g).astype(jnp.int32))
        chosen_t = jnp.logical_or(jnp.logical_and(rank < MOBA_TOPK, past), blk_row == qi)
        bias_t = jnp.concatenate([jnp.where(chosen_t, 0.0, NEG), jnp.full((LANES - nb_pad, blk), NEG, F32)],
                                 axis=0)
        q_aug.append(jnp.concatenate([q, bias_t.T.astype(BF16)], axis=1))

    pq = pq_ref[...]

    def step(n, carry, masked):
        ks = pl.multiple_of(n * blk, blk)
        onehot = jnp.where(lane == n, 1.0, 0.0).astype(BF16)
        dist_t = jnp.abs(pk_ref[pl.ds(ks, blk), :] - pq)
        s_all = []
        for h in range(hp):
            k_aug = jnp.concatenate([k_ref[pl.ds(ks, blk), h * hd:(h + 1) * hd], onehot], axis=1)
            s_t = _dot_nt(k_aug, q_aug[h]) - slope_ref[hg * hp + h] * dist_t
            if masked:
                key = lax.broadcasted_iota(jnp.int32, s_t.shape, 0)
                qry = lax.broadcasted_iota(jnp.int32, s_t.shape, 1)
                s_t = jnp.where(key <= qry, s_t, NEG)
            s_all.append(s_t)
        stats = [_softmax_stats_t(s_all[h], carry[h][0], carry[h][1]) for h in range(hp)]
        out = []
        for h in range(hp):
            m_new, l, alpha, p = stats[h]
            out.append((m_new, l, alpha * carry[h][2] + _dot(vt_ref[n, h * hd:(h + 1) * hd, :], p)))
        return tuple(out)

    init = tuple((jnp.full((1, blk), M_INIT, F32), jnp.zeros((1, blk), F32), jnp.zeros((hd, blk), F32))
                 for _ in range(hp))
    carry = lax.fori_loop(0, qi, lambda n, c: step(n, c, False), init)
    carry = step(qi, carry, True)
    for h in range(hp):
        _, l, acc_t = carry[h]
        o_ref[:, h * hd:(h + 1) * hd] = (acc_t / jnp.maximum(l, TINY)).T.astype(BF16)


def _alibi_slopes(n):
    return 2.0 ** (-8.0 * jnp.arange(1, n + 1, dtype=F32) / n)


def _moba_mixer(x, mod, pos_col, pos_f, w_in):
    bsz, seq, d = x.shape
    hh, hd, blk = MOBA_HEADS, MOBA_HD, MOBA_BLOCK
    assert seq % blk == 0 and seq // blk + MOBA_TOPK < LANES and blk == ATT_TK
    n_blocks = seq // blk
    hp = HEADS_PER_STEP
    nqk = 2 * hh * hd
    col_scale = jnp.concatenate([jnp.full((hh * hd,), hd ** -0.5 * LOG2E, F32),
                                 jnp.ones((hh * hd,), F32)]).reshape(1, -1)
    w_bf = w_in.astype(BF16)
    qk, v_t = _project(x, mod, w_bf[:, :nqk], col_scale, nqk, w_bf[:, nqk:].T)
    pos_row = pos_f.reshape(bsz, n_blocks, 1, blk)
    return pl.pallas_call(
        functools.partial(_moba_kernel, n_blocks=n_blocks),
        grid=(bsz, hh // hp, n_blocks),
        in_specs=[pl.BlockSpec(memory_space=pltpu.SMEM),
                  pl.BlockSpec((None, blk, hp * hd), lambda b, h, i: (b, i, h)),
                  pl.BlockSpec((None, seq, hp * hd), lambda b, h, i: (b, 0, hh // hp + h)),
                  pl.BlockSpec((None, n_blocks, hp * hd, blk), lambda b, h, i: (b, 0, h, 0)),
                  pl.BlockSpec((None, seq, 1), lambda b, h, i: (b, 0, 0)),
                  pl.BlockSpec((None, None, 1, blk), lambda b, h, i: (b, i, 0, 0))],
        out_specs=pl.BlockSpec((None, blk, hp * hd), lambda b, h, i: (b, i, h)),
        out_shape=jax.ShapeDtypeStruct((bsz, seq, hh * hd), BF16),
        scratch_shapes=[pltpu.VMEM((hp, LANES, hd), F32)],
        compiler_params=_cparams(3),
        name="moba_attention",
    )(_alibi_slopes(hh) * LOG2E, qk, qk, v_t, pos_col, pos_row)


def _sb_kernel(q_ref, k_ref, vt_ref, o_ref):
    qi = pl.program_id(2)
    tq, tk, hp, hd = q_ref.shape[0], ATT_TK, HEADS_PER_STEP, SB_HD
    rs = lax.broadcasted_iota(jnp.int32, (tk, tk), 0)
    cj = lax.broadcasted_iota(jnp.int32, (tk, tk), 1)
    after = jnp.where(cj > rs, 1.0, 0.0).astype(BF16)

    def step(j, carry, masked):
        ks = pl.multiple_of(j * tk, tk)
        z_all = [_dot_nt(k_ref[pl.ds(ks, tk), h * hd:(h + 1) * hd], q_ref[:, h * hd:(h + 1) * hd])
                 for h in range(hp)]
        if masked:
            key = lax.broadcasted_iota(jnp.int32, (tk, tq), 0)
            qry = lax.broadcasted_iota(jnp.int32, (tk, tq), 1)
            mask = key < qry
        parts = []
        for h in range(hp):
            z2 = z_all[h]
            log_beta = jnp.minimum(z2, 0.0) - jnp.log2(1.0 + jnp.exp2(-jnp.abs(z2)))
            log_1mb = log_beta - z2
            if masked:
                log_1mb = jnp.where(mask, log_1mb, 0.0)
            hi = log_1mb.astype(BF16)
            lo = (log_1mb - hi.astype(F32)).astype(BF16)
            parts.append((log_beta, log_1mb, hi, lo))
        tails = [_dot(after, parts[h][2]) + _dot(after, parts[h][3]) for h in range(hp)]
        out = []
        for h in range(hp):
            c, acc_t = carry[h]
            log_beta, log_1mb = parts[h][0], parts[h][1]
            a = jnp.exp2(log_beta + tails[h] + c)
            if masked:
                a = jnp.where(mask, a, 0.0)
            acc_t = acc_t + _dot(vt_ref[j, h * hd:(h + 1) * hd, :], a.astype(BF16))
            out.append((c + jnp.sum(log_1mb, axis=0, keepdims=True), acc_t))
        return tuple(out)

    init = tuple((jnp.zeros((1, tq), F32), jnp.zeros((hd, tq), F32)) for _ in range(hp))
    carry = step(qi, init, True)
    carry = lax.fori_loop(0, qi, lambda t, cr: step(qi - 1 - t, cr, False), carry)
    for h in range(hp):
        o_ref[:, h * hd:(h + 1) * hd] = carry[h][1].T.astype(BF16)


def _sb_mixer(x, mod, w_in):
    bsz, seq, d = x.shape
    hh, hd, hp = SB_HEADS, SB_HD, HEADS_PER_STEP
    nqk = 2 * hh * hd
    col_scale = jnp.concatenate([jnp.full((hh * hd,), hd ** -0.5 * LOG2E, F32),
                                 jnp.ones((hh * hd,), F32)]).reshape(1, -1)
    w_bf = w_in.astype(BF16)
    qk, v_t = _project(x, mod, w_bf[:, :nqk], col_scale, nqk, w_bf[:, nqk:].T)
    tq = ATT_TQ
    assert tq == ATT_TK
    return pl.pallas_call(
        _sb_kernel,
        grid=(bsz, hh // hp, seq // tq),
        in_specs=[pl.BlockSpec((None, tq, hp * hd), lambda b, h, i: (b, i, h)),
                  pl.BlockSpec((None, seq, hp * hd), lambda b, h, i: (b, 0, hh // hp + h)),
                  pl.BlockSpec((None, seq // ATT_TK, hp * hd, ATT_TK), lambda b, h, i: (b, 0, h, 0))],
        out_specs=pl.BlockSpec((None, tq, hp * hd), lambda b, h, i: (b, i, h)),
        out_shape=jax.ShapeDtypeStruct((bsz, seq, hh * hd), BF16),
        compiler_params=_cparams(3),
        name="stick_breaking_attention",
    )(qk, qk, v_t)


def _nsa_compress_kernel(t_ref, pe_ref, w1_ref, w2_ref, o_ref, ot_ref, *, n_cmp):
    st, half = NSA_CMP_STRIDE, NSA_CMP_LEN // NSA_CMP_STRIDE
    assert half == 2
    rows = t_ref.shape[0] // st
    xa, xb = [], []
    for r in range(st):
        xr = t_ref[pl.ds(r, rows, stride=st), :]
        xa.append((xr + pe_ref[r:r + 1, :]).astype(BF16))
        xb.append((xr + pe_ref[st + r:st + r + 1, :]).astype(BF16))
    k_half = st * NSA_HD
    pre_a = _dot(jnp.concatenate(xa, axis=1), w1_ref[:k_half, :])
    pre_b = _dot(jnp.concatenate(xb, axis=1), w1_ref[k_half:, :])
    pre = pre_a + pltpu.roll(pre_b, rows - 1, axis=0)
    out = _dot(_gelu_tanh(pre).astype(BF16), w2_ref[...])
    row = lax.broadcasted_iota(jnp.int32, out.shape, 0)
    out = jnp.where(row < n_cmp, out, 0.0)
    o_ref[...] = out.astype(BF16)
    ot_ref[...] = out.T.astype(BF16)


def _nsa_cmp_select_kernel(slope_ref, q_ref, kc_ref, vct_ref, gt_ref, pq_ref, pe_ref, ovlt_ref,
                           acc_ref, sel_ref, *, n_cmp, n_sel):
    qi = pl.program_id(1)
    tq, gg, rr, hd = q_ref.shape[0], NSA_GROUPS, NSA_R, NSA_HD
    shape = (LANES, tq)
    n_row = lax.broadcasted_iota(jnp.int32, shape, 0)
    qidx = qi * tq + lax.broadcasted_iota(jnp.int32, shape, 1)
    c_end = n_row * NSA_CMP_STRIDE + (NSA_CMP_LEN - 1)
    mask_c = jnp.logical_and(c_end <= qidx, n_row < n_cmp)
    dist_t = jnp.abs(pe_ref[...] - pq_ref[...])
    sig_t = jax.nn.sigmoid(gt_ref[...]).T
    ovl_t = ovlt_ref[...]

    heads = [(g, r) for g in range(gg) for r in range(rr)]
    s_all = [_dot_nt(kc_ref[g], q_ref[:, (g * rr + r) * hd:(g * rr + r + 1) * hd])
             - slope_ref[g * rr + r] * dist_t for g, r in heads]
    p_all = []
    for s_t in s_all:
        s_t = jnp.where(mask_c, s_t, NEG)
        p = jnp.where(mask_c, jnp.exp2(s_t - jnp.max(s_t, axis=0, keepdims=True)), 0.0)
        p_all.append((p / jnp.maximum(jnp.sum(p, axis=0, keepdims=True), TINY)).astype(BF16))
    imp_t = [jnp.zeros((n_sel, tq), F32) for _ in range(gg)]
    for (g, r), p in zip(heads, p_all):
        imp_t[g] = imp_t[g] + _dot(ovl_t, p)
        c = g * rr + r
        acc_ref[:, c * hd:(c + 1) * hd] = (sig_t[c:c + 1, :] * _dot(vct_ref[gg + g], p)).T

    sshape = (n_sel, tq)
    j_row = lax.broadcasted_iota(jnp.int32, sshape, 0)
    q_blk = (qi * tq + lax.broadcasted_iota(jnp.int32, sshape, 1)) >> 6
    assert NSA_SEL_BLOCK == 64
    forced = jnp.logical_or(j_row == 0, jnp.logical_or(j_row == q_blk, j_row == q_blk - 1))
    cand = j_row <= q_blk
    bonus = jnp.where(forced, NSA_FORCE_BONUS, 0.0)
    for g in range(gg):
        score = jnp.where(cand, imp_t[g] + bonus, NEG)
        rank = jnp.zeros(sshape, jnp.int32)
        for dlt in range(1, n_sel):
            other = pltpu.roll(score, dlt, axis=0)
            rank = rank + jnp.where(j_row >= dlt, (other >= score).astype(jnp.int32),
                                    (other > score).astype(jnp.int32))
        chosen_t = jnp.logical_and(rank < NSA_SEL_TOPN, cand)
        bias_t = jnp.concatenate([jnp.where(chosen_t, 0.0, NEG), jnp.full((LANES - n_sel, tq), NEG, F32)],
                                 axis=0)
        sel_ref[g] = bias_t.T.astype(BF16)


def _nsa_select_window_kernel(slope_ref, q_ref, sel_ref, k_ref, vt_ref, gt_ref, pk_ref, pq_ref, prev_ref,
                              out_ref):
    qi = pl.program_id(1)
    tq, tk, hd, gg, rr = q_ref.shape[0], ATT_TK, NSA_HD, NSA_GROUPS, NSA_R
    assert tq == tk and NSA_WINDOW % tk == 0
    lanes = rr * tq
    pq4 = jnp.concatenate([pq_ref[...]] * rr, axis=1)
    qidx = qi * tq + (lax.broadcasted_iota(jnp.int32, (tk, lanes), 1) & (tq - 1))
    krow = lax.broadcasted_iota(jnp.int32, (tk, lanes), 0)
    klane = lax.broadcasted_iota(jnp.int32, (tk, LANES), 1)
    kblk = lax.broadcasted_iota(jnp.int32, (tk, LANES), 0) >> 6
    per_tile = tk // NSA_SEL_BLOCK
    slope_rows = [jnp.concatenate([jnp.full((1, tq), slope_ref[g * rr + r], F32) for r in range(rr)], axis=1)
                  for g in range(gg)]
    q4 = [jnp.concatenate([q_ref[:, (g * rr + r) * hd:(g * rr + r + 1) * hd] for r in range(rr)], axis=0)
          for g in range(gg)]
    q_aug = [jnp.concatenate([q4[g], jnp.concatenate([sel_ref[g]] * rr, axis=0)], axis=1) for g in range(gg)]
    k_win0 = gg * hd

    def update(s_all, carry, v_row0, j):
        stats = [_softmax_stats_t(s_all[g], carry[g][0], carry[g][1]) for g in range(gg)]
        out = []
        for g in range(gg):
            m_new, l, alpha, p = stats[g]
            v_t = vt_ref[j, v_row0 + g * hd:v_row0 + (g + 1) * hd, :]
            out.append((m_new, l, alpha * carry[g][2] + _dot(v_t, p)))
        return tuple(out)

    def select_step(j, carry, masked):
        ks = pl.multiple_of(j * tk, tk)
        dist = jnp.abs(pk_ref[pl.ds(ks, tk), :] - pq4)
        onehot = jnp.where(klane == j * per_tile + kblk, 1.0, 0.0).astype(BF16)
        s_all = []
        for g in range(gg):
            k_aug = jnp.concatenate([k_ref[pl.ds(ks, tk), g * hd:(g + 1) * hd], onehot], axis=1)
            s_t = _dot_nt(k_aug, q_aug[g]) - slope_rows[g] * dist
            if masked:
                s_t = jnp.where(j * tk + krow <= qidx, s_t, NEG)
            s_all.append(s_t)
        return update(s_all, carry, 0, j)

    def init():
        return tuple((jnp.full((1, lanes), M_INIT, F32), jnp.zeros((1, lanes), F32),
                      jnp.zeros((hd, lanes), F32)) for _ in range(gg))

    sel = lax.fori_loop(0, qi, lambda j, c: select_step(j, c, False), init())
    sel = select_step(qi, sel, True)

    win = init()
    for t in range(NSA_WINDOW // tk + 1):
        jt = qi - NSA_WINDOW // tk + t
        jc = jnp.maximum(jt, 0)
        ks = pl.multiple_of(jc * tk, tk)
        kidx = jt * tk + krow
        mask = jnp.logical_and(jnp.logical_and(kidx >= 0, kidx <= qidx), qidx - kidx < NSA_WINDOW)
        dist = jnp.abs(pk_ref[pl.ds(ks, tk), :] - pq4)
        s_all = []
        for g in range(gg):
            s_t = _dot_nt(k_ref[pl.ds(ks, tk), k_win0 + g * hd:k_win0 + (g + 1) * hd], q4[g])
            s_all.append(jnp.where(mask, s_t - slope_rows[g] * dist, NEG))
        win = update(s_all, win, gg * hd, jc)

    sig_t = jax.nn.sigmoid(gt_ref[...]).T
    for g in range(gg):
        gate = lambda branch: jnp.concatenate(
            [sig_t[branch * NSA_HEADS + g * rr + r:branch * NSA_HEADS + g * rr + r + 1, :] for r in range(rr)],
            axis=1)
        o_t = (gate(1) * (sel[g][2] / jnp.maximum(sel[g][1], TINY))
               + gate(2) * (win[g][2] / jnp.maximum(win[g][1], TINY)))
        for r in range(rr):
            c = g * rr + r
            out_ref[:, c * hd:(c + 1) * hd] = (prev_ref[:, c * hd:(c + 1) * hd]
                                                + o_t[:, r * tq:(r + 1) * tq].T).astype(BF16)


def _nsa_mixer(x, mod, pos_col, pos_f, w_in, cmp_pos, cmp_w1, cmp_w2):
    bsz, seq, d = x.shape
    hh, gg, hd = NSA_HEADS, NSA_GROUPS, NSA_HD
    n_cmp = (seq - NSA_CMP_LEN) // NSA_CMP_STRIDE + 1
    n_sel = seq // NSA_SEL_BLOCK
    tq = ATT_TQ
    assert seq % tq == 0 and seq // NSA_CMP_STRIDE == LANES and n_sel % SUBLANES == 0 and n_sel <= LANES
    assert NSA_SEL_TOPN <= n_sel
    nq, nkv = hh * hd, gg * hd
    kv = lambda t: w_in[:, nq + t * nkv:nq + (t + 1) * nkv]
    n_gate = 3 * hh
    w_perm = jnp.concatenate([w_in[:, :nq], kv(2), kv(4), kv(0), kv(1),
                              w_in[:, nq + 6 * nkv:], jnp.zeros((d, LANES - n_gate), F32)], axis=1)
    n_main = nq + 2 * nkv
    col_scale = jnp.concatenate([jnp.full((nq,), hd ** -0.5 * LOG2E, F32),
                                 jnp.ones((w_perm.shape[1] - nq,), F32)]).reshape(1, -1)
    w_vt = jnp.concatenate([kv(3), kv(5)], axis=1).T.astype(BF16)
    main, v_t, tail = _project(x, mod, w_perm.astype(BF16), col_scale, n_main, w_vt)
    slopes = _alibi_slopes(hh) * LOG2E
    gate_blk = 2 * nkv // LANES

    cmp_spec = pl.BlockSpec((None, None, LANES, hd), lambda b, w, g: (b, w * gg + g, 0, 0))
    cmp_kv, cmp_kv_t = pl.pallas_call(
        functools.partial(_nsa_compress_kernel, n_cmp=n_cmp),
        grid=(bsz, 2, gg),
        in_specs=[pl.BlockSpec((None, seq, hd), lambda b, w, g: (b, 0, w * gg + g)),
                  pl.BlockSpec((None, NSA_CMP_LEN, hd), lambda b, w, g: (w, 0, 0)),
                  pl.BlockSpec((None, NSA_CMP_LEN * hd, hd), lambda b, w, g: (w, 0, 0)),
                  pl.BlockSpec((None, hd, hd), lambda b, w, g: (w, 0, 0))],
        out_specs=[cmp_spec, cmp_spec],
        out_shape=[jax.ShapeDtypeStruct((bsz, 2 * gg, LANES, hd), BF16)] * 2,
        compiler_params=_cparams(3),
        name="nsa_compress",
    )(tail, cmp_pos, cmp_w1.astype(BF16), cmp_w2.astype(BF16))

    pos_cend = jnp.pad(pos_f[:, NSA_CMP_LEN - 1::NSA_CMP_STRIDE], ((0, 0), (0, LANES - n_cmp)))
    pos_cend = pos_cend.reshape(bsz, LANES, 1)
    c_start = jnp.arange(LANES) * NSA_CMP_STRIDE
    j_start = jnp.arange(n_sel) * NSA_SEL_BLOCK
    overlap_t = ((c_start[None, :] < j_start[:, None] + NSA_SEL_BLOCK)
                 & (c_start[None, :] + NSA_CMP_LEN - 1 >= j_start[:, None])
                 & (jnp.arange(LANES)[None, :] < n_cmp)).astype(BF16)

    smem = pl.BlockSpec(memory_space=pltpu.SMEM)
    q_spec = pl.BlockSpec((None, tq, nq), lambda b, i: (b, i, 0))
    gate_spec = pl.BlockSpec((None, tq, LANES), lambda b, i: (b, i, gate_blk))
    pq_spec = pl.BlockSpec((None, None, 1, tq), lambda b, i: (b, i, 0, 0))
    sel_spec = pl.BlockSpec((None, gg, tq, LANES), lambda b, i: (b, 0, i, 0))
    cmp_all = pl.BlockSpec((None, 2 * gg, LANES, hd), lambda b, i: (b, 0, 0, 0))
    pos_q = pos_f.reshape(bsz, seq // tq, 1, tq)
    grid = (bsz, seq // tq)

    acc0, sel_bias = pl.pallas_call(
        functools.partial(_nsa_cmp_select_kernel, n_cmp=n_cmp, n_sel=n_sel),
        grid=grid,
        in_specs=[smem, q_spec, cmp_all, cmp_all, gate_spec, pq_spec,
                  pl.BlockSpec((None, LANES, 1), lambda b, i: (b, 0, 0)),
                  pl.BlockSpec((n_sel, LANES), lambda b, i: (0, 0))],
        out_specs=[q_spec, sel_spec],
        out_shape=[jax.ShapeDtypeStruct((bsz, seq, nq), F32),
                   jax.ShapeDtypeStruct((bsz, gg, seq, LANES), BF16)],
        compiler_params=_cparams(2),
        name="nsa_compressed_attention_select",
    )(slopes, main, cmp_kv, cmp_kv_t, tail, pos_q, pos_cend, overlap_t)

    return pl.pallas_call(
        _nsa_select_window_kernel,
        grid=grid,
        in_specs=[smem, q_spec, sel_spec,
                  pl.BlockSpec((None, seq, 2 * nkv), lambda b, i: (b, 0, nq // (2 * nkv))),
                  pl.BlockSpec((None, seq // ATT_TK, 2 * nkv, ATT_TK), lambda b, i: (b, 0, 0, 0)),
                  gate_spec,
                  pl.BlockSpec((None, seq, 1), lambda b, i: (b, 0, 0)),
                  pq_spec, q_spec],
        out_specs=q_spec,
        out_shape=jax.ShapeDtypeStruct((bsz, seq, nq), BF16),
        compiler_params=_cparams(2),
        name="nsa_selected_window_attention",
    )(slopes, main, sel_bias, main, v_t, tail, pos_col, pos_q, acc0)


def kernel(x, c, positions, mod_w, mod_b, ln_g, ln_b, ffn_w_in, ffn_conv_w, ffn_conv_b, ffn_w_out,
           mla_w_in, mla_q_norm, mla_w_uq, mla_kv_norm, mla_w_ukv, mla_w_o,
           moba_w_in, moba_w_o, nsa_w_in, nsa_cmp_pos, nsa_cmp_w1, nsa_cmp_w2, nsa_w_o,
           sb_w_in, sb_w_o):
    bsz, seq, d = x.shape
    depth = mod_w.shape[0]
    assert d == D_MODEL and seq % TOKEN_TILE == 0 and seq % ATT_TQ == 0
    alpha = float((2 * depth) ** 0.25)
    pos_f = positions.astype(F32)
    pos_col = pos_f.reshape(bsz, seq, 1)
    mod_all = _modulation(c, mod_w, mod_b)
    for i in range(depth):
        kind, j = i % N_MIXERS, i // N_MIXERS
        mod = mod_all[i]
        if kind == 0:
            o = _mla_mixer(x, mod, pos_col, mla_w_in[j], mla_q_norm[j], mla_w_uq[j], mla_kv_norm[j],
                           mla_w_ukv[j])
            w_o = mla_w_o[j]
        elif kind == 1:
            o = _moba_mixer(x, mod, pos_col, pos_f, moba_w_in[j])
            w_o = moba_w_o[j]
        elif kind == 2:
            o = _nsa_mixer(x, mod, pos_col, pos_f, nsa_w_in[j], nsa_cmp_pos[j], nsa_cmp_w1[j],
                           nsa_cmp_w2[j])
            w_o = nsa_w_o[j]
        else:
            o = _sb_mixer(x, mod, sb_w_in[j])
            w_o = sb_w_o[j]
        x = _outproj_ln(o, w_o.astype(BF16), x, mod, ln_g[i, 0], ln_b[i, 0], alpha)
        w_ffn = ffn_w_in[i].astype(BF16)
        x = _ffn_ln(x, mod, w_ffn[:, :D_FF], w_ffn[:, D_FF:], ffn_conv_w[i], ffn_conv_b[i],
                    ffn_w_out[i].astype(BF16), ln_g[i, 1], ln_b[i, 1], alpha)
    return x
```

```python
import functools

import jax
import jax.numpy as jnp
from jax import lax
from jax.experimental import pallas as pl
from jax.experimental.pallas import tpu as pltpu

F32 = jnp.float32
BF16 = jnp.bfloat16

D_MODEL = 1024
N_MIXERS = 4
MLA_HEADS, MLA_NOPE, MLA_ROPE, MLA_V = 8, 128, 64, 128
MLA_Q_RANK, MLA_KV_RANK = 256, 256
ROPE_BASE = 10000.0
MOBA_HEADS, MOBA_HD, MOBA_BLOCK, MOBA_TOPK = 8, 128, 256, 3
NSA_HEADS, NSA_GROUPS, NSA_HD = 8, 2, 128
NSA_R = NSA_HEADS // NSA_GROUPS
NSA_CMP_LEN, NSA_CMP_STRIDE, NSA_SEL_BLOCK, NSA_SEL_TOPN, NSA_WINDOW = 32, 16, 64, 16, 512
NSA_FORCE_BONUS = 100.0
SB_HEADS, SB_HD = 8, 128
D_FF = 2816
LN_EPS = 1e-5
RMS_EPS = 1e-6
NEG = -1e30
M_INIT = -1e29
LOW = -3e38
TINY = 1e-30
LOG2E = 1.4426950408889634

LANES = 128
SUBLANES = 8
VMEM_LIMIT_BYTES = 56 * 1024 * 1024

TOKEN_TILE = 512
ATT_TQ = 256
ATT_TK = 256
HEADS_PER_STEP = 8
FF_CHUNK = 256
PROJ_CHUNK = 512


def _cparams(n_grid):
    return pltpu.CompilerParams(dimension_semantics=("arbitrary",) * n_grid,
                                vmem_limit_bytes=VMEM_LIMIT_BYTES)


def _dot(a, b):
    return jnp.dot(a, b, preferred_element_type=F32)


def _dot_nt(a, b):
    return lax.dot_general(a, b, (((1,), (1,)), ((), ())), preferred_element_type=F32)


def _layer_norm(z, g, b):
    mu = jnp.mean(z, axis=-1, keepdims=True)
    d = z - mu
    var = jnp.mean(d * d, axis=-1, keepdims=True)
    return d * lax.rsqrt(var + LN_EPS) * g + b


def _rms_norm(z, g):
    return z * lax.rsqrt(jnp.mean(z * z, axis=-1, keepdims=True) + RMS_EPS) * g


def _gelu_tanh(x):
    return 0.5 * x * (1.0 + jnp.tanh(0.7978845608028654 * (x + 0.044715 * (x * x * x))))


def _softmax_stats_t(s_t, m, l):
    m_new = jnp.maximum(m, jnp.max(s_t, axis=0, keepdims=True))
    alpha = jnp.exp2(m - m_new)
    p = jnp.exp2(s_t - m_new)
    l = alpha * l + jnp.sum(p, axis=0, keepdims=True)
    return m_new, l, alpha, p.astype(BF16)


def _mod_kernel(c_ref, w_ref, b_ref, o_ref):
    c = c_ref[...]
    c_act = c * jax.nn.sigmoid(c)
    o_ref[...] = jnp.dot(c_act, w_ref[...], preferred_element_type=F32,
                         precision=lax.Precision.HIGHEST) + b_ref[...]


def _modulation(c, mod_w, mod_b):
    depth, d, n = mod_w.shape
    bsz = c.shape[0]
    tn = 1536
    out = pl.pallas_call(
        _mod_kernel,
        grid=(depth, n // tn),
        in_specs=[pl.BlockSpec((bsz, d), lambda l, j: (0, 0)),
                  pl.BlockSpec((None, d, tn), lambda l, j: (l, 0, j)),
                  pl.BlockSpec((None, 1, tn), lambda l, j: (l, 0, j))],
        out_specs=pl.BlockSpec((None, bsz, tn), lambda l, j: (l, 0, j)),
        out_shape=jax.ShapeDtypeStruct((depth, bsz, n), F32),
        compiler_params=_cparams(2),
        name="modulation",
    )(c, mod_w, mod_b.reshape(depth, 1, n))
    return out.reshape(depth, bsz, 6, d)


def _proj_kernel(x_ref, mod_ref, w_ref, cs_ref, wt_ref, *out_refs, n_main, n_total):
    h = (x_ref[...] * (1.0 + mod_ref[1:2, :]) + mod_ref[0:1, :]).astype(BF16)
    for n0 in range(0, n_total, PROJ_CHUNK):
        n1 = min(n0 + PROJ_CHUNK, n_total)
        y = _dot(h, w_ref[:, n0:n1]) * cs_ref[:, n0:n1]
        if n0 < n_main:
            out_refs[0][:, n0:n1] = y.astype(BF16)
        else:
            out_refs[2][:, n0 - n_main:n1 - n_main] = y
    vt_ref = out_refs[1]
    for n0 in range(0, wt_ref.shape[0], PROJ_CHUNK):
        y_t = _dot_nt(wt_ref[n0:n0 + PROJ_CHUNK, :], h).astype(BF16)
        for t in range(vt_ref.shape[0]):
            vt_ref[t, n0:n0 + PROJ_CHUNK, :] = y_t[:, t * ATT_TK:(t + 1) * ATT_TK]


def _project(x, mod, w, col_scale, n_main, w_t):
    bsz, seq, d = x.shape
    n_total, n_t = w.shape[1], w_t.shape[0]
    assert n_main % PROJ_CHUNK == 0 and n_total % LANES == 0 and n_t % PROJ_CHUNK == 0
    tm = TOKEN_TILE
    out_shape = [jax.ShapeDtypeStruct((bsz, seq, n_main), BF16),
                 jax.ShapeDtypeStruct((bsz, seq // ATT_TK, n_t, ATT_TK), BF16)]
    out_specs = [pl.BlockSpec((None, tm, n_main), lambda b, i: (b, i, 0)),
                 pl.BlockSpec((None, tm // ATT_TK, n_t, ATT_TK), lambda b, i: (b, i, 0, 0))]
    if n_main < n_total:
        out_shape.append(jax.ShapeDtypeStruct((bsz, seq, n_total - n_main), F32))
        out_specs.append(pl.BlockSpec((None, tm, n_total - n_main), lambda b, i: (b, i, 0)))
    return pl.pallas_call(
        functools.partial(_proj_kernel, n_main=n_main, n_total=n_total),
        grid=(bsz, seq // tm),
        in_specs=[pl.BlockSpec((None, tm, d), lambda b, i: (b, i, 0)),
                  pl.BlockSpec((None, 6, d), lambda b, i: (b, 0, 0)),
                  pl.BlockSpec((d, n_total), lambda b, i: (0, 0)),
                  pl.BlockSpec((1, n_total), lambda b, i: (0, 0)),
                  pl.BlockSpec((n_t, d), lambda b, i: (0, 0))],
        out_specs=out_specs,
        out_shape=out_shape,
        compiler_params=_cparams(2),
        name="modulate_project",
    )(x, mod, w, col_scale, w_t)


def _mixer_out_ffn_kernel(o_ref, wmix_ref, x_ref, mod_ref, g1_ref, b1_ref, wa_ref, wb_ref, cw_ref, cb_ref,
                          wo_ref, g2_ref, b2_ref, out_ref, a_scr, g_scr, halo_scr, *, alpha):
    tm = x_ref.shape[0]
    halo = halo_scr.shape[0]
    x1 = _layer_norm(alpha * x_ref[...] + mod_ref[2:3, :] * _dot(o_ref[...], wmix_ref[...]),
                     g1_ref[...], b1_ref[...])
    shift, scale, gate = mod_ref[3:4, :], mod_ref[4:5, :], mod_ref[5:6, :]
    h = (x1 * (1.0 + scale) + shift).astype(BF16)

    @pl.when(pl.program_id(1) == 0)
    def _():
        halo_scr[...] = jnp.zeros_like(halo_scr)

    h_ext = jnp.concatenate([halo_scr[...], h], axis=0)
    halo_scr[...] = h[tm - halo:, :]
    chunks = [(c0, min(c0 + FF_CHUNK, D_FF)) for c0 in range(0, D_FF, FF_CHUNK)]

    def matmuls(i):
        c0, c1 = chunks[i]
        a_scr[i % 2, :, :c1 - c0] = _dot(h_ext, wa_ref[:, c0:c1])
        return _dot(h, wb_ref[:, c0:c1])

    b_next = matmuls(0)
    for i, (c0, c1) in enumerate(chunks):
        b = b_next
        if i + 1 < len(chunks):
            b_next = matmuls(i + 1)
        w = c1 - c0
        a = a_scr[i % 2, halo:halo + tm, :w]
        a_m1 = a_scr[i % 2, halo - 1:halo - 1 + tm, :w]
        a_m2 = a_scr[i % 2, halo - 2:halo - 2 + tm, :w]
        conv = (a * cw_ref[2:3, c0:c1] + a_m1 * cw_ref[1:2, c0:c1] + a_m2 * cw_ref[0:1, c0:c1]
                + cb_ref[:, c0:c1])
        g_scr[:, c0:c1] = (_gelu_tanh(conv) * b).astype(BF16)
    y = _dot(g_scr[...], wo_ref[...])
    out_ref[...] = _layer_norm(alpha * x1 + gate * y, g2_ref[...], b2_ref[...])


def _mixer_out_ffn(o, w_mix, x, mod, ln_g, ln_b, w_a, w_b, conv_w, conv_b, w_out, alpha):
    bsz, seq, d = x.shape
    k = o.shape[-1]
    tm = TOKEN_TILE
    halo = 2 * SUBLANES
    const = lambda shape: pl.BlockSpec(shape, lambda b, i: (0, 0), pipeline_mode=pl.Buffered(1))
    return pl.pallas_call(
        functools.partial(_mixer_out_ffn_kernel, alpha=alpha),
        grid=(bsz, seq // tm),
        in_specs=[pl.BlockSpec((None, tm, k), lambda b, i: (b, i, 0)),
                  const((k, d)),
                  pl.BlockSpec((None, tm, d), lambda b, i: (b, i, 0)),
                  pl.BlockSpec((None, 6, d), lambda b, i: (b, 0, 0)),
                  const((1, d)), const((1, d)),
                  const((d, D_FF)), const((d, D_FF)), const((3, D_FF)), const((1, D_FF)),
                  const((D_FF, d)), const((1, d)), const((1, d))],
        out_specs=pl.BlockSpec((None, tm, d), lambda b, i: (b, i, 0)),
        out_shape=jax.ShapeDtypeStruct((bsz, seq, d), F32),
        scratch_shapes=[pltpu.VMEM((2, halo + tm, FF_CHUNK), F32), pltpu.VMEM((tm, D_FF), BF16),
                        pltpu.VMEM((halo, d), BF16)],
        compiler_params=_cparams(2),
        name="mixer_out_conv_ffn",
    )(o, w_mix, x, mod, ln_g[0].reshape(1, d), ln_b[0].reshape(1, d), w_a, w_b, conv_w,
      conv_b.reshape(1, D_FF), w_out, ln_g[1].reshape(1, d), ln_b[1].reshape(1, d))


def _mla_proj_kernel(x_ref, mod_ref, pos_ref, invf_ref, win_ref, qn_ref, kvn_ref, wuq_ref,
                     wuqr_ref, wuk_ref, wuvt_ref, q_ref, kn_ref, kr_ref, vt_ref, *, scale):
    h = (x_ref[...] * (1.0 + mod_ref[1:2, :]) + mod_ref[0:1, :]).astype(BF16)
    proj = _dot(h, win_ref[...])
    c_q = _rms_norm(proj[:, :MLA_Q_RANK], qn_ref[...]).astype(BF16)
    c_kv = _rms_norm(proj[:, MLA_Q_RANK:MLA_Q_RANK + MLA_KV_RANK], kvn_ref[...]).astype(BF16)
    ang = pos_ref[...] * invf_ref[...]
    cos, sin = jnp.cos(ang), jnp.sin(ang)
    r0 = MLA_Q_RANK + MLA_KV_RANK
    kr_ref[...] = (proj[:, r0:r0 + LANES] * cos + proj[:, r0 + LANES:r0 + 2 * LANES] * sin).astype(BF16)
    for hd in range(MLA_HEADS):
        qa = _dot(c_q, wuq_ref[:, hd * 2 * LANES:(hd + 1) * 2 * LANES])
        qr = _dot(c_q, wuqr_ref[:, hd * LANES:(hd + 1) * LANES])
        q_ref[:, hd * 2 * LANES:hd * 2 * LANES + LANES] = (qa[:, :LANES] * scale).astype(BF16)
        q_ref[:, hd * 2 * LANES + LANES:(hd + 1) * 2 * LANES] = (
            (qa[:, LANES:] * cos + qr * sin) * scale).astype(BF16)
    kn_ref[...] = _dot(c_kv, wuk_ref[...]).astype(BF16)
    v_t = _dot_nt(wuvt_ref[...], c_kv).astype(BF16)
    for t in range(vt_ref.shape[0]):
        vt_ref[t] = v_t[:, t * ATT_TK:(t + 1) * ATT_TK]


def _mla_attn_kernel(q_ref, kn_ref, kr_ref, vt_ref, o_ref):
    qi = pl.program_id(2)
    tq = q_ref.shape[0]
    hp = HEADS_PER_STEP

    def step(j, carry, masked):
        ks = pl.multiple_of(j * ATT_TK, ATT_TK)
        kr = kr_ref[pl.ds(ks, ATT_TK), :]
        def scores(h):
            k = jnp.concatenate([kn_ref[pl.ds(ks, ATT_TK), h * MLA_NOPE:(h + 1) * MLA_NOPE], kr], axis=1)
            s_t = _dot_nt(k, q_ref[:, h * 2 * LANES:(h + 1) * 2 * LANES])
            if masked:
                key = lax.broadcasted_iota(jnp.int32, s_t.shape, 0)
                qry = lax.broadcasted_iota(jnp.int32, s_t.shape, 1)
                s_t = jnp.where(key <= qry, s_t, NEG)
            return s_t

        s_all = [scores(h) for h in range(hp)]
        stats = [_softmax_stats_t(s_all[h], carry[h][0], carry[h][1]) for h in range(hp)]
        out = []
        for h in range(hp):
            m_new, l, alpha, p = stats[h]
            out.append((m_new, l, alpha * carry[h][2] + _dot(vt_ref[j, h * MLA_V:(h + 1) * MLA_V, :], p)))
        return tuple(out)

    init = tuple((jnp.full((1, tq), M_INIT, F32), jnp.zeros((1, tq), F32), jnp.zeros((MLA_V, tq), F32))
                 for _ in range(hp))
    carry = lax.fori_loop(0, qi, lambda j, c: step(j, c, False), init)
    carry = step(qi, carry, True)
    for h in range(hp):
        _, l, acc_t = carry[h]
        o_ref[:, h * MLA_V:(h + 1) * MLA_V] = (acc_t / jnp.maximum(l, TINY)).T.astype(BF16)


def _mla_mixer(x, mod, pos_col, w_in, q_norm, w_uq, kv_norm, w_ukv):
    bsz, seq, d = x.shape
    hh, dn, dr, dv = MLA_HEADS, MLA_NOPE, MLA_ROPE, MLA_V
    half = dr // 2
    r0 = MLA_Q_RANK + MLA_KV_RANK
    zpad = lambda rows, n: jnp.zeros((rows, n), F32)
    x1, x2 = w_in[:, r0:r0 + half], w_in[:, r0 + half:r0 + dr]
    w_in_ext = jnp.concatenate([w_in[:, :r0], x1, x2, zpad(d, LANES - dr),
                                -x2, x1, zpad(d, LANES - dr)], axis=1).astype(BF16)
    wq = w_uq.reshape(MLA_Q_RANK, hh, dn + dr)
    q1, q2 = wq[:, :, dn:dn + half], wq[:, :, dn + half:]
    zq = jnp.zeros((MLA_Q_RANK, hh, LANES - dr), F32)
    w_uq_main = jnp.concatenate([wq[:, :, :dn], q1, q2, zq], axis=2).reshape(MLA_Q_RANK, hh * 2 * LANES)
    w_uq_rot = jnp.concatenate([-q2, q1, zq], axis=2).reshape(MLA_Q_RANK, hh * LANES)
    wkv = w_ukv.reshape(MLA_KV_RANK, hh, dn + dv)
    w_uk = wkv[:, :, :dn].reshape(MLA_KV_RANK, hh * dn)
    w_uv_t = wkv[:, :, dn:].reshape(MLA_KV_RANK, hh * dv).T
    inv_freq = ROPE_BASE ** (-jnp.arange(half, dtype=F32) / half)
    invf = jnp.concatenate([inv_freq, inv_freq, jnp.zeros((LANES - dr,), F32)]).reshape(1, LANES)

    tm = TOKEN_TILE
    const = lambda b, i: (0, 0)
    tok = lambda n: pl.BlockSpec((None, tm, n), lambda b, i: (b, i, 0))
    q, kn, kr, v_t = pl.pallas_call(
        functools.partial(_mla_proj_kernel, scale=float((dn + dr) ** -0.5) * LOG2E),
        grid=(bsz, seq // tm),
        in_specs=[tok(d),
                  pl.BlockSpec((None, 6, d), lambda b, i: (b, 0, 0)),
                  tok(1),
                  pl.BlockSpec((1, LANES), const),
                  pl.BlockSpec(w_in_ext.shape, const),
                  pl.BlockSpec((1, MLA_Q_RANK), const),
                  pl.BlockSpec((1, MLA_KV_RANK), const),
                  pl.BlockSpec(w_uq_main.shape, const),
                  pl.BlockSpec(w_uq_rot.shape, const),
                  pl.BlockSpec(w_uk.shape, const),
                  pl.BlockSpec(w_uv_t.shape, const)],
        out_specs=[tok(hh * 2 * LANES), tok(hh * dn), tok(LANES),
                   pl.BlockSpec((None, tm // ATT_TK, hh * dv, ATT_TK), lambda b, i: (b, i, 0, 0))],
        out_shape=[jax.ShapeDtypeStruct((bsz, seq, hh * 2 * LANES), BF16),
                   jax.ShapeDtypeStruct((bsz, seq, hh * dn), BF16),
                   jax.ShapeDtypeStruct((bsz, seq, LANES), BF16),
                   jax.ShapeDtypeStruct((bsz, seq // ATT_TK, hh * dv, ATT_TK), BF16)],
        compiler_params=_cparams(2),
        name="mla_project",
    )(x, mod, pos_col, invf, w_in_ext, q_norm.reshape(1, -1), kv_norm.reshape(1, -1),
      w_uq_main.astype(BF16), w_uq_rot.astype(BF16), w_uk.astype(BF16), w_uv_t.astype(BF16))

    tq, hp = ATT_TQ, HEADS_PER_STEP
    return pl.pallas_call(
        _mla_attn_kernel,
        grid=(bsz, hh // hp, seq // tq),
        in_specs=[pl.BlockSpec((None, tq, hp * 2 * LANES), lambda b, h, i: (b, i, h)),
                  pl.BlockSpec((None, seq, hp * dn), lambda b, h, i: (b, 0, h)),
                  pl.BlockSpec((None, seq, LANES), lambda b, h, i: (b, 0, 0)),
                  pl.BlockSpec((None, seq // ATT_TK, hp * dv, ATT_TK), lambda b, h, i: (b, 0, h, 0))],
        out_specs=pl.BlockSpec((None, tq, hp * dv), lambda b, h, i: (b, i, h)),
        out_shape=jax.ShapeDtypeStruct((bsz, seq, hh * dv), BF16),
        compiler_params=_cparams(3),
        name="mla_attention",
    )(q, kn, kr, v_t)


def _moba_kernel(slope_ref, q_ref, k_ref, vt_ref, pk_ref, pq_ref, o_ref, kmean_ref, *, n_blocks):
    blk, hp, hd = MOBA_BLOCK, HEADS_PER_STEP, MOBA_HD
    hg = pl.program_id(1)
    qi = pl.program_id(2)

    @pl.when(qi == 0)
    def _():
        kmean_ref[...] = jnp.zeros_like(kmean_ref)
        for h in range(hp):
            for n in range(n_blocks):
                kmean_ref[h, n:n + 1, :] = jnp.mean(
                    k_ref[n * blk:(n + 1) * blk, h * hd:(h + 1) * hd].astype(F32), axis=0, keepdims=True)

    lane = lax.broadcasted_iota(jnp.int32, (blk, LANES), 1)
    nb_pad = -(-n_blocks // SUBLANES) * SUBLANES
    blk_row = lax.broadcasted_iota(jnp.int32, (nb_pad, blk), 0)
    past = blk_row < qi
    q_aug = []
    for h in range(hp):
        q = q_ref[:, h * hd:(h + 1) * hd]
        km = kmean_ref[h]
        km_hi = km.astype(BF16)
        km_lo = (km - km_hi.astype(F32)).astype(BF16)
        gate_t = (_dot_nt(km_hi, q) + _dot_nt(km_lo, q))[:nb_pad, :]
        g = jnp.where(past, gate_t, LOW)
        rank = jnp.zeros(g.shape, jnp.int32)
        for dlt in range(1, nb_pad):
            other = pltpu.roll(g, dlt, axis=0)
            rank = rank + jnp.where(blk_row >= dlt, (other >= g).astype(jnp.int32),
                                    (other > g).astype(jnp.int32))
        chosen_t = jnp.logical_or(jnp.logical_and(rank < MOBA_TOPK, past), blk_row == qi)
        bias_t = jnp.concatenate([jnp.where(chosen_t, 0.0, NEG), jnp.full((LANES - nb_pad, blk), NEG, F32)],
                                 axis=0)
        q_aug.append(jnp.concatenate([q, bias_t.T.astype(BF16)], axis=1))

    pq = pq_ref[...]

    def step(n, carry, masked):
        ks = pl.multiple_of(n * blk, blk)
        onehot = jnp.where(lane == n, 1.0, 0.0).astype(BF16)
        dist_t = jnp.abs(pk_ref[pl.ds(ks, blk), :] - pq)
        s_all = []
        for h in range(hp):
            k_aug = jnp.concatenate([k_ref[pl.ds(ks, blk), h * hd:(h + 1) * hd], onehot], axis=1)
            s_t = _dot_nt(k_aug, q_aug[h]) - slope_ref[hg * hp + h] * dist_t
            if masked:
                key = lax.broadcasted_iota(jnp.int32, s_t.shape, 0)
                qry = lax.broadcasted_iota(jnp.int32, s_t.shape, 1)
                s_t = jnp.where(key <= qry, s_t, NEG)
            s_all.append(s_t)
        stats = [_softmax_stats_t(s_all[h], carry[h][0], carry[h][1]) for h in range(hp)]
        out = []
        for h in range(hp):
            m_new, l, alpha, p = stats[h]
            out.append((m_new, l, alpha * carry[h][2] + _dot(vt_ref[n, h * hd:(h + 1) * hd, :], p)))
        return tuple(out)

    init = tuple((jnp.full((1, blk), M_INIT, F32), jnp.zeros((1, blk), F32), jnp.zeros((hd, blk), F32))
                 for _ in range(hp))
    carry = lax.fori_loop(0, qi, lambda n, c: step(n, c, False), init)
    carry = step(qi, carry, True)
    for h in range(hp):
        _, l, acc_t = carry[h]
        o_ref[:, h * hd:(h + 1) * hd] = (acc_t / jnp.maximum(l, TINY)).T.astype(BF16)


def _alibi_slopes(n):
    return 2.0 ** (-8.0 * jnp.arange(1, n + 1, dtype=F32) / n)


def _moba_mixer(x, mod, pos_col, pos_f, w_in):
    bsz, seq, d = x.shape
    hh, hd, blk = MOBA_HEADS, MOBA_HD, MOBA_BLOCK
    assert seq % blk == 0 and seq // blk + MOBA_TOPK < LANES and blk == ATT_TK
    n_blocks = seq // blk
    hp = HEADS_PER_STEP
    nqk = 2 * hh * hd
    col_scale = jnp.concatenate([jnp.full((hh * hd,), hd ** -0.5 * LOG2E, F32),
                                 jnp.ones((hh * hd,), F32)]).reshape(1, -1)
    w_bf = w_in.astype(BF16)
    qk, v_t = _project(x, mod, w_bf[:, :nqk], col_scale, nqk, w_bf[:, nqk:].T)
    pos_row = pos_f.reshape(bsz, n_blocks, 1, blk)
    return pl.pallas_call(
        functools.partial(_moba_kernel, n_blocks=n_blocks),
        grid=(bsz, hh // hp, n_blocks),
        in_specs=[pl.BlockSpec(memory_space=pltpu.SMEM),
                  pl.BlockSpec((None, blk, hp * hd), lambda b, h, i: (b, i, h)),
                  pl.BlockSpec((None, seq, hp * hd), lambda b, h, i: (b, 0, hh // hp + h)),
                  pl.BlockSpec((None, n_blocks, hp * hd, blk), lambda b, h, i: (b, 0, h, 0)),
                  pl.BlockSpec((None, seq, 1), lambda b, h, i: (b, 0, 0)),
                  pl.BlockSpec((None, None, 1, blk), lambda b, h, i: (b, i, 0, 0))],
        out_specs=pl.BlockSpec((None, blk, hp * hd), lambda b, h, i: (b, i, h)),
        out_shape=jax.ShapeDtypeStruct((bsz, seq, hh * hd), BF16),
        scratch_shapes=[pltpu.VMEM((hp, LANES, hd), F32)],
        compiler_params=_cparams(3),
        name="moba_attention",
    )(_alibi_slopes(hh) * LOG2E, qk, qk, v_t, pos_col, pos_row)


def _sb_kernel(q_ref, k_ref, vt_ref, o_ref):
    qi = pl.program_id(2)
    tq, tk, hp, hd = q_ref.shape[0], ATT_TK, HEADS_PER_STEP, SB_HD
    rs = lax.broadcasted_iota(jnp.int32, (tk, tk), 0)
    cj = lax.broadcasted_iota(jnp.int32, (tk, tk), 1)
    after = jnp.where(cj > rs, 1.0, 0.0).astype(BF16)
    after2 = jnp.concatenate([after, after], axis=1)

    def step(j, carry, masked):
        ks = pl.multiple_of(j * tk, tk)
        z_all = [_dot_nt(k_ref[pl.ds(ks, tk), h * hd:(h + 1) * hd], q_ref[:, h * hd:(h + 1) * hd])
                 for h in range(hp)]
        if masked:
            key = lax.broadcasted_iota(jnp.int32, (tk, tq), 0)
            qry = lax.broadcasted_iota(jnp.int32, (tk, tq), 1)
            mask = key < qry
        parts = []
        for h in range(hp):
            z2 = z_all[h]
            log_beta = jnp.minimum(z2, 0.0) - jnp.log2(1.0 + jnp.exp2(-jnp.abs(z2)))
            log_1mb = log_beta - z2
            if masked:
                log_1mb = jnp.where(mask, log_1mb, 0.0)
            hi = log_1mb.astype(BF16)
            lo = (log_1mb - hi.astype(F32)).astype(BF16)
            parts.append((log_beta, log_1mb, jnp.concatenate([hi, lo], axis=0)))
        tails = [_dot(after2, parts[h][2]) for h in range(hp)]
        out = []
        for h in range(hp):
            c, acc_t = carry[h]
            log_beta, log_1mb = parts[h][0], parts[h][1]
            a = jnp.exp2(log_beta + tails[h] + c)
            if masked:
                a = jnp.where(mask, a, 0.0)
            acc_t = acc_t + _dot(vt_ref[j, h * hd:(h + 1) * hd, :], a.astype(BF16))
            out.append((c + jnp.sum(log_1mb, axis=0, keepdims=True), acc_t))
        return tuple(out)

    init = tuple((jnp.zeros((1, tq), F32), jnp.zeros((hd, tq), F32)) for _ in range(hp))
    carry = step(qi, init, True)
    carry = lax.fori_loop(0, qi, lambda t, cr: step(qi - 1 - t, cr, False), carry)
    for h in range(hp):
        o_ref[:, h * hd:(h + 1) * hd] = carry[h][1].T.astype(BF16)


def _sb_mixer(x, mod, w_in):
    bsz, seq, d = x.shape
    hh, hd, hp = SB_HEADS, SB_HD, HEADS_PER_STEP
    nqk = 2 * hh * hd
    col_scale = jnp.concatenate([jnp.full((hh * hd,), hd ** -0.5 * LOG2E, F32),
                                 jnp.ones((hh * hd,), F32)]).reshape(1, -1)
    w_bf = w_in.astype(BF16)
    qk, v_t = _project(x, mod, w_bf[:, :nqk], col_scale, nqk, w_bf[:, nqk:].T)
    tq = ATT_TQ
    assert tq == ATT_TK
    return pl.pallas_call(
        _sb_kernel,
        grid=(bsz, hh // hp, seq // tq),
        in_specs=[pl.BlockSpec((None, tq, hp * hd), lambda b, h, i: (b, i, h)),
                  pl.BlockSpec((None, seq, hp * hd), lambda b, h, i: (b, 0, hh // hp + h)),
                  pl.BlockSpec((None, seq // ATT_TK, hp * hd, ATT_TK), lambda b, h, i: (b, 0, h, 0))],
        out_specs=pl.BlockSpec((None, tq, hp * hd), lambda b, h, i: (b, i, h)),
        out_shape=jax.ShapeDtypeStruct((bsz, seq, hh * hd), BF16),
        compiler_params=_cparams(3),
        name="stick_breaking_attention",
    )(qk, qk, v_t)


def _nsa_compress_kernel(t_ref, pe_ref, w1_ref, w2_ref, o_ref, ot_ref, *, n_cmp):
    st, half = NSA_CMP_STRIDE, NSA_CMP_LEN // NSA_CMP_STRIDE
    assert half == 2
    rows = t_ref.shape[0] // st
    xa, xb = [], []
    for r in range(st):
        xr = t_ref[pl.ds(r, rows, stride=st), :]
        xa.append((xr + pe_ref[r:r + 1, :]).astype(BF16))
        xb.append((xr + pe_ref[st + r:st + r + 1, :]).astype(BF16))
    k_half = st * NSA_HD
    pre_a = _dot(jnp.concatenate(xa, axis=1), w1_ref[:k_half, :])
    pre_b = _dot(jnp.concatenate(xb, axis=1), w1_ref[k_half:, :])
    pre = pre_a + pltpu.roll(pre_b, rows - 1, axis=0)
    out = _dot(_gelu_tanh(pre).astype(BF16), w2_ref[...])
    row = lax.broadcasted_iota(jnp.int32, out.shape, 0)
    out = jnp.where(row < n_cmp, out, 0.0)
    o_ref[...] = out.astype(BF16)
    ot_ref[...] = out.T.astype(BF16)


def _nsa_cmp_select_kernel(slope_ref, q_ref, kc_ref, vct_ref, gt_ref, pq_ref, pe_ref, ovlt_ref,
                           acc_ref, sel_ref, *, n_cmp, n_sel):
    qi = pl.program_id(1)
    tq, gg, rr, hd = q_ref.shape[0], NSA_GROUPS, NSA_R, NSA_HD
    shape = (LANES, tq)
    n_row = lax.broadcasted_iota(jnp.int32, shape, 0)
    qidx = qi * tq + lax.broadcasted_iota(jnp.int32, shape, 1)
    c_end = n_row * NSA_CMP_STRIDE + (NSA_CMP_LEN - 1)
    mask_c = jnp.logical_and(c_end <= qidx, n_row < n_cmp)
    dist_t = jnp.abs(pe_ref[...] - pq_ref[...])
    sig_t = jax.nn.sigmoid(gt_ref[...]).T
    ovl_t = ovlt_ref[...]

    heads = [(g, r) for g in range(gg) for r in range(rr)]
    s_all = [_dot_nt(kc_ref[g], q_ref[:, (g * rr + r) * hd:(g * rr + r + 1) * hd])
             - slope_ref[g * rr + r] * dist_t for g, r in heads]
    p_all = []
    for s_t in s_all:
        s_t = jnp.where(mask_c, s_t, NEG)
        p = jnp.where(mask_c, jnp.exp2(s_t - jnp.max(s_t, axis=0, keepdims=True)), 0.0)
        p_all.append((p / jnp.maximum(jnp.sum(p, axis=0, keepdims=True), TINY)).astype(BF16))
    imp_t = [jnp.zeros((n_sel, tq), F32) for _ in range(gg)]
    for (g, r), p in zip(heads, p_all):
        imp_t[g] = imp_t[g] + _dot(ovl_t, p)
        c = g * rr + r
        acc_ref[:, c * hd:(c + 1) * hd] = (sig_t[c:c + 1, :] * _dot(vct_ref[gg + g], p)).T

    sshape = (n_sel, tq)
    j_row = lax.broadcasted_iota(jnp.int32, sshape, 0)
    q_blk = (qi * tq + lax.broadcasted_iota(jnp.int32, sshape, 1)) >> 6
    assert NSA_SEL_BLOCK == 64
    forced = jnp.logical_or(j_row == 0, jnp.logical_or(j_row == q_blk, j_row == q_blk - 1))
    cand = j_row <= q_blk
    bonus = jnp.where(forced, NSA_FORCE_BONUS, 0.0)
    for g in range(gg):
        score = jnp.where(cand, imp_t[g] + bonus, NEG)
        rank = jnp.zeros(sshape, jnp.int32)
        for dlt in range(1, n_sel):
            other = pltpu.roll(score, dlt, axis=0)
            rank = rank + jnp.where(j_row >= dlt, (other >= score).astype(jnp.int32),
                                    (other > score).astype(jnp.int32))
        chosen_t = jnp.logical_and(rank < NSA_SEL_TOPN, cand)
        bias_t = jnp.concatenate([jnp.where(chosen_t, 0.0, NEG), jnp.full((LANES - n_sel, tq), NEG, F32)],
                                 axis=0)
        sel_ref[g] = bias_t.T.astype(BF16)


def _nsa_select_window_kernel(slope_ref, q_ref, sel_ref, k_ref, vt_ref, gt_ref, pk_ref, pq_ref, prev_ref,
                              out_ref):
    qi = pl.program_id(1)
    tq, tk, hd, gg, rr = q_ref.shape[0], ATT_TK, NSA_HD, NSA_GROUPS, NSA_R
    assert tq == tk and NSA_WINDOW % tk == 0
    lanes = rr * tq
    pq4 = jnp.concatenate([pq_ref[...]] * rr, axis=1)
    qidx = qi * tq + (lax.broadcasted_iota(jnp.int32, (tk, lanes), 1) & (tq - 1))
    krow = lax.broadcasted_iota(jnp.int32, (tk, lanes), 0)
    klane = lax.broadcasted_iota(jnp.int32, (tk, LANES), 1)
    kblk = lax.broadcasted_iota(jnp.int32, (tk, LANES), 0) >> 6
    per_tile = tk // NSA_SEL_BLOCK
    slope_rows = [jnp.concatenate([jnp.full((1, tq), slope_ref[g * rr + r], F32) for r in range(rr)], axis=1)
                  for g in range(gg)]
    q4 = [jnp.concatenate([q_ref[:, (g * rr + r) * hd:(g * rr + r + 1) * hd] for r in range(rr)], axis=0)
          for g in range(gg)]
    q_aug = [jnp.concatenate([q4[g], jnp.concatenate([sel_ref[g]] * rr, axis=0)], axis=1) for g in range(gg)]
    k_win0 = gg * hd

    def update(s_all, carry, v_row0, j):
        stats = [_softmax_stats_t(s_all[g], carry[g][0], carry[g][1]) for g in range(gg)]
        out = []
        for g in range(gg):
            m_new, l, alpha, p = stats[g]
            v_t = vt_ref[j, v_row0 + g * hd:v_row0 + (g + 1) * hd, :]
            out.append((m_new, l, alpha * carry[g][2] + _dot(v_t, p)))
        return tuple(out)

    def select_step(j, carry, masked):
        ks = pl.multiple_of(j * tk, tk)
        dist = jnp.abs(pk_ref[pl.ds(ks, tk), :] - pq4)
        onehot = jnp.where(klane == j * per_tile + kblk, 1.0, 0.0).astype(BF16)
        s_all = []
        for g in range(gg):
            k_aug = jnp.concatenate([k_ref[pl.ds(ks, tk), g * hd:(g + 1) * hd], onehot], axis=1)
            s_t = _dot_nt(k_aug, q_aug[g]) - slope_rows[g] * dist
            if masked:
                s_t = jnp.where(j * tk + krow <= qidx, s_t, NEG)
            s_all.append(s_t)
        return update(s_all, carry, 0, j)

    def init():
        return tuple((jnp.full((1, lanes), M_INIT, F32), jnp.zeros((1, lanes), F32),
                      jnp.zeros((hd, lanes), F32)) for _ in range(gg))

    sel = lax.fori_loop(0, qi, lambda j, c: select_step(j, c, False), init())
    sel = select_step(qi, sel, True)

    win = init()
    n_band = NSA_WINDOW // tk + 1
    qloc = qidx - qi * tq
    for t in range(n_band):
        jt = qi - (n_band - 1) + t
        jc = jnp.maximum(jt, 0)
        ks = pl.multiple_of(jc * tk, tk)
        none = jnp.where(jt >= 0, 0, tk)
        if t == n_band - 1:
            mask = krow <= qloc
        elif t == 0:
            mask = krow > qloc + none
        else:
            mask = krow >= none
        dist = jnp.abs(pk_ref[pl.ds(ks, tk), :] - pq4)
        s_all = []
        for g in range(gg):
            s_t = _dot_nt(k_ref[pl.ds(ks, tk), k_win0 + g * hd:k_win0 + (g + 1) * hd], q4[g])
            s_all.append(jnp.where(mask, s_t - slope_rows[g] * dist, NEG))
        win = update(s_all, win, gg * hd, jc)

    sig_t = jax.nn.sigmoid(gt_ref[...]).T
    for g in range(gg):
        gate = lambda branch: jnp.concatenate(
            [sig_t[branch * NSA_HEADS + g * rr + r:branch * NSA_HEADS + g * rr + r + 1, :] for r in range(rr)],
            axis=1)
        o_t = (gate(1) * (sel[g][2] / jnp.maximum(sel[g][1], TINY))
               + gate(2) * (win[g][2] / jnp.maximum(win[g][1], TINY)))
        for r in range(rr):
            c = g * rr + r
            out_ref[:, c * hd:(c + 1) * hd] = (prev_ref[:, c * hd:(c + 1) * hd]
                                                + o_t[:, r * tq:(r + 1) * tq].T).astype(BF16)


def _nsa_mixer(x, mod, pos_col, pos_f, w_in, cmp_pos, cmp_w1, cmp_w2):
    bsz, seq, d = x.shape
    hh, gg, hd = NSA_HEADS, NSA_GROUPS, NSA_HD
    n_cmp = (seq - NSA_CMP_LEN) // NSA_CMP_STRIDE + 1
    n_sel = seq // NSA_SEL_BLOCK
    tq = ATT_TQ
    assert seq % tq == 0 and seq // NSA_CMP_STRIDE == LANES and n_sel % SUBLANES == 0 and n_sel <= LANES
    assert NSA_SEL_TOPN <= n_sel
    nq, nkv = hh * hd, gg * hd
    kv = lambda t: w_in[:, nq + t * nkv:nq + (t + 1) * nkv]
    n_gate = 3 * hh
    w_perm = jnp.concatenate([w_in[:, :nq], kv(2), kv(4), kv(0), kv(1),
                              w_in[:, nq + 6 * nkv:], jnp.zeros((d, LANES - n_gate), F32)], axis=1)
    n_main = nq + 2 * nkv
    col_scale = jnp.concatenate([jnp.full((nq,), hd ** -0.5 * LOG2E, F32),
                                 jnp.ones((w_perm.shape[1] - nq,), F32)]).reshape(1, -1)
    w_vt = jnp.concatenate([kv(3), kv(5)], axis=1).T.astype(BF16)
    main, v_t, tail = _project(x, mod, w_perm.astype(BF16), col_scale, n_main, w_vt)
    slopes = _alibi_slopes(hh) * LOG2E
    gate_blk = 2 * nkv // LANES

    cmp_spec = pl.BlockSpec((None, None, LANES, hd), lambda b, w, g: (b, w * gg + g, 0, 0))
    cmp_kv, cmp_kv_t = pl.pallas_call(
        functools.partial(_nsa_compress_kernel, n_cmp=n_cmp),
        grid=(bsz, 2, gg),
        in_specs=[pl.BlockSpec((None, seq, hd), lambda b, w, g: (b, 0, w * gg + g)),
                  pl.BlockSpec((None, NSA_CMP_LEN, hd), lambda b, w, g: (w, 0, 0)),
                  pl.BlockSpec((None, NSA_CMP_LEN * hd, hd), lambda b, w, g: (w, 0, 0)),
                  pl.BlockSpec((None, hd, hd), lambda b, w, g: (w, 0, 0))],
        out_specs=[cmp_spec, cmp_spec],
        out_shape=[jax.ShapeDtypeStruct((bsz, 2 * gg, LANES, hd), BF16)] * 2,
        compiler_params=_cparams(3),
        name="nsa_compress",
    )(tail, cmp_pos, cmp_w1.astype(BF16), cmp_w2.astype(BF16))

    pos_cend = jnp.pad(pos_f[:, NSA_CMP_LEN - 1::NSA_CMP_STRIDE], ((0, 0), (0, LANES - n_cmp)))
    pos_cend = pos_cend.reshape(bsz, LANES, 1)
    c_start = jnp.arange(LANES) * NSA_CMP_STRIDE
    j_start = jnp.arange(n_sel) * NSA_SEL_BLOCK
    overlap_t = ((c_start[None, :] < j_start[:, None] + NSA_SEL_BLOCK)
                 & (c_start[None, :] + NSA_CMP_LEN - 1 >= j_start[:, None])
                 & (jnp.arange(LANES)[None, :] < n_cmp)).astype(BF16)

    smem = pl.BlockSpec(memory_space=pltpu.SMEM)
    q_spec = pl.BlockSpec((None, tq, nq), lambda b, i: (b, i, 0))
    gate_spec = pl.BlockSpec((None, tq, LANES), lambda b, i: (b, i, gate_blk))
    pq_spec = pl.BlockSpec((None, None, 1, tq), lambda b, i: (b, i, 0, 0))
    sel_spec = pl.BlockSpec((None, gg, tq, LANES), lambda b, i: (b, 0, i, 0))
    cmp_all = pl.BlockSpec((None, 2 * gg, LANES, hd), lambda b, i: (b, 0, 0, 0))
    pos_q = pos_f.reshape(bsz, seq // tq, 1, tq)
    grid = (bsz, seq // tq)

    acc0, sel_bias = pl.pallas_call(
        functools.partial(_nsa_cmp_select_kernel, n_cmp=n_cmp, n_sel=n_sel),
        grid=grid,
        in_specs=[smem, q_spec, cmp_all, cmp_all, gate_spec, pq_spec,
                  pl.BlockSpec((None, LANES, 1), lambda b, i: (b, 0, 0)),
                  pl.BlockSpec((n_sel, LANES), lambda b, i: (0, 0))],
        out_specs=[q_spec, sel_spec],
        out_shape=[jax.ShapeDtypeStruct((bsz, seq, nq), F32),
                   jax.ShapeDtypeStruct((bsz, gg, seq, LANES), BF16)],
        compiler_params=_cparams(2),
        name="nsa_compressed_attention_select",
    )(slopes, main, cmp_kv, cmp_kv_t, tail, pos_q, pos_cend, overlap_t)

    return pl.pallas_call(
        _nsa_select_window_kernel,
        grid=grid,
        in_specs=[smem, q_spec, sel_spec,
                  pl.BlockSpec((None, seq, 2 * nkv), lambda b, i: (b, 0, nq // (2 * nkv))),
                  pl.BlockSpec((None, seq // ATT_TK, 2 * nkv, ATT_TK), lambda b, i: (b, 0, 0, 0)),
                  gate_spec,
                  pl.BlockSpec((None, seq, 1), lambda b, i: (b, 0, 0)),
                  pq_spec, q_spec],
        out_specs=q_spec,
        out_shape=jax.ShapeDtypeStruct((bsz, seq, nq), BF16),
        compiler_params=_cparams(2),
        name="nsa_selected_window_attention",
    )(slopes, main, sel_bias, main, v_t, tail, pos_col, pos_q, acc0)


def kernel(x, c, positions, mod_w, mod_b, ln_g, ln_b, ffn_w_in, ffn_conv_w, ffn_conv_b, ffn_w_out,
           mla_w_in, mla_q_norm, mla_w_uq, mla_kv_norm, mla_w_ukv, mla_w_o,
           moba_w_in, moba_w_o, nsa_w_in, nsa_cmp_pos, nsa_cmp_w1, nsa_cmp_w2, nsa_w_o,
           sb_w_in, sb_w_o):
    bsz, seq, d = x.shape
    depth = mod_w.shape[0]
    assert d == D_MODEL and seq % TOKEN_TILE == 0 and seq % ATT_TQ == 0
    alpha = float((2 * depth) ** 0.25)
    pos_f = positions.astype(F32)
    pos_col = pos_f.reshape(bsz, seq, 1)
    mod_all = _modulation(c, mod_w, mod_b)
    for i in range(depth):
        kind, j = i % N_MIXERS, i // N_MIXERS
        mod = mod_all[i]
        if kind == 0:
            o = _mla_mixer(x, mod, pos_col, mla_w_in[j], mla_q_norm[j], mla_w_uq[j], mla_kv_norm[j],
                           mla_w_ukv[j])
            w_o = mla_w_o[j]
        elif kind == 1:
            o = _moba_mixer(x, mod, pos_col, pos_f, moba_w_in[j])
            w_o = moba_w_o[j]
        elif kind == 2:
            o = _nsa_mixer(x, mod, pos_col, pos_f, nsa_w_in[j], nsa_cmp_pos[j], nsa_cmp_w1[j],
                           nsa_cmp_w2[j])
            w_o = nsa_w_o[j]
        else:
            o = _sb_mixer(x, mod, sb_w_in[j])
            w_o = sb_w_o[j]
        w_ffn = ffn_w_in[i].astype(BF16)
        x = _mixer_out_ffn(o, w_o.astype(BF16), x, mod, ln_g[i], ln_b[i], w_ffn[:, :D_FF], w_ffn[:, D_FF:],
                           ffn_conv_w[i], ffn_conv_b[i], ffn_w_out[i].astype(BF16), alpha)
    return x
```

```python
import functools

import jax
import jax.numpy as jnp
from jax import lax
from jax.experimental import pallas as pl
from jax.experimental.pallas import tpu as pltpu

F32 = jnp.float32
BF16 = jnp.bfloat16

D_MODEL = 1024
N_MIXERS = 4
MLA_HEADS, MLA_NOPE, MLA_ROPE, MLA_V = 8, 128, 64, 128
MLA_Q_RANK, MLA_KV_RANK = 256, 256
ROPE_BASE = 10000.0
MOBA_HEADS, MOBA_HD, MOBA_BLOCK, MOBA_TOPK = 8, 128, 256, 3
NSA_HEADS, NSA_GROUPS, NSA_HD = 8, 2, 128
NSA_R = NSA_HEADS // NSA_GROUPS
NSA_CMP_LEN, NSA_CMP_STRIDE, NSA_SEL_BLOCK, NSA_SEL_TOPN, NSA_WINDOW = 32, 16, 64, 16, 512
NSA_FORCE_BONUS = 100.0
SB_HEADS, SB_HD = 8, 128
D_FF = 2816
LN_EPS = 1e-5
RMS_EPS = 1e-6
NEG = -1e30
M_INIT = -1e29
LOW = -3e38
TINY = 1e-30
LOG2E = 1.4426950408889634
SB_DEAD_LOG2 = -150.0

LANES = 128
SUBLANES = 8
VMEM_LIMIT_BYTES = 56 * 1024 * 1024

TOKEN_TILE = 512
ATT_TQ = 256
ATT_TK = 256
HEADS_PER_STEP = 8
FF_CHUNK = 256
PROJ_CHUNK = 512


def _cparams(n_grid):
    return pltpu.CompilerParams(dimension_semantics=("arbitrary",) * n_grid,
                                vmem_limit_bytes=VMEM_LIMIT_BYTES)


def _dot(a, b):
    return jnp.dot(a, b, preferred_element_type=F32)


def _dot_nt(a, b):
    return lax.dot_general(a, b, (((1,), (1,)), ((), ())), preferred_element_type=F32)


def _layer_norm(z, g, b):
    mu = jnp.mean(z, axis=-1, keepdims=True)
    d = z - mu
    var = jnp.mean(d * d, axis=-1, keepdims=True)
    return d * lax.rsqrt(var + LN_EPS) * g + b


def _rms_norm(z, g):
    return z * lax.rsqrt(jnp.mean(z * z, axis=-1, keepdims=True) + RMS_EPS) * g


def _gelu_tanh(x):
    return 0.5 * x * (1.0 + jnp.tanh(0.7978845608028654 * (x + 0.044715 * (x * x * x))))


def _softmax_stats_t(s_t, m):
    m_new = jnp.maximum(m, jnp.max(s_t, axis=0, keepdims=True))
    return m_new, jnp.exp2(m - m_new), jnp.exp2((s_t - m_new).astype(BF16))


def _with_ones_rows(v_t):
    return jnp.concatenate([v_t, jnp.ones((2 * SUBLANES, v_t.shape[1]), v_t.dtype)], axis=0)


def _normalize_t(acc_t, hd):
    return acc_t[:hd, :] / jnp.maximum(acc_t[hd:hd + 1, :], TINY)


def _mod_kernel(c_ref, w_ref, b_ref, o_ref):
    c = c_ref[...]
    c_act = c * jax.nn.sigmoid(c)
    o_ref[...] = jnp.dot(c_act, w_ref[...], preferred_element_type=F32,
                         precision=lax.Precision.HIGHEST) + b_ref[...]


def _modulation(c, mod_w, mod_b):
    depth, d, n = mod_w.shape
    bsz = c.shape[0]
    tn = 1536
    out = pl.pallas_call(
        _mod_kernel,
        grid=(depth, n // tn),
        in_specs=[pl.BlockSpec((bsz, d), lambda l, j: (0, 0)),
                  pl.BlockSpec((None, d, tn), lambda l, j: (l, 0, j)),
                  pl.BlockSpec((None, 1, tn), lambda l, j: (l, 0, j))],
        out_specs=pl.BlockSpec((None, bsz, tn), lambda l, j: (l, 0, j)),
        out_shape=jax.ShapeDtypeStruct((depth, bsz, n), F32),
        compiler_params=_cparams(2),
        name="modulation",
    )(c, mod_w, mod_b.reshape(depth, 1, n))
    return out.reshape(depth, bsz, 6, d)


def _proj_kernel(x_ref, mod_ref, w_ref, cs_ref, wt_ref, *out_refs, n_main, n_total):
    h = (x_ref[...] * (1.0 + mod_ref[1:2, :]) + mod_ref[0:1, :]).astype(BF16)
    for n0 in range(0, n_total, PROJ_CHUNK):
        n1 = min(n0 + PROJ_CHUNK, n_total)
        y = _dot(h, w_ref[:, n0:n1]) * cs_ref[:, n0:n1]
        if n0 < n_main:
            out_refs[0][:, n0:n1] = y.astype(BF16)
        else:
            out_refs[2][:, n0 - n_main:n1 - n_main] = y
    vt_ref = out_refs[1]
    for n0 in range(0, wt_ref.shape[0], PROJ_CHUNK):
        y_t = _dot_nt(wt_ref[n0:n0 + PROJ_CHUNK, :], h).astype(BF16)
        for t in range(vt_ref.shape[0]):
            vt_ref[t, n0:n0 + PROJ_CHUNK, :] = y_t[:, t * ATT_TK:(t + 1) * ATT_TK]


def _project(x, mod, w, col_scale, n_main, w_t):
    bsz, seq, d = x.shape
    n_total, n_t = w.shape[1], w_t.shape[0]
    assert n_main % PROJ_CHUNK == 0 and n_total % LANES == 0 and n_t % PROJ_CHUNK == 0
    tm = TOKEN_TILE
    out_shape = [jax.ShapeDtypeStruct((bsz, seq, n_main), BF16),
                 jax.ShapeDtypeStruct((bsz, seq // ATT_TK, n_t, ATT_TK), BF16)]
    out_specs = [pl.BlockSpec((None, tm, n_main), lambda b, i: (b, i, 0)),
                 pl.BlockSpec((None, tm // ATT_TK, n_t, ATT_TK), lambda b, i: (b, i, 0, 0))]
    if n_main < n_total:
        out_shape.append(jax.ShapeDtypeStruct((bsz, seq, n_total - n_main), F32))
        out_specs.append(pl.BlockSpec((None, tm, n_total - n_main), lambda b, i: (b, i, 0)))
    return pl.pallas_call(
        functools.partial(_proj_kernel, n_main=n_main, n_total=n_total),
        grid=(bsz, seq // tm),
        in_specs=[pl.BlockSpec((None, tm, d), lambda b, i: (b, i, 0)),
                  pl.BlockSpec((None, 6, d), lambda b, i: (b, 0, 0)),
                  pl.BlockSpec((d, n_total), lambda b, i: (0, 0)),
                  pl.BlockSpec((1, n_total), lambda b, i: (0, 0)),
                  pl.BlockSpec((n_t, d), lambda b, i: (0, 0))],
        out_specs=out_specs,
        out_shape=out_shape,
        compiler_params=_cparams(2),
        name="modulate_project",
    )(x, mod, w, col_scale, w_t)


def _mixer_out_ffn_kernel(o_ref, wmix_ref, x_ref, mod_ref, g1_ref, b1_ref, win_ref, cw_ref, cb_ref,
                          wo_ref, g2_ref, b2_ref, out_ref, a_scr, g_scr, halo_scr, *, alpha):
    tm = x_ref.shape[0]
    halo = halo_scr.shape[0]
    x1 = _layer_norm(alpha * x_ref[...] + mod_ref[2:3, :] * _dot(o_ref[...], wmix_ref[...]),
                     g1_ref[...], b1_ref[...])
    shift, scale, gate = mod_ref[3:4, :], mod_ref[4:5, :], mod_ref[5:6, :]
    h = (x1 * (1.0 + scale) + shift).astype(BF16)

    @pl.when(pl.program_id(1) == 0)
    def _():
        halo_scr[...] = jnp.zeros_like(halo_scr)

    h_ext = jnp.concatenate([halo_scr[...], h], axis=0)
    halo_scr[...] = h[tm - halo:, :]
    chunks = [(c0, min(c0 + FF_CHUNK, D_FF)) for c0 in range(0, D_FF, FF_CHUNK)]

    def matmuls(i):
        c0, c1 = chunks[i]
        a_scr[i % 2, :, :c1 - c0] = _dot(h_ext, win_ref[:, c0:c1])
        return _dot(h, win_ref[:, D_FF + c0:D_FF + c1])

    b_next = matmuls(0)
    for i, (c0, c1) in enumerate(chunks):
        b = b_next
        if i + 1 < len(chunks):
            b_next = matmuls(i + 1)
        w = c1 - c0
        a = a_scr[i % 2, halo:halo + tm, :w]
        a_m1 = a_scr[i % 2, halo - 1:halo - 1 + tm, :w]
        a_m2 = a_scr[i % 2, halo - 2:halo - 2 + tm, :w]
        conv = (a * cw_ref[2:3, c0:c1] + a_m1 * cw_ref[1:2, c0:c1] + a_m2 * cw_ref[0:1, c0:c1]
                + cb_ref[:, c0:c1])
        g_scr[:, c0:c1] = (_gelu_tanh(conv) * b).astype(BF16)
    y = _dot(g_scr[...], wo_ref[...])
    out_ref[...] = _layer_norm(alpha * x1 + gate * y, g2_ref[...], b2_ref[...])


def _mixer_out_ffn(o, w_mix, x, mod, ln_g, ln_b, w_in, conv_w, conv_b, w_out, alpha):
    bsz, seq, d = x.shape
    k = o.shape[-1]
    tm = TOKEN_TILE
    halo = 2 * SUBLANES
    const = lambda shape: pl.BlockSpec(shape, lambda b, i: (0, 0), pipeline_mode=pl.Buffered(1))
    return pl.pallas_call(
        functools.partial(_mixer_out_ffn_kernel, alpha=alpha),
        grid=(bsz, seq // tm),
        in_specs=[pl.BlockSpec((None, tm, k), lambda b, i: (b, i, 0)),
                  const((k, d)),
                  pl.BlockSpec((None, tm, d), lambda b, i: (b, i, 0)),
                  pl.BlockSpec((None, 6, d), lambda b, i: (b, 0, 0)),
                  const((1, d)), const((1, d)),
                  const((d, 2 * D_FF)), const((3, D_FF)), const((1, D_FF)),
                  const((D_FF, d)), const((1, d)), const((1, d))],
        out_specs=pl.BlockSpec((None, tm, d), lambda b, i: (b, i, 0)),
        out_shape=jax.ShapeDtypeStruct((bsz, seq, d), F32),
        scratch_shapes=[pltpu.VMEM((2, halo + tm, FF_CHUNK), F32), pltpu.VMEM((tm, D_FF), BF16),
                        pltpu.VMEM((halo, d), BF16)],
        compiler_params=_cparams(2),
        name="mixer_out_conv_ffn",
    )(o, w_mix, x, mod, ln_g[0].reshape(1, d), ln_b[0].reshape(1, d), w_in, conv_w,
      conv_b.reshape(1, D_FF), w_out, ln_g[1].reshape(1, d), ln_b[1].reshape(1, d))


def _mla_proj_kernel(x_ref, mod_ref, pos_ref, invf_ref, win_ref, qn_ref, kvn_ref, wuq_ref,
                     wuqr_ref, wuk_ref, wuvt_ref, q_ref, kn_ref, kr_ref, vt_ref, *, scale):
    h = (x_ref[...] * (1.0 + mod_ref[1:2, :]) + mod_ref[0:1, :]).astype(BF16)
    proj = _dot(h, win_ref[...])
    c_q = _rms_norm(proj[:, :MLA_Q_RANK], qn_ref[...]).astype(BF16)
    c_kv = _rms_norm(proj[:, MLA_Q_RANK:MLA_Q_RANK + MLA_KV_RANK], kvn_ref[...]).astype(BF16)
    ang = pos_ref[...] * invf_ref[...]
    cos, sin = jnp.cos(ang), jnp.sin(ang)
    r0 = MLA_Q_RANK + MLA_KV_RANK
    kr_ref[...] = (proj[:, r0:r0 + LANES] * cos + proj[:, r0 + LANES:r0 + 2 * LANES] * sin).astype(BF16)
    for hd in range(MLA_HEADS):
        qa = _dot(c_q, wuq_ref[:, hd * 2 * LANES:(hd + 1) * 2 * LANES])
        qr = _dot(c_q, wuqr_ref[:, hd * LANES:(hd + 1) * LANES])
        q_ref[:, hd * 2 * LANES:hd * 2 * LANES + LANES] = (qa[:, :LANES] * scale).astype(BF16)
        q_ref[:, hd * 2 * LANES + LANES:(hd + 1) * 2 * LANES] = (
            (qa[:, LANES:] * cos + qr * sin) * scale).astype(BF16)
    kn_ref[...] = _dot(c_kv, wuk_ref[...]).astype(BF16)
    v_t = _dot_nt(wuvt_ref[...], c_kv).astype(BF16)
    for t in range(vt_ref.shape[0]):
        vt_ref[t] = v_t[:, t * ATT_TK:(t + 1) * ATT_TK]


def _mla_attn_kernel(q_ref, kn_ref, kr_ref, vt_ref, o_ref):
    qi = pl.program_id(2)
    tq = q_ref.shape[0]
    hp = HEADS_PER_STEP

    def step(j, carry, masked):
        ks = pl.multiple_of(j * ATT_TK, ATT_TK)
        kr = kr_ref[pl.ds(ks, ATT_TK), :]
        def scores(h):
            k = jnp.concatenate([kn_ref[pl.ds(ks, ATT_TK), h * MLA_NOPE:(h + 1) * MLA_NOPE], kr], axis=1)
            s_t = _dot_nt(k, q_ref[:, h * 2 * LANES:(h + 1) * 2 * LANES])
            if masked:
                key = lax.broadcasted_iota(jnp.int32, s_t.shape, 0)
                qry = lax.broadcasted_iota(jnp.int32, s_t.shape, 1)
                s_t = jnp.where(key <= qry, s_t, NEG)
            return s_t

        s_all = [scores(h) for h in range(hp)]
        stats = [_softmax_stats_t(s_all[h], carry[h][0]) for h in range(hp)]
        out = []
        for h in range(hp):
            m_new, alpha, p = stats[h]
            v_t = _with_ones_rows(vt_ref[j, h * MLA_V:(h + 1) * MLA_V, :])
            out.append((m_new, alpha * carry[h][1] + _dot(v_t, p)))
        return tuple(out)

    init = tuple((jnp.full((1, tq), M_INIT, F32), jnp.zeros((MLA_V + 2 * SUBLANES, tq), F32))
                 for _ in range(hp))
    carry = lax.fori_loop(0, qi, lambda j, c: step(j, c, False), init)
    carry = step(qi, carry, True)
    for h in range(hp):
        o_ref[:, h * MLA_V:(h + 1) * MLA_V] = _normalize_t(carry[h][1], MLA_V).T.astype(BF16)


def _mla_mixer(x, mod, pos_col, w_in, q_norm, w_uq, kv_norm, w_ukv):
    bsz, seq, d = x.shape
    hh, dn, dr, dv = MLA_HEADS, MLA_NOPE, MLA_ROPE, MLA_V
    half = dr // 2
    r0 = MLA_Q_RANK + MLA_KV_RANK
    zpad = lambda rows, n: jnp.zeros((rows, n), F32)
    x1, x2 = w_in[:, r0:r0 + half], w_in[:, r0 + half:r0 + dr]
    w_in_ext = jnp.concatenate([w_in[:, :r0], x1, x2, zpad(d, LANES - dr),
                                -x2, x1, zpad(d, LANES - dr)], axis=1).astype(BF16)
    wq = w_uq.reshape(MLA_Q_RANK, hh, dn + dr)
    q1, q2 = wq[:, :, dn:dn + half], wq[:, :, dn + half:]
    zq = jnp.zeros((MLA_Q_RANK, hh, LANES - dr), F32)
    w_uq_main = jnp.concatenate([wq[:, :, :dn], q1, q2, zq], axis=2).reshape(MLA_Q_RANK, hh * 2 * LANES)
    w_uq_rot = jnp.concatenate([-q2, q1, zq], axis=2).reshape(MLA_Q_RANK, hh * LANES)
    wkv = w_ukv.reshape(MLA_KV_RANK, hh, dn + dv)
    w_uk = wkv[:, :, :dn].reshape(MLA_KV_RANK, hh * dn)
    w_uv_t = wkv[:, :, dn:].reshape(MLA_KV_RANK, hh * dv).T
    inv_freq = ROPE_BASE ** (-jnp.arange(half, dtype=F32) / half)
    invf = jnp.concatenate([inv_freq, inv_freq, jnp.zeros((LANES - dr,), F32)]).reshape(1, LANES)

    tm = TOKEN_TILE
    const = lambda b, i: (0, 0)
    tok = lambda n: pl.BlockSpec((None, tm, n), lambda b, i: (b, i, 0))
    q, kn, kr, v_t = pl.pallas_call(
        functools.partial(_mla_proj_kernel, scale=float((dn + dr) ** -0.5) * LOG2E),
        grid=(bsz, seq // tm),
        in_specs=[tok(d),
                  pl.BlockSpec((None, 6, d), lambda b, i: (b, 0, 0)),
                  tok(1),
                  pl.BlockSpec((1, LANES), const),
                  pl.BlockSpec(w_in_ext.shape, const),
                  pl.BlockSpec((1, MLA_Q_RANK), const),
                  pl.BlockSpec((1, MLA_KV_RANK), const),
                  pl.BlockSpec(w_uq_main.shape, const),
                  pl.BlockSpec(w_uq_rot.shape, const),
                  pl.BlockSpec(w_uk.shape, const),
                  pl.BlockSpec(w_uv_t.shape, const)],
        out_specs=[tok(hh * 2 * LANES), tok(hh * dn), tok(LANES),
                   pl.BlockSpec((None, tm // ATT_TK, hh * dv, ATT_TK), lambda b, i: (b, i, 0, 0))],
        out_shape=[jax.ShapeDtypeStruct((bsz, seq, hh * 2 * LANES), BF16),
                   jax.ShapeDtypeStruct((bsz, seq, hh * dn), BF16),
                   jax.ShapeDtypeStruct((bsz, seq, LANES), BF16),
                   jax.ShapeDtypeStruct((bsz, seq // ATT_TK, hh * dv, ATT_TK), BF16)],
        compiler_params=_cparams(2),
        name="mla_project",
    )(x, mod, pos_col, invf, w_in_ext, q_norm.reshape(1, -1), kv_norm.reshape(1, -1),
      w_uq_main.astype(BF16), w_uq_rot.astype(BF16), w_uk.astype(BF16), w_uv_t.astype(BF16))

    tq, hp = ATT_TQ, HEADS_PER_STEP
    return pl.pallas_call(
        _mla_attn_kernel,
        grid=(bsz, hh // hp, seq // tq),
        in_specs=[pl.BlockSpec((None, tq, hp * 2 * LANES), lambda b, h, i: (b, i, h)),
                  pl.BlockSpec((None, seq, hp * dn), lambda b, h, i: (b, 0, h)),
                  pl.BlockSpec((None, seq, LANES), lambda b, h, i: (b, 0, 0)),
                  pl.BlockSpec((None, seq // ATT_TK, hp * dv, ATT_TK), lambda b, h, i: (b, 0, h, 0))],
        out_specs=pl.BlockSpec((None, tq, hp * dv), lambda b, h, i: (b, i, h)),
        out_shape=jax.ShapeDtypeStruct((bsz, seq, hh * dv), BF16),
        compiler_params=_cparams(3),
        name="mla_attention",
    )(q, kn, kr, v_t)


def _moba_kernel(slope_ref, q_ref, k_ref, vt_ref, pk_ref, pq_ref, o_ref, kmean_ref, *, n_blocks):
    blk, hp, hd = MOBA_BLOCK, HEADS_PER_STEP, MOBA_HD
    hg = pl.program_id(1)
    qi = pl.program_id(2)

    @pl.when(qi == 0)
    def _():
        kmean_ref[...] = jnp.zeros_like(kmean_ref)
        for h in range(hp):
            for n in range(n_blocks):
                kmean_ref[h, n:n + 1, :] = jnp.mean(
                    k_ref[n * blk:(n + 1) * blk, h * hd:(h + 1) * hd].astype(F32), axis=0, keepdims=True)

    lane = lax.broadcasted_iota(jnp.int32, (blk, LANES), 1)
    nb_pad = -(-n_blocks // SUBLANES) * SUBLANES
    blk_row = lax.broadcasted_iota(jnp.int32, (nb_pad, blk), 0)
    past = blk_row < qi
    q_aug = []
    for h in range(hp):
        q = q_ref[:, h * hd:(h + 1) * hd]
        km = kmean_ref[h]
        km_hi = km.astype(BF16)
        km_lo = (km - km_hi.astype(F32)).astype(BF16)
        gate_t = (_dot_nt(km_hi, q) + _dot_nt(km_lo, q))[:nb_pad, :]
        g = jnp.where(past, gate_t, LOW)
        rank = jnp.zeros(g.shape, jnp.int32)
        for dlt in range(1, nb_pad):
            other = pltpu.roll(g, dlt, axis=0)
            rank = rank + jnp.where(blk_row >= dlt, (other >= g).astype(jnp.int32),
                                    (other > g).astype(jnp.int32))
        chosen_t = jnp.logical_or(jnp.logical_and(rank < MOBA_TOPK, past), blk_row == qi)
        bias_t = jnp.concatenate([jnp.where(chosen_t, 0.0, NEG), jnp.full((LANES - nb_pad, blk), NEG, F32)],
                                 axis=0)
        q_aug.append(jnp.concatenate([q, bias_t.T.astype(BF16)], axis=1))

    pq = pq_ref[...]

    def step(n, carry, masked):
        ks = pl.multiple_of(n * blk, blk)
        onehot = jnp.where(lane == n, 1.0, 0.0).astype(BF16)
        dist_t = jnp.abs(pk_ref[pl.ds(ks, blk), :] - pq)
        s_all = []
        for h in range(hp):
            k_aug = jnp.concatenate([k_ref[pl.ds(ks, blk), h * hd:(h + 1) * hd], onehot], axis=1)
            s_t = _dot_nt(k_aug, q_aug[h]) - slope_ref[hg * hp + h] * dist_t
            if masked:
                key = lax.broadcasted_iota(jnp.int32, s_t.shape, 0)
                qry = lax.broadcasted_iota(jnp.int32, s_t.shape, 1)
                s_t = jnp.where(key <= qry, s_t, NEG)
            s_all.append(s_t)
        stats = [_softmax_stats_t(s_all[h], carry[h][0]) for h in range(hp)]
        out = []
        for h in range(hp):
            m_new, alpha, p = stats[h]
            v_t = _with_ones_rows(vt_ref[n, h * hd:(h + 1) * hd, :])
            out.append((m_new, alpha * carry[h][1] + _dot(v_t, p)))
        return tuple(out)

    init = tuple((jnp.full((1, blk), M_INIT, F32), jnp.zeros((hd + 2 * SUBLANES, blk), F32))
                 for _ in range(hp))
    carry = lax.fori_loop(0, qi, lambda n, c: step(n, c, False), init)
    carry = step(qi, carry, True)
    for h in range(hp):
        o_ref[:, h * hd:(h + 1) * hd] = _normalize_t(carry[h][1], hd).T.astype(BF16)


def _alibi_slopes(n):
    return 2.0 ** (-8.0 * jnp.arange(1, n + 1, dtype=F32) / n)


def _moba_mixer(x, mod, pos_col, pos_f, w_in):
    bsz, seq, d = x.shape
    hh, hd, blk = MOBA_HEADS, MOBA_HD, MOBA_BLOCK
    assert seq % blk == 0 and seq // blk + MOBA_TOPK < LANES and blk == ATT_TK
    n_blocks = seq // blk
    hp = HEADS_PER_STEP
    nqk = 2 * hh * hd
    col_scale = jnp.concatenate([jnp.full((hh * hd,), hd ** -0.5 * LOG2E, F32),
                                 jnp.ones((hh * hd,), F32)]).reshape(1, -1)
    w_bf = w_in.astype(BF16)
    qk, v_t = _project(x, mod, w_bf[:, :nqk], col_scale, nqk, w_bf[:, nqk:].T)
    pos_row = pos_f.reshape(bsz, n_blocks, 1, blk)
    return pl.pallas_call(
        functools.partial(_moba_kernel, n_blocks=n_blocks),
        grid=(bsz, hh // hp, n_blocks),
        in_specs=[pl.BlockSpec(memory_space=pltpu.SMEM),
                  pl.BlockSpec((None, blk, hp * hd), lambda b, h, i: (b, i, h)),
                  pl.BlockSpec((None, seq, hp * hd), lambda b, h, i: (b, 0, hh // hp + h)),
                  pl.BlockSpec((None, n_blocks, hp * hd, blk), lambda b, h, i: (b, 0, h, 0)),
                  pl.BlockSpec((None, seq, 1), lambda b, h, i: (b, 0, 0)),
                  pl.BlockSpec((None, None, 1, blk), lambda b, h, i: (b, i, 0, 0))],
        out_specs=pl.BlockSpec((None, blk, hp * hd), lambda b, h, i: (b, i, h)),
        out_shape=jax.ShapeDtypeStruct((bsz, seq, hh * hd), BF16),
        scratch_shapes=[pltpu.VMEM((hp, LANES, hd), F32)],
        compiler_params=_cparams(3),
        name="moba_attention",
    )(_alibi_slopes(hh) * LOG2E, qk, qk, v_t, pos_col, pos_row)


def _sb_kernel(q_ref, k_ref, vt_ref, o_ref):
    qi = pl.program_id(2)
    tq, tk, hp, hd = q_ref.shape[0], ATT_TK, HEADS_PER_STEP, SB_HD
    rs = lax.broadcasted_iota(jnp.int32, (tk, tk), 0)
    cj = lax.broadcasted_iota(jnp.int32, (tk, tk), 1)
    after = jnp.where(cj > rs, 1.0, 0.0).astype(BF16)
    after2 = jnp.concatenate([after, after], axis=1)

    def step(j, carry, masked):
        ks = pl.multiple_of(j * tk, tk)
        z_all = [_dot_nt(k_ref[pl.ds(ks, tk), h * hd:(h + 1) * hd], q_ref[:, h * hd:(h + 1) * hd])
                 for h in range(hp)]
        if masked:
            key = lax.broadcasted_iota(jnp.int32, (tk, tq), 0)
            qry = lax.broadcasted_iota(jnp.int32, (tk, tq), 1)
            mask = key < qry
        parts = []
        for h in range(hp):
            z2 = z_all[h]
            log_beta = jnp.minimum(z2, 0.0) - jnp.log2(1.0 + jnp.exp2(-jnp.abs(z2)))
            log_1mb = log_beta - z2
            if masked:
                log_1mb = jnp.where(mask, log_1mb, 0.0)
            hi = log_1mb.astype(BF16)
            lo = (log_1mb - hi.astype(F32)).astype(BF16)
            parts.append((log_beta, log_1mb, jnp.concatenate([hi, lo], axis=0)))
        tails = [_dot(after2, parts[h][2]) for h in range(hp)]
        out = []
        for h in range(hp):
            c, acc_t = carry[h]
            log_beta, log_1mb = parts[h][0], parts[h][1]
            a = jnp.exp2(log_beta + tails[h] + c)
            if masked:
                a = jnp.where(mask, a, 0.0)
            acc_t = acc_t + _dot(vt_ref[j, h * hd:(h + 1) * hd, :], a.astype(BF16))
            out.append((c + jnp.sum(log_1mb, axis=0, keepdims=True), acc_t))
        return tuple(out)

    init = tuple((jnp.zeros((1, tq), F32), jnp.zeros((hd, tq), F32)) for _ in range(hp))
    carry = step(qi, init, True)

    def live(state):
        t, cr = state
        c_max = cr[0][0]
        for h in range(1, hp):
            c_max = jnp.maximum(c_max, cr[h][0])
        return jnp.logical_and(t < qi, jnp.max(c_max) > SB_DEAD_LOG2)

    _, carry = lax.while_loop(live, lambda st: (st[0] + 1, step(qi - 1 - st[0], st[1], False)),
                              (jnp.int32(0), carry))
    for h in range(hp):
        o_ref[:, h * hd:(h + 1) * hd] = carry[h][1].T.astype(BF16)


def _sb_mixer(x, mod, w_in):
    bsz, seq, d = x.shape
    hh, hd, hp = SB_HEADS, SB_HD, HEADS_PER_STEP
    nqk = 2 * hh * hd
    col_scale = jnp.concatenate([jnp.full((hh * hd,), hd ** -0.5 * LOG2E, F32),
                                 jnp.ones((hh * hd,), F32)]).reshape(1, -1)
    w_bf = w_in.astype(BF16)
    qk, v_t = _project(x, mod, w_bf[:, :nqk], col_scale, nqk, w_bf[:, nqk:].T)
    tq = ATT_TQ
    assert tq == ATT_TK
    return pl.pallas_call(
        _sb_kernel,
        grid=(bsz, hh // hp, seq // tq),
        in_specs=[pl.BlockSpec((None, tq, hp * hd), lambda b, h, i: (b, i, h)),
                  pl.BlockSpec((None, seq, hp * hd), lambda b, h, i: (b, 0, hh // hp + h)),
                  pl.BlockSpec((None, seq // ATT_TK, hp * hd, ATT_TK), lambda b, h, i: (b, 0, h, 0))],
        out_specs=pl.BlockSpec((None, tq, hp * hd), lambda b, h, i: (b, i, h)),
        out_shape=jax.ShapeDtypeStruct((bsz, seq, hh * hd), BF16),
        compiler_params=_cparams(3),
        name="stick_breaking_attention",
    )(qk, qk, v_t)


def _nsa_compress_kernel(t_ref, pe_ref, w1_ref, w2_ref, o_ref, ot_ref, *, n_cmp):
    st, half = NSA_CMP_STRIDE, NSA_CMP_LEN // NSA_CMP_STRIDE
    assert half == 2
    rows = t_ref.shape[0] // st
    xa, xb = [], []
    for r in range(st):
        xr = t_ref[pl.ds(r, rows, stride=st), :]
        xa.append((xr + pe_ref[r:r + 1, :]).astype(BF16))
        xb.append((xr + pe_ref[st + r:st + r + 1, :]).astype(BF16))
    k_half = st * NSA_HD
    pre_a = _dot(jnp.concatenate(xa, axis=1), w1_ref[:k_half, :])
    pre_b = _dot(jnp.concatenate(xb, axis=1), w1_ref[k_half:, :])
    pre = pre_a + pltpu.roll(pre_b, rows - 1, axis=0)
    out = _dot(_gelu_tanh(pre).astype(BF16), w2_ref[...])
    row = lax.broadcasted_iota(jnp.int32, out.shape, 0)
    out = jnp.where(row < n_cmp, out, 0.0)
    o_ref[...] = out.astype(BF16)
    ot_ref[...] = out.T.astype(BF16)


def _nsa_cmp_select_kernel(slope_ref, q_ref, kc_ref, vct_ref, gt_ref, pq_ref, pe_ref, ovlt_ref,
                           acc_ref, sel_ref, *, n_cmp, n_sel):
    qi = pl.program_id(1)
    tq, gg, rr, hd = q_ref.shape[0], NSA_GROUPS, NSA_R, NSA_HD
    shape = (LANES, tq)
    n_row = lax.broadcasted_iota(jnp.int32, shape, 0)
    qidx = qi * tq + lax.broadcasted_iota(jnp.int32, shape, 1)
    c_end = n_row * NSA_CMP_STRIDE + (NSA_CMP_LEN - 1)
    mask_c = jnp.logical_and(c_end <= qidx, n_row < n_cmp)
    dist_t = jnp.abs(pe_ref[...] - pq_ref[...])
    sig_t = jax.nn.sigmoid(gt_ref[...]).T
    ovl_t = ovlt_ref[...]

    heads = [(g, r) for g in range(gg) for r in range(rr)]
    s_all = [_dot_nt(kc_ref[g], q_ref[:, (g * rr + r) * hd:(g * rr + r + 1) * hd])
             - slope_ref[g * rr + r] * dist_t for g, r in heads]
    p_all = []
    for s_t in s_all:
        s_t = jnp.where(mask_c, s_t, NEG)
        p = jnp.where(mask_c, jnp.exp2(s_t - jnp.max(s_t, axis=0, keepdims=True)), 0.0)
        p_all.append((p / jnp.maximum(jnp.sum(p, axis=0, keepdims=True), TINY)).astype(BF16))
    imp_t = [jnp.zeros((n_sel, tq), F32) for _ in range(gg)]
    for (g, r), p in zip(heads, p_all):
        imp_t[g] = imp_t[g] + _dot(ovl_t, p)
        c = g * rr + r
        acc_ref[:, c * hd:(c + 1) * hd] = (sig_t[c:c + 1, :] * _dot(vct_ref[gg + g], p)).T

    sshape = (n_sel, tq)
    j_row = lax.broadcasted_iota(jnp.int32, sshape, 0)
    q_blk = (qi * tq + lax.broadcasted_iota(jnp.int32, sshape, 1)) >> 6
    assert NSA_SEL_BLOCK == 64
    forced = jnp.logical_or(j_row == 0, jnp.logical_or(j_row == q_blk, j_row == q_blk - 1))
    cand = j_row <= q_blk
    bonus = jnp.where(forced, NSA_FORCE_BONUS, 0.0)
    for g in range(gg):
        score = jnp.where(cand, imp_t[g] + bonus, NEG)
        rank = jnp.zeros(sshape, jnp.int32)
        for dlt in range(1, n_sel):
            other = pltpu.roll(score, dlt, axis=0)
            rank = rank + jnp.where(j_row >= dlt, (other >= score).astype(jnp.int32),
                                    (other > score).astype(jnp.int32))
        chosen_t = jnp.logical_and(rank < NSA_SEL_TOPN, cand)
        bias_t = jnp.concatenate([jnp.where(chosen_t, 0.0, NEG), jnp.full((LANES - n_sel, tq), NEG, F32)],
                                 axis=0)
        sel_ref[g] = bias_t.T.astype(BF16)


def _nsa_select_window_kernel(slope_ref, q_ref, sel_ref, k_ref, vt_ref, gt_ref, pk_ref, pq_ref, prev_ref,
                              out_ref):
    qi = pl.program_id(1)
    tq, tk, hd, gg, rr = q_ref.shape[0], ATT_TK, NSA_HD, NSA_GROUPS, NSA_R
    assert tq == tk and NSA_WINDOW % tk == 0
    lanes = rr * tq
    pq4 = jnp.concatenate([pq_ref[...]] * rr, axis=1)
    qidx = qi * tq + (lax.broadcasted_iota(jnp.int32, (tk, lanes), 1) & (tq - 1))
    krow = lax.broadcasted_iota(jnp.int32, (tk, lanes), 0)
    klane = lax.broadcasted_iota(jnp.int32, (tk, LANES), 1)
    kblk = lax.broadcasted_iota(jnp.int32, (tk, LANES), 0) >> 6
    per_tile = tk // NSA_SEL_BLOCK
    slope_rows = [jnp.concatenate([jnp.full((1, tq), slope_ref[g * rr + r], F32) for r in range(rr)], axis=1)
                  for g in range(gg)]
    q4 = [jnp.concatenate([q_ref[:, (g * rr + r) * hd:(g * rr + r + 1) * hd] for r in range(rr)], axis=0)
          for g in range(gg)]
    q_aug = [jnp.concatenate([q4[g], jnp.concatenate([sel_ref[g]] * rr, axis=0)], axis=1) for g in range(gg)]
    k_win0 = gg * hd

    def update(s_all, carry, v_row0, j):
        stats = [_softmax_stats_t(s_all[g], carry[g][0]) for g in range(gg)]
        out = []
        for g in range(gg):
            m_new, alpha, p = stats[g]
            v_t = _with_ones_rows(vt_ref[j, v_row0 + g * hd:v_row0 + (g + 1) * hd, :])
            out.append((m_new, alpha * carry[g][1] + _dot(v_t, p)))
        return tuple(out)

    def select_step(j, carry, masked):
        ks = pl.multiple_of(j * tk, tk)
        dist = jnp.abs(pk_ref[pl.ds(ks, tk), :] - pq4)
        onehot = jnp.where(klane == j * per_tile + kblk, 1.0, 0.0).astype(BF16)
        s_all = []
        for g in range(gg):
            k_aug = jnp.concatenate([k_ref[pl.ds(ks, tk), g * hd:(g + 1) * hd], onehot], axis=1)
            s_t = _dot_nt(k_aug, q_aug[g]) - slope_rows[g] * dist
            if masked:
                s_t = jnp.where(j * tk + krow <= qidx, s_t, NEG)
            s_all.append(s_t)
        return update(s_all, carry, 0, j)

    def init():
        return tuple((jnp.full((1, lanes), M_INIT, F32), jnp.zeros((hd + 2 * SUBLANES, lanes), F32))
                     for _ in range(gg))

    sel = lax.fori_loop(0, qi, lambda j, c: select_step(j, c, False), init())
    sel = select_step(qi, sel, True)

    win = init()
    n_band = NSA_WINDOW // tk + 1
    qloc = qidx - qi * tq
    for t in range(n_band):
        jt = qi - (n_band - 1) + t
        jc = jnp.maximum(jt, 0)
        ks = pl.multiple_of(jc * tk, tk)
        none = jnp.where(jt >= 0, 0, tk)
        if t == n_band - 1:
            mask = krow <= qloc
        elif t == 0:
            mask = krow > qloc + none
        else:
            mask = krow >= none
        dist = jnp.abs(pk_ref[pl.ds(ks, tk), :] - pq4)
        s_all = []
        for g in range(gg):
            s_t = _dot_nt(k_ref[pl.ds(ks, tk), k_win0 + g * hd:k_win0 + (g + 1) * hd], q4[g])
            s_all.append(jnp.where(mask, s_t - slope_rows[g] * dist, NEG))
        win = update(s_all, win, gg * hd, jc)

    sig_t = jax.nn.sigmoid(gt_ref[...]).T
    for g in range(gg):
        gate = lambda branch: jnp.concatenate(
            [sig_t[branch * NSA_HEADS + g * rr + r:branch * NSA_HEADS + g * rr + r + 1, :] for r in range(rr)],
            axis=1)
        o_t = gate(1) * _normalize_t(sel[g][1], hd) + gate(2) * _normalize_t(win[g][1], hd)
        for r in range(rr):
            c = g * rr + r
            out_ref[:, c * hd:(c + 1) * hd] = (prev_ref[:, c * hd:(c + 1) * hd]
                                                + o_t[:, r * tq:(r + 1) * tq].T).astype(BF16)


def _nsa_mixer(x, mod, pos_col, pos_f, w_in, cmp_pos, cmp_w1, cmp_w2):
    bsz, seq, d = x.shape
    hh, gg, hd = NSA_HEADS, NSA_GROUPS, NSA_HD
    n_cmp = (seq - NSA_CMP_LEN) // NSA_CMP_STRIDE + 1
    n_sel = seq // NSA_SEL_BLOCK
    tq = ATT_TQ
    assert seq % tq == 0 and seq // NSA_CMP_STRIDE == LANES and n_sel % SUBLANES == 0 and n_sel <= LANES
    assert NSA_SEL_TOPN <= n_sel
    nq, nkv = hh * hd, gg * hd
    kv = lambda t: w_in[:, nq + t * nkv:nq + (t + 1) * nkv]
    n_gate = 3 * hh
    w_perm = jnp.concatenate([w_in[:, :nq], kv(2), kv(4), kv(0), kv(1),
                              w_in[:, nq + 6 * nkv:], jnp.zeros((d, LANES - n_gate), F32)], axis=1)
    n_main = nq + 2 * nkv
    col_scale = jnp.concatenate([jnp.full((nq,), hd ** -0.5 * LOG2E, F32),
                                 jnp.ones((w_perm.shape[1] - nq,), F32)]).reshape(1, -1)
    w_vt = jnp.concatenate([kv(3), kv(5)], axis=1).T.astype(BF16)
    main, v_t, tail = _project(x, mod, w_perm.astype(BF16), col_scale, n_main, w_vt)
    slopes = _alibi_slopes(hh) * LOG2E
    gate_blk = 2 * nkv // LANES

    cmp_spec = pl.BlockSpec((None, None, LANES, hd), lambda b, w, g: (b, w * gg + g, 0, 0))
    cmp_kv, cmp_kv_t = pl.pallas_call(
        functools.partial(_nsa_compress_kernel, n_cmp=n_cmp),
        grid=(bsz, 2, gg),
        in_specs=[pl.BlockSpec((None, seq, hd), lambda b, w, g: (b, 0, w * gg + g)),
                  pl.BlockSpec((None, NSA_CMP_LEN, hd), lambda b, w, g: (w, 0, 0)),
                  pl.BlockSpec((None, NSA_CMP_LEN * hd, hd), lambda b, w, g: (w, 0, 0)),
                  pl.BlockSpec((None, hd, hd), lambda b, w, g: (w, 0, 0))],
        out_specs=[cmp_spec, cmp_spec],
        out_shape=[jax.ShapeDtypeStruct((bsz, 2 * gg, LANES, hd), BF16)] * 2,
        compiler_params=_cparams(3),
        name="nsa_compress",
    )(tail, cmp_pos, cmp_w1.astype(BF16), cmp_w2.astype(BF16))

    pos_cend = jnp.pad(pos_f[:, NSA_CMP_LEN - 1::NSA_CMP_STRIDE], ((0, 0), (0, LANES - n_cmp)))
    pos_cend = pos_cend.reshape(bsz, LANES, 1)
    c_start = jnp.arange(LANES) * NSA_CMP_STRIDE
    j_start = jnp.arange(n_sel) * NSA_SEL_BLOCK
    overlap_t = ((c_start[None, :] < j_start[:, None] + NSA_SEL_BLOCK)
                 & (c_start[None, :] + NSA_CMP_LEN - 1 >= j_start[:, None])
                 & (jnp.arange(LANES)[None, :] < n_cmp)).astype(BF16)

    smem = pl.BlockSpec(memory_space=pltpu.SMEM)
    q_spec = pl.BlockSpec((None, tq, nq), lambda b, i: (b, i, 0))
    gate_spec = pl.BlockSpec((None, tq, LANES), lambda b, i: (b, i, gate_blk))
    pq_spec = pl.BlockSpec((None, None, 1, tq), lambda b, i: (b, i, 0, 0))
    sel_spec = pl.BlockSpec((None, gg, tq, LANES), lambda b, i: (b, 0, i, 0))
    cmp_all = pl.BlockSpec((None, 2 * gg, LANES, hd), lambda b, i: (b, 0, 0, 0))
    pos_q = pos_f.reshape(bsz, seq // tq, 1, tq)
    grid = (bsz, seq // tq)

    acc0, sel_bias = pl.pallas_call(
        functools.partial(_nsa_cmp_select_kernel, n_cmp=n_cmp, n_sel=n_sel),
        grid=grid,
        in_specs=[smem, q_spec, cmp_all, cmp_all, gate_spec, pq_spec,
                  pl.BlockSpec((None, LANES, 1), lambda b, i: (b, 0, 0)),
                  pl.BlockSpec((n_sel, LANES), lambda b, i: (0, 0))],
        out_specs=[q_spec, sel_spec],
        out_shape=[jax.ShapeDtypeStruct((bsz, seq, nq), F32),
                   jax.ShapeDtypeStruct((bsz, gg, seq, LANES), BF16)],
        compiler_params=_cparams(2),
        name="nsa_compressed_attention_select",
    )(slopes, main, cmp_kv, cmp_kv_t, tail, pos_q, pos_cend, overlap_t)

    return pl.pallas_call(
        _nsa_select_window_kernel,
        grid=grid,
        in_specs=[smem, q_spec, sel_spec,
                  pl.BlockSpec((None, seq, 2 * nkv), lambda b, i: (b, 0, nq // (2 * nkv))),
                  pl.BlockSpec((None, seq // ATT_TK, 2 * nkv, ATT_TK), lambda b, i: (b, 0, 0, 0)),
                  gate_spec,
                  pl.BlockSpec((None, seq, 1), lambda b, i: (b, 0, 0)),
                  pq_spec, q_spec],
        out_specs=q_spec,
        out_shape=jax.ShapeDtypeStruct((bsz, seq, nq), BF16),
        compiler_params=_cparams(2),
        name="nsa_selected_window_attention",
    )(slopes, main, sel_bias, main, v_t, tail, pos_col, pos_q, acc0)


def kernel(x, c, positions, mod_w, mod_b, ln_g, ln_b, ffn_w_in, ffn_conv_w, ffn_conv_b, ffn_w_out,
           mla_w_in, mla_q_norm, mla_w_uq, mla_kv_norm, mla_w_ukv, mla_w_o,
           moba_w_in, moba_w_o, nsa_w_in, nsa_cmp_pos, nsa_cmp_w1, nsa_cmp_w2, nsa_w_o,
           sb_w_in, sb_w_o):
    bsz, seq, d = x.shape
    depth = mod_w.shape[0]
    assert d == D_MODEL and seq % TOKEN_TILE == 0 and seq % ATT_TQ == 0
    alpha = float((2 * depth) ** 0.25)
    pos_f = positions.astype(F32)
    pos_col = pos_f.reshape(bsz, seq, 1)
    mod_all = _modulation(c, mod_w, mod_b)
    for i in range(depth):
        kind, j = i % N_MIXERS, i // N_MIXERS
        mod = mod_all[i]
        if kind == 0:
            o = _mla_mixer(x, mod, pos_col, mla_w_in[j], mla_q_norm[j], mla_w_uq[j], mla_kv_norm[j],
                           mla_w_ukv[j])
            w_o = mla_w_o[j]
        elif kind == 1:
            o = _moba_mixer(x, mod, pos_col, pos_f, moba_w_in[j])
            w_o = moba_w_o[j]
        elif kind == 2:
            o = _nsa_mixer(x, mod, pos_col, pos_f, nsa_w_in[j], nsa_cmp_pos[j], nsa_cmp_w1[j],
                           nsa_cmp_w2[j])
            w_o = nsa_w_o[j]
        else:
            o = _sb_mixer(x, mod, sb_w_in[j])
            w_o = sb_w_o[j]
        x = _mixer_out_ffn(o, w_o.astype(BF16), x, mod, ln_g[i], ln_b[i], ffn_w_in[i].astype(BF16),
                           ffn_conv_w[i], ffn_conv_b[i], ffn_w_out[i].astype(BF16), alpha)
    return x
```

```python
import functools

import jax
import jax.numpy as jnp
from jax import lax
from jax.experimental import pallas as pl
from jax.experimental.pallas import tpu as pltpu

F32 = jnp.float32
BF16 = jnp.bfloat16

D_MODEL = 1024
N_MIXERS = 4
MLA_HEADS, MLA_NOPE, MLA_ROPE, MLA_V = 8, 128, 64, 128
MLA_Q_RANK, MLA_KV_RANK = 256, 256
ROPE_BASE = 10000.0
MOBA_HEADS, MOBA_HD, MOBA_BLOCK, MOBA_TOPK = 8, 128, 256, 3
NSA_HEADS, NSA_GROUPS, NSA_HD = 8, 2, 128
NSA_R = NSA_HEADS // NSA_GROUPS
NSA_CMP_LEN, NSA_CMP_STRIDE, NSA_SEL_BLOCK, NSA_SEL_TOPN, NSA_WINDOW = 32, 16, 64, 16, 512
NSA_FORCE_BONUS = 100.0
SB_HEADS, SB_HD = 8, 128
D_FF = 2816
LN_EPS = 1e-5
RMS_EPS = 1e-6
NEG = -1e30
M_INIT = -1e29
LOW = -3e38
TINY = 1e-30
LOG2E = 1.4426950408889634
SB_DEAD_LOG2 = -150.0

LANES = 128
SUBLANES = 8
VMEM_LIMIT_BYTES = 56 * 1024 * 1024

TOKEN_TILE = 512
ATT_TQ = 256
ATT_TK = 256
HEADS_PER_STEP = 8
FF_CHUNK = 256
PROJ_CHUNK = 512


def _cparams(n_grid):
    return pltpu.CompilerParams(dimension_semantics=("arbitrary",) * n_grid,
                                vmem_limit_bytes=VMEM_LIMIT_BYTES)


def _dot(a, b):
    return jnp.dot(a, b, preferred_element_type=F32)


def _dot_nt(a, b):
    return lax.dot_general(a, b, (((1,), (1,)), ((), ())), preferred_element_type=F32)


def _layer_norm(z, g, b):
    mu = jnp.mean(z, axis=-1, keepdims=True)
    d = z - mu
    var = jnp.mean(d * d, axis=-1, keepdims=True)
    return d * lax.rsqrt(var + LN_EPS) * g + b


def _rms_norm(z, g):
    return z * lax.rsqrt(jnp.mean(z * z, axis=-1, keepdims=True) + RMS_EPS) * g


def _gelu_tanh(x):
    return 0.5 * x * (1.0 + jnp.tanh(0.7978845608028654 * (x + 0.044715 * (x * x * x))))


def _softmax_stats_t(s_t, m):
    m_new = jnp.maximum(m, jnp.max(s_t, axis=0, keepdims=True))
    return m_new, jnp.exp2(m - m_new), jnp.exp2((s_t - m_new).astype(BF16))


def _with_ones_rows(v_t):
    return jnp.concatenate([v_t, jnp.ones((2 * SUBLANES, v_t.shape[1]), v_t.dtype)], axis=0)


def _normalize_t(acc_t, hd):
    return acc_t[:hd, :] / jnp.maximum(acc_t[hd:hd + 1, :], TINY)


def _mod_kernel(c_ref, w_ref, b_ref, o_ref):
    c = c_ref[...]
    c_act = c * jax.nn.sigmoid(c)
    o_ref[...] = jnp.dot(c_act, w_ref[...], preferred_element_type=F32,
                         precision=lax.Precision.HIGHEST) + b_ref[...]


def _modulation(c, mod_w, mod_b):
    depth, d, n = mod_w.shape
    bsz = c.shape[0]
    tn = 1536
    out = pl.pallas_call(
        _mod_kernel,
        grid=(depth, n // tn),
        in_specs=[pl.BlockSpec((bsz, d), lambda l, j: (0, 0)),
                  pl.BlockSpec((None, d, tn), lambda l, j: (l, 0, j)),
                  pl.BlockSpec((None, 1, tn), lambda l, j: (l, 0, j))],
        out_specs=pl.BlockSpec((None, bsz, tn), lambda l, j: (l, 0, j)),
        out_shape=jax.ShapeDtypeStruct((depth, bsz, n), F32),
        compiler_params=_cparams(2),
        name="modulation",
    )(c, mod_w, mod_b.reshape(depth, 1, n))
    return out.reshape(depth, bsz, 6, d)


def _proj_kernel(x_ref, mod_ref, w_ref, cs_ref, wt_ref, *out_refs, n_main, n_total):
    h = (x_ref[...] * (1.0 + mod_ref[1:2, :]) + mod_ref[0:1, :]).astype(BF16)
    for n0 in range(0, n_total, PROJ_CHUNK):
        n1 = min(n0 + PROJ_CHUNK, n_total)
        y = _dot(h, w_ref[:, n0:n1]) * cs_ref[:, n0:n1]
        if n0 < n_main:
            out_refs[0][:, n0:n1] = y.astype(BF16)
        else:
            out_refs[2][:, n0 - n_main:n1 - n_main] = y
    vt_ref = out_refs[1]
    for n0 in range(0, wt_ref.shape[0], PROJ_CHUNK):
        y_t = _dot_nt(wt_ref[n0:n0 + PROJ_CHUNK, :], h).astype(BF16)
        for t in range(vt_ref.shape[0]):
            vt_ref[t, n0:n0 + PROJ_CHUNK, :] = y_t[:, t * ATT_TK:(t + 1) * ATT_TK]


def _project(x, mod, w, col_scale, n_main, w_t):
    bsz, seq, d = x.shape
    n_total, n_t = w.shape[1], w_t.shape[0]
    assert n_main % PROJ_CHUNK == 0 and n_total % LANES == 0 and n_t % PROJ_CHUNK == 0
    tm = TOKEN_TILE
    out_shape = [jax.ShapeDtypeStruct((bsz, seq, n_main), BF16),
                 jax.ShapeDtypeStruct((bsz, seq // ATT_TK, n_t, ATT_TK), BF16)]
    out_specs = [pl.BlockSpec((None, tm, n_main), lambda b, i: (b, i, 0)),
                 pl.BlockSpec((None, tm // ATT_TK, n_t, ATT_TK), lambda b, i: (b, i, 0, 0))]
    if n_main < n_total:
        out_shape.append(jax.ShapeDtypeStruct((bsz, seq, n_total - n_main), F32))
        out_specs.append(pl.BlockSpec((None, tm, n_total - n_main), lambda b, i: (b, i, 0)))
    return pl.pallas_call(
        functools.partial(_proj_kernel, n_main=n_main, n_total=n_total),
        grid=(bsz, seq // tm),
        in_specs=[pl.BlockSpec((None, tm, d), lambda b, i: (b, i, 0)),
                  pl.BlockSpec((None, 6, d), lambda b, i: (b, 0, 0)),
                  pl.BlockSpec((d, n_total), lambda b, i: (0, 0)),
                  pl.BlockSpec((1, n_total), lambda b, i: (0, 0)),
                  pl.BlockSpec((n_t, d), lambda b, i: (0, 0))],
        out_specs=out_specs,
        out_shape=out_shape,
        compiler_params=_cparams(2),
        name="modulate_project",
    )(x, mod, w, col_scale, w_t)


def _mixer_out_ffn_kernel(o_ref, wmix_ref, x_ref, mod_ref, g1_ref, b1_ref, win_ref, cw_ref, cb_ref,
                          wo_ref, g2_ref, b2_ref, out_ref, a_scr, g_scr, halo_scr, *, alpha):
    tm = x_ref.shape[0]
    halo = halo_scr.shape[0]
    x1 = _layer_norm(alpha * x_ref[...] + mod_ref[2:3, :] * _dot(o_ref[...], wmix_ref[...]),
                     g1_ref[...], b1_ref[...])
    shift, scale, gate = mod_ref[3:4, :], mod_ref[4:5, :], mod_ref[5:6, :]
    h = (x1 * (1.0 + scale) + shift).astype(BF16)

    @pl.when(pl.program_id(1) == 0)
    def _():
        halo_scr[...] = jnp.zeros_like(halo_scr)

    h_ext = jnp.concatenate([halo_scr[...], h], axis=0)
    halo_scr[...] = h[tm - halo:, :]
    chunks = [(c0, min(c0 + FF_CHUNK, D_FF)) for c0 in range(0, D_FF, FF_CHUNK)]

    def matmuls(i):
        c0, c1 = chunks[i]
        a_scr[i % 2, :, :c1 - c0] = _dot(h_ext, win_ref[:, c0:c1])
        return _dot(h, win_ref[:, D_FF + c0:D_FF + c1])

    b_next = matmuls(0)
    for i, (c0, c1) in enumerate(chunks):
        b = b_next
        if i + 1 < len(chunks):
            b_next = matmuls(i + 1)
        w = c1 - c0
        a = a_scr[i % 2, halo:halo + tm, :w]
        a_m1 = a_scr[i % 2, halo - 1:halo - 1 + tm, :w]
        a_m2 = a_scr[i % 2, halo - 2:halo - 2 + tm, :w]
        conv = (a * cw_ref[2:3, c0:c1] + a_m1 * cw_ref[1:2, c0:c1] + a_m2 * cw_ref[0:1, c0:c1]
                + cb_ref[:, c0:c1])
        g_scr[:, c0:c1] = (_gelu_tanh(conv) * b).astype(BF16)
    y = _dot(g_scr[...], wo_ref[...])
    out_ref[...] = _layer_norm(alpha * x1 + gate * y, g2_ref[...], b2_ref[...])


def _mixer_out_ffn(o, w_mix, x, mod, ln_g, ln_b, w_in_all, conv_w, conv_b, w_out_all, layer, alpha):
    bsz, seq, d = x.shape
    k = o.shape[-1]
    tm = TOKEN_TILE
    halo = 2 * SUBLANES
    const = lambda shape: pl.BlockSpec(shape, lambda b, i: (0, 0), pipeline_mode=pl.Buffered(1))
    of_layer = lambda shape: pl.BlockSpec((None,) + shape, lambda b, i: (layer, 0, 0),
                                          pipeline_mode=pl.Buffered(1))
    return pl.pallas_call(
        functools.partial(_mixer_out_ffn_kernel, alpha=alpha),
        grid=(bsz, seq // tm),
        in_specs=[pl.BlockSpec((None, tm, k), lambda b, i: (b, i, 0)),
                  const((k, d)),
                  pl.BlockSpec((None, tm, d), lambda b, i: (b, i, 0)),
                  pl.BlockSpec((None, 6, d), lambda b, i: (b, 0, 0)),
                  const((1, d)), const((1, d)),
                  of_layer((d, 2 * D_FF)), const((3, D_FF)), const((1, D_FF)),
                  of_layer((D_FF, d)), const((1, d)), const((1, d))],
        out_specs=pl.BlockSpec((None, tm, d), lambda b, i: (b, i, 0)),
        out_shape=jax.ShapeDtypeStruct((bsz, seq, d), F32),
        scratch_shapes=[pltpu.VMEM((2, halo + tm, FF_CHUNK), F32), pltpu.VMEM((tm, D_FF), BF16),
                        pltpu.VMEM((halo, d), BF16)],
        compiler_params=_cparams(2),
        name="mixer_out_conv_ffn",
    )(o, w_mix, x, mod, ln_g[0].reshape(1, d), ln_b[0].reshape(1, d), w_in_all, conv_w,
      conv_b.reshape(1, D_FF), w_out_all, ln_g[1].reshape(1, d), ln_b[1].reshape(1, d))


def _mla_proj_kernel(x_ref, mod_ref, pos_ref, invf_ref, win_ref, qn_ref, kvn_ref, wuq_ref,
                     wuqr_ref, wuk_ref, wuvt_ref, q_ref, kn_ref, kr_ref, vt_ref, *, scale):
    h = (x_ref[...] * (1.0 + mod_ref[1:2, :]) + mod_ref[0:1, :]).astype(BF16)
    proj = _dot(h, win_ref[...])
    c_q = _rms_norm(proj[:, :MLA_Q_RANK], qn_ref[...]).astype(BF16)
    c_kv = _rms_norm(proj[:, MLA_Q_RANK:MLA_Q_RANK + MLA_KV_RANK], kvn_ref[...]).astype(BF16)
    ang = pos_ref[...] * invf_ref[...]
    cos, sin = jnp.cos(ang), jnp.sin(ang)
    r0 = MLA_Q_RANK + MLA_KV_RANK
    kr_ref[...] = (proj[:, r0:r0 + LANES] * cos + proj[:, r0 + LANES:r0 + 2 * LANES] * sin).astype(BF16)
    for hd in range(MLA_HEADS):
        qa = _dot(c_q, wuq_ref[:, hd * 2 * LANES:(hd + 1) * 2 * LANES])
        qr = _dot(c_q, wuqr_ref[:, hd * LANES:(hd + 1) * LANES])
        q_ref[:, hd * 2 * LANES:hd * 2 * LANES + LANES] = (qa[:, :LANES] * scale).astype(BF16)
        q_ref[:, hd * 2 * LANES + LANES:(hd + 1) * 2 * LANES] = (
            (qa[:, LANES:] * cos + qr * sin) * scale).astype(BF16)
    kn_ref[...] = _dot(c_kv, wuk_ref[...]).astype(BF16)
    v_t = _dot_nt(wuvt_ref[...], c_kv).astype(BF16)
    for t in range(vt_ref.shape[0]):
        vt_ref[t] = v_t[:, t * ATT_TK:(t + 1) * ATT_TK]


def _mla_attn_kernel(q_ref, kn_ref, kr_ref, vt_ref, o_ref):
    qi = pl.program_id(2)
    tq = q_ref.shape[0]
    hp = HEADS_PER_STEP

    def step(j, carry, masked):
        ks = pl.multiple_of(j * ATT_TK, ATT_TK)
        kr = kr_ref[pl.ds(ks, ATT_TK), :]
        def scores(h):
            k = jnp.concatenate([kn_ref[pl.ds(ks, ATT_TK), h * MLA_NOPE:(h + 1) * MLA_NOPE], kr], axis=1)
            s_t = _dot_nt(k, q_ref[:, h * 2 * LANES:(h + 1) * 2 * LANES])
            if masked:
                key = lax.broadcasted_iota(jnp.int32, s_t.shape, 0)
                qry = lax.broadcasted_iota(jnp.int32, s_t.shape, 1)
                s_t = jnp.where(key <= qry, s_t, NEG)
            return s_t

        s_all = [scores(h) for h in range(hp)]
        stats = [_softmax_stats_t(s_all[h], carry[h][0]) for h in range(hp)]
        out = []
        for h in range(hp):
            m_new, alpha, p = stats[h]
            v_t = _with_ones_rows(vt_ref[j, h * MLA_V:(h + 1) * MLA_V, :])
            out.append((m_new, alpha * carry[h][1] + _dot(v_t, p)))
        return tuple(out)

    init = tuple((jnp.full((1, tq), M_INIT, F32), jnp.zeros((MLA_V + 2 * SUBLANES, tq), F32))
                 for _ in range(hp))
    carry = lax.fori_loop(0, qi, lambda j, c: step(j, c, False), init)
    carry = step(qi, carry, True)
    for h in range(hp):
        o_ref[:, h * MLA_V:(h + 1) * MLA_V] = _normalize_t(carry[h][1], MLA_V).T.astype(BF16)


def _mla_mixer(x, mod, pos_col, w_in, q_norm, w_uq, kv_norm, w_ukv):
    bsz, seq, d = x.shape
    hh, dn, dr, dv = MLA_HEADS, MLA_NOPE, MLA_ROPE, MLA_V
    half = dr // 2
    r0 = MLA_Q_RANK + MLA_KV_RANK
    w_in, w_uq, w_ukv = w_in.astype(BF16), w_uq.astype(BF16), w_ukv.astype(BF16)
    zpad = lambda rows, n: jnp.zeros((rows, n), BF16)
    x1, x2 = w_in[:, r0:r0 + half], w_in[:, r0 + half:r0 + dr]
    w_in_ext = jnp.concatenate([w_in[:, :r0], x1, x2, zpad(d, LANES - dr),
                                -x2, x1, zpad(d, LANES - dr)], axis=1)
    wq = w_uq.reshape(MLA_Q_RANK, hh, dn + dr)
    q1, q2 = wq[:, :, dn:dn + half], wq[:, :, dn + half:]
    zq = jnp.zeros((MLA_Q_RANK, hh, LANES - dr), BF16)
    w_uq_main = jnp.concatenate([wq[:, :, :dn], q1, q2, zq], axis=2).reshape(MLA_Q_RANK, hh * 2 * LANES)
    w_uq_rot = jnp.concatenate([-q2, q1, zq], axis=2).reshape(MLA_Q_RANK, hh * LANES)
    wkv = w_ukv.reshape(MLA_KV_RANK, hh, dn + dv)
    w_uk = wkv[:, :, :dn].reshape(MLA_KV_RANK, hh * dn)
    w_uv_t = wkv[:, :, dn:].reshape(MLA_KV_RANK, hh * dv).T
    inv_freq = ROPE_BASE ** (-jnp.arange(half, dtype=F32) / half)
    invf = jnp.concatenate([inv_freq, inv_freq, jnp.zeros((LANES - dr,), F32)]).reshape(1, LANES)

    tm = TOKEN_TILE
    const = lambda b, i: (0, 0)
    tok = lambda n: pl.BlockSpec((None, tm, n), lambda b, i: (b, i, 0))
    q, kn, kr, v_t = pl.pallas_call(
        functools.partial(_mla_proj_kernel, scale=float((dn + dr) ** -0.5) * LOG2E),
        grid=(bsz, seq // tm),
        in_specs=[tok(d),
                  pl.BlockSpec((None, 6, d), lambda b, i: (b, 0, 0)),
                  tok(1),
                  pl.BlockSpec((1, LANES), const),
                  pl.BlockSpec(w_in_ext.shape, const),
                  pl.BlockSpec((1, MLA_Q_RANK), const),
                  pl.BlockSpec((1, MLA_KV_RANK), const),
                  pl.BlockSpec(w_uq_main.shape, const),
                  pl.BlockSpec(w_uq_rot.shape, const),
                  pl.BlockSpec(w_uk.shape, const),
                  pl.BlockSpec(w_uv_t.shape, const)],
        out_specs=[tok(hh * 2 * LANES), tok(hh * dn), tok(LANES),
                   pl.BlockSpec((None, tm // ATT_TK, hh * dv, ATT_TK), lambda b, i: (b, i, 0, 0))],
        out_shape=[jax.ShapeDtypeStruct((bsz, seq, hh * 2 * LANES), BF16),
                   jax.ShapeDtypeStruct((bsz, seq, hh * dn), BF16),
                   jax.ShapeDtypeStruct((bsz, seq, LANES), BF16),
                   jax.ShapeDtypeStruct((bsz, seq // ATT_TK, hh * dv, ATT_TK), BF16)],
        compiler_params=_cparams(2),
        name="mla_project",
    )(x, mod, pos_col, invf, w_in_ext, q_norm.reshape(1, -1), kv_norm.reshape(1, -1),
      w_uq_main, w_uq_rot, w_uk, w_uv_t)

    tq, hp = ATT_TQ, HEADS_PER_STEP
    return pl.pallas_call(
        _mla_attn_kernel,
        grid=(bsz, hh // hp, seq // tq),
        in_specs=[pl.BlockSpec((None, tq, hp * 2 * LANES), lambda b, h, i: (b, i, h)),
                  pl.BlockSpec((None, seq, hp * dn), lambda b, h, i: (b, 0, h)),
                  pl.BlockSpec((None, seq, LANES), lambda b, h, i: (b, 0, 0)),
                  pl.BlockSpec((None, seq // ATT_TK, hp * dv, ATT_TK), lambda b, h, i: (b, 0, h, 0))],
        out_specs=pl.BlockSpec((None, tq, hp * dv), lambda b, h, i: (b, i, h)),
        out_shape=jax.ShapeDtypeStruct((bsz, seq, hh * dv), BF16),
        compiler_params=_cparams(3),
        name="mla_attention",
    )(q, kn, kr, v_t)


def _moba_kernel(slope_ref, q_ref, k_ref, vt_ref, pk_ref, pq_ref, o_ref, kmean_ref, *, n_blocks):
    blk, hp, hd = MOBA_BLOCK, HEADS_PER_STEP, MOBA_HD
    hg = pl.program_id(1)
    qi = pl.program_id(2)

    @pl.when(qi == 0)
    def _():
        kmean_ref[...] = jnp.zeros_like(kmean_ref)
        for h in range(hp):
            for n in range(n_blocks):
                kmean_ref[h, n:n + 1, :] = jnp.mean(
                    k_ref[n * blk:(n + 1) * blk, h * hd:(h + 1) * hd].astype(F32), axis=0, keepdims=True)

    lane = lax.broadcasted_iota(jnp.int32, (blk, LANES), 1)
    nb_pad = -(-n_blocks // SUBLANES) * SUBLANES
    blk_row = lax.broadcasted_iota(jnp.int32, (nb_pad, blk), 0)
    past = blk_row < qi
    q_aug = []
    for h in range(hp):
        q = q_ref[:, h * hd:(h + 1) * hd]
        km = kmean_ref[h]
        km_hi = km.astype(BF16)
        km_lo = (km - km_hi.astype(F32)).astype(BF16)
        gate_t = (_dot_nt(km_hi, q) + _dot_nt(km_lo, q))[:nb_pad, :]
        g = jnp.where(past, gate_t, LOW)
        rank = jnp.zeros(g.shape, jnp.int32)
        for dlt in range(1, nb_pad):
            other = pltpu.roll(g, dlt, axis=0)
            rank = rank + jnp.where(blk_row >= dlt, (other >= g).astype(jnp.int32),
                                    (other > g).astype(jnp.int32))
        chosen_t = jnp.logical_or(jnp.logical_and(rank < MOBA_TOPK, past), blk_row == qi)
        bias_t = jnp.concatenate([jnp.where(chosen_t, 0.0, NEG), jnp.full((LANES - nb_pad, blk), NEG, F32)],
                                 axis=0)
        q_aug.append(jnp.concatenate([q, bias_t.T.astype(BF16)], axis=1))

    pq = pq_ref[...]

    def step(n, carry, masked):
        ks = pl.multiple_of(n * blk, blk)
        onehot = jnp.where(lane == n, 1.0, 0.0).astype(BF16)
        dist_t = jnp.abs(pk_ref[pl.ds(ks, blk), :] - pq)
        s_all = []
        for h in range(hp):
            k_aug = jnp.concatenate([k_ref[pl.ds(ks, blk), h * hd:(h + 1) * hd], onehot], axis=1)
            s_t = _dot_nt(k_aug, q_aug[h]) - slope_ref[hg * hp + h] * dist_t
            if masked:
                key = lax.broadcasted_iota(jnp.int32, s_t.shape, 0)
                qry = lax.broadcasted_iota(jnp.int32, s_t.shape, 1)
                s_t = jnp.where(key <= qry, s_t, NEG)
            s_all.append(s_t)
        stats = [_softmax_stats_t(s_all[h], carry[h][0]) for h in range(hp)]
        out = []
        for h in range(hp):
            m_new, alpha, p = stats[h]
            v_t = _with_ones_rows(vt_ref[n, h * hd:(h + 1) * hd, :])
            out.append((m_new, alpha * carry[h][1] + _dot(v_t, p)))
        return tuple(out)

    init = tuple((jnp.full((1, blk), M_INIT, F32), jnp.zeros((hd + 2 * SUBLANES, blk), F32))
                 for _ in range(hp))
    carry = lax.fori_loop(0, qi, lambda n, c: step(n, c, False), init)
    carry = step(qi, carry, True)
    for h in range(hp):
        o_ref[:, h * hd:(h + 1) * hd] = _normalize_t(carry[h][1], hd).T.astype(BF16)


def _alibi_slopes(n):
    return 2.0 ** (-8.0 * jnp.arange(1, n + 1, dtype=F32) / n)


def _moba_mixer(x, mod, pos_col, pos_f, w_in):
    bsz, seq, d = x.shape
    hh, hd, blk = MOBA_HEADS, MOBA_HD, MOBA_BLOCK
    assert seq % blk == 0 and seq // blk + MOBA_TOPK < LANES and blk == ATT_TK
    n_blocks = seq // blk
    hp = HEADS_PER_STEP
    nqk = 2 * hh * hd
    col_scale = jnp.concatenate([jnp.full((hh * hd,), hd ** -0.5 * LOG2E, F32),
                                 jnp.ones((hh * hd,), F32)]).reshape(1, -1)
    w_bf = w_in.astype(BF16)
    qk, v_t = _project(x, mod, w_bf[:, :nqk], col_scale, nqk, w_bf[:, nqk:].T)
    pos_row = pos_f.reshape(bsz, n_blocks, 1, blk)
    return pl.pallas_call(
        functools.partial(_moba_kernel, n_blocks=n_blocks),
        grid=(bsz, hh // hp, n_blocks),
        in_specs=[pl.BlockSpec(memory_space=pltpu.SMEM),
                  pl.BlockSpec((None, blk, hp * hd), lambda b, h, i: (b, i, h)),
                  pl.BlockSpec((None, seq, hp * hd), lambda b, h, i: (b, 0, hh // hp + h)),
                  pl.BlockSpec((None, n_blocks, hp * hd, blk), lambda b, h, i: (b, 0, h, 0)),
                  pl.BlockSpec((None, seq, 1), lambda b, h, i: (b, 0, 0)),
                  pl.BlockSpec((None, None, 1, blk), lambda b, h, i: (b, i, 0, 0))],
        out_specs=pl.BlockSpec((None, blk, hp * hd), lambda b, h, i: (b, i, h)),
        out_shape=jax.ShapeDtypeStruct((bsz, seq, hh * hd), BF16),
        scratch_shapes=[pltpu.VMEM((hp, LANES, hd), F32)],
        compiler_params=_cparams(3),
        name="moba_attention",
    )(_alibi_slopes(hh) * LOG2E, qk, qk, v_t, pos_col, pos_row)


def _sb_kernel(q_ref, k_ref, vt_ref, o_ref):
    qi = pl.program_id(2)
    tq, tk, hp, hd = q_ref.shape[0], ATT_TK, HEADS_PER_STEP, SB_HD
    rs = lax.broadcasted_iota(jnp.int32, (tk, tk), 0)
    cj = lax.broadcasted_iota(jnp.int32, (tk, tk), 1)
    after = jnp.where(cj > rs, 1.0, 0.0).astype(BF16)
    after2 = jnp.concatenate([after, after], axis=1)

    def step(j, carry, masked):
        ks = pl.multiple_of(j * tk, tk)
        z_all = [_dot_nt(k_ref[pl.ds(ks, tk), h * hd:(h + 1) * hd], q_ref[:, h * hd:(h + 1) * hd])
                 for h in range(hp)]
        if masked:
            key = lax.broadcasted_iota(jnp.int32, (tk, tq), 0)
            qry = lax.broadcasted_iota(jnp.int32, (tk, tq), 1)
            mask = key < qry
        parts = []
        for h in range(hp):
            z2 = z_all[h]
            log_beta = jnp.minimum(z2, 0.0) - jnp.log2(1.0 + jnp.exp2(-jnp.abs(z2)))
            log_1mb = log_beta - z2
            if masked:
                log_1mb = jnp.where(mask, log_1mb, 0.0)
            hi = log_1mb.astype(BF16)
            lo = (log_1mb - hi.astype(F32)).astype(BF16)
            parts.append((log_beta, log_1mb, jnp.concatenate([hi, lo], axis=0)))
        tails = [_dot(after2, parts[h][2]) for h in range(hp)]
        out = []
        for h in range(hp):
            c, acc_t = carry[h]
            log_beta, log_1mb = parts[h][0], parts[h][1]
            a = jnp.exp2(log_beta + tails[h] + c)
            if masked:
                a = jnp.where(mask, a, 0.0)
            acc_t = acc_t + _dot(vt_ref[j, h * hd:(h + 1) * hd, :], a.astype(BF16))
            out.append((c + jnp.sum(log_1mb, axis=0, keepdims=True), acc_t))
        return tuple(out)

    init = tuple((jnp.zeros((1, tq), F32), jnp.zeros((hd, tq), F32)) for _ in range(hp))
    carry = step(qi, init, True)

    def live(state):
        t, cr = state
        c_max = cr[0][0]
        for h in range(1, hp):
            c_max = jnp.maximum(c_max, cr[h][0])
        return jnp.logical_and(t < qi, jnp.max(c_max) > SB_DEAD_LOG2)

    _, carry = lax.while_loop(live, lambda st: (st[0] + 1, step(qi - 1 - st[0], st[1], False)),
                              (jnp.int32(0), carry))
    for h in range(hp):
        o_ref[:, h * hd:(h + 1) * hd] = carry[h][1].T.astype(BF16)


def _sb_mixer(x, mod, w_in):
    bsz, seq, d = x.shape
    hh, hd, hp = SB_HEADS, SB_HD, HEADS_PER_STEP
    nqk = 2 * hh * hd
    col_scale = jnp.concatenate([jnp.full((hh * hd,), hd ** -0.5 * LOG2E, F32),
                                 jnp.ones((hh * hd,), F32)]).reshape(1, -1)
    w_bf = w_in.astype(BF16)
    qk, v_t = _project(x, mod, w_bf[:, :nqk], col_scale, nqk, w_bf[:, nqk:].T)
    tq = ATT_TQ
    assert tq == ATT_TK
    return pl.pallas_call(
        _sb_kernel,
        grid=(bsz, hh // hp, seq // tq),
        in_specs=[pl.BlockSpec((None, tq, hp * hd), lambda b, h, i: (b, i, h)),
                  pl.BlockSpec((None, seq, hp * hd), lambda b, h, i: (b, 0, hh // hp + h)),
                  pl.BlockSpec((None, seq // ATT_TK, hp * hd, ATT_TK), lambda b, h, i: (b, 0, h, 0))],
        out_specs=pl.BlockSpec((None, tq, hp * hd), lambda b, h, i: (b, i, h)),
        out_shape=jax.ShapeDtypeStruct((bsz, seq, hh * hd), BF16),
        compiler_params=_cparams(3),
        name="stick_breaking_attention",
    )(qk, qk, v_t)


def _nsa_compress_kernel(t_ref, pe_ref, w1_ref, w2_ref, o_ref, ot_ref, *, n_cmp):
    st, half = NSA_CMP_STRIDE, NSA_CMP_LEN // NSA_CMP_STRIDE
    assert half == 2
    rows = t_ref.shape[0] // st
    xa, xb = [], []
    for r in range(st):
        xr = t_ref[pl.ds(r, rows, stride=st), :]
        xa.append((xr + pe_ref[r:r + 1, :]).astype(BF16))
        xb.append((xr + pe_ref[st + r:st + r + 1, :]).astype(BF16))
    k_half = st * NSA_HD
    pre_a = _dot(jnp.concatenate(xa, axis=1), w1_ref[:k_half, :])
    pre_b = _dot(jnp.concatenate(xb, axis=1), w1_ref[k_half:, :])
    pre = pre_a + pltpu.roll(pre_b, rows - 1, axis=0)
    out = _dot(_gelu_tanh(pre).astype(BF16), w2_ref[...])
    row = lax.broadcasted_iota(jnp.int32, out.shape, 0)
    out = jnp.where(row < n_cmp, out, 0.0)
    o_ref[...] = out.astype(BF16)
    ot_ref[...] = out.T.astype(BF16)


def _nsa_cmp_select_kernel(slope_ref, q_ref, kc_ref, vct_ref, gt_ref, pq_ref, pe_ref, ovlt_ref,
                           acc_ref, sel_ref, *, n_cmp, n_sel):
    qi = pl.program_id(1)
    tq, gg, rr, hd = q_ref.shape[0], NSA_GROUPS, NSA_R, NSA_HD
    shape = (LANES, tq)
    n_row = lax.broadcasted_iota(jnp.int32, shape, 0)
    qidx = qi * tq + lax.broadcasted_iota(jnp.int32, shape, 1)
    c_end = n_row * NSA_CMP_STRIDE + (NSA_CMP_LEN - 1)
    mask_c = jnp.logical_and(c_end <= qidx, n_row < n_cmp)
    dist_t = jnp.abs(pe_ref[...] - pq_ref[...])
    sig_t = jax.nn.sigmoid(gt_ref[...]).T
    ovl_t = ovlt_ref[...]

    heads = [(g, r) for g in range(gg) for r in range(rr)]
    s_all = [_dot_nt(kc_ref[g], q_ref[:, (g * rr + r) * hd:(g * rr + r + 1) * hd])
             - slope_ref[g * rr + r] * dist_t for g, r in heads]
    p_all = []
    for s_t in s_all:
        s_t = jnp.where(mask_c, s_t, NEG)
        p = jnp.where(mask_c, jnp.exp2(s_t - jnp.max(s_t, axis=0, keepdims=True)), 0.0)
        p_all.append((p / jnp.maximum(jnp.sum(p, axis=0, keepdims=True), TINY)).astype(BF16))
    imp_t = [jnp.zeros((n_sel, tq), F32) for _ in range(gg)]
    for (g, r), p in zip(heads, p_all):
        imp_t[g] = imp_t[g] + _dot(ovl_t, p)
        c = g * rr + r
        acc_ref[:, c * hd:(c + 1) * hd] = (sig_t[c:c + 1, :] * _dot(vct_ref[gg + g], p)).T

    sshape = (n_sel, tq)
    j_row = lax.broadcasted_iota(jnp.int32, sshape, 0)
    q_blk = (qi * tq + lax.broadcasted_iota(jnp.int32, sshape, 1)) >> 6
    assert NSA_SEL_BLOCK == 64
    forced = jnp.logical_or(j_row == 0, jnp.logical_or(j_row == q_blk, j_row == q_blk - 1))
    cand = j_row <= q_blk
    bonus = jnp.where(forced, NSA_FORCE_BONUS, 0.0)
    for g in range(gg):
        score = jnp.where(cand, imp_t[g] + bonus, NEG)
        rank = jnp.zeros(sshape, jnp.int32)
        for dlt in range(1, n_sel):
            other = pltpu.roll(score, dlt, axis=0)
            rank = rank + jnp.where(j_row >= dlt, (other >= score).astype(jnp.int32),
                                    (other > score).astype(jnp.int32))
        chosen_t = jnp.logical_and(rank < NSA_SEL_TOPN, cand)
        bias_t = jnp.concatenate([jnp.where(chosen_t, 0.0, NEG), jnp.full((LANES - n_sel, tq), NEG, F32)],
                                 axis=0)
        sel_ref[g] = bias_t.T.astype(BF16)


def _nsa_select_window_kernel(slope_ref, q_ref, sel_ref, k_ref, vt_ref, gt_ref, pk_ref, pq_ref, prev_ref,
                              out_ref):
    qi = pl.program_id(1)
    tq, tk, hd, gg, rr = q_ref.shape[0], ATT_TK, NSA_HD, NSA_GROUPS, NSA_R
    assert tq == tk and NSA_WINDOW % tk == 0
    lanes = rr * tq
    pq4 = jnp.concatenate([pq_ref[...]] * rr, axis=1)
    qidx = qi * tq + (lax.broadcasted_iota(jnp.int32, (tk, lanes), 1) & (tq - 1))
    krow = lax.broadcasted_iota(jnp.int32, (tk, lanes), 0)
    klane = lax.broadcasted_iota(jnp.int32, (tk, LANES), 1)
    kblk = lax.broadcasted_iota(jnp.int32, (tk, LANES), 0) >> 6
    per_tile = tk // NSA_SEL_BLOCK
    slope_rows = [jnp.concatenate([jnp.full((1, tq), slope_ref[g * rr + r], F32) for r in range(rr)], axis=1)
                  for g in range(gg)]
    q4 = [jnp.concatenate([q_ref[:, (g * rr + r) * hd:(g * rr + r + 1) * hd] for r in range(rr)], axis=0)
          for g in range(gg)]
    q_aug = [jnp.concatenate([q4[g], jnp.concatenate([sel_ref[g]] * rr, axis=0)], axis=1) for g in range(gg)]
    k_win0 = gg * hd

    def update(s_all, carry, v_row0, j):
        stats = [_softmax_stats_t(s_all[g], carry[g][0]) for g in range(gg)]
        out = []
        for g in range(gg):
            m_new, alpha, p = stats[g]
            v_t = _with_ones_rows(vt_ref[j, v_row0 + g * hd:v_row0 + (g + 1) * hd, :])
            out.append((m_new, alpha * carry[g][1] + _dot(v_t, p)))
        return tuple(out)

    def select_step(j, carry, masked):
        ks = pl.multiple_of(j * tk, tk)
        dist = jnp.abs(pk_ref[pl.ds(ks, tk), :] - pq4)
        onehot = jnp.where(klane == j * per_tile + kblk, 1.0, 0.0).astype(BF16)
        s_all = []
        for g in range(gg):
            k_aug = jnp.concatenate([k_ref[pl.ds(ks, tk), g * hd:(g + 1) * hd], onehot], axis=1)
            s_t = _dot_nt(k_aug, q_aug[g]) - slope_rows[g] * dist
            if masked:
                s_t = jnp.where(j * tk + krow <= qidx, s_t, NEG)
            s_all.append(s_t)
        return update(s_all, carry, 0, j)

    def init():
        return tuple((jnp.full((1, lanes), M_INIT, F32), jnp.zeros((hd + 2 * SUBLANES, lanes), F32))
                     for _ in range(gg))

    sel = lax.fori_loop(0, qi, lambda j, c: select_step(j, c, False), init())
    sel = select_step(qi, sel, True)

    win = init()
    n_band = NSA_WINDOW // tk + 1
    qloc = qidx - qi * tq
    for t in range(n_band):
        jt = qi - (n_band - 1) + t
        jc = jnp.maximum(jt, 0)
        ks = pl.multiple_of(jc * tk, tk)
        none = jnp.where(jt >= 0, 0, tk)
        if t == n_band - 1:
            mask = krow <= qloc
        elif t == 0:
            mask = krow > qloc + none
        else:
            mask = krow >= none
        dist = jnp.abs(pk_ref[pl.ds(ks, tk), :] - pq4)
        s_all = []
        for g in range(gg):
            s_t = _dot_nt(k_ref[pl.ds(ks, tk), k_win0 + g * hd:k_win0 + (g + 1) * hd], q4[g])
            s_all.append(jnp.where(mask, s_t - slope_rows[g] * dist, NEG))
        win = update(s_all, win, gg * hd, jc)

    sig_t = jax.nn.sigmoid(gt_ref[...]).T
    for g in range(gg):
        gate = lambda branch: jnp.concatenate(
            [sig_t[branch * NSA_HEADS + g * rr + r:branch * NSA_HEADS + g * rr + r + 1, :] for r in range(rr)],
            axis=1)
        o_t = gate(1) * _normalize_t(sel[g][1], hd) + gate(2) * _normalize_t(win[g][1], hd)
        for r in range(rr):
            c = g * rr + r
            out_ref[:, c * hd:(c + 1) * hd] = (prev_ref[:, c * hd:(c + 1) * hd]
                                                + o_t[:, r * tq:(r + 1) * tq].T).astype(BF16)


def _nsa_mixer(x, mod, pos_col, pos_f, w_in, cmp_pos, cmp_w1, cmp_w2):
    bsz, seq, d = x.shape
    hh, gg, hd = NSA_HEADS, NSA_GROUPS, NSA_HD
    n_cmp = (seq - NSA_CMP_LEN) // NSA_CMP_STRIDE + 1
    n_sel = seq // NSA_SEL_BLOCK
    tq = ATT_TQ
    assert seq % tq == 0 and seq // NSA_CMP_STRIDE == LANES and n_sel % SUBLANES == 0 and n_sel <= LANES
    assert NSA_SEL_TOPN <= n_sel
    nq, nkv = hh * hd, gg * hd
    w_bf = w_in.astype(BF16)
    kv = lambda t: w_bf[:, nq + t * nkv:nq + (t + 1) * nkv]
    n_gate = 3 * hh
    w_perm = jnp.concatenate([w_bf[:, :nq], kv(2), kv(4), kv(0), kv(1),
                              w_bf[:, nq + 6 * nkv:], jnp.zeros((d, LANES - n_gate), BF16)], axis=1)
    n_main = nq + 2 * nkv
    col_scale = jnp.concatenate([jnp.full((nq,), hd ** -0.5 * LOG2E, F32),
                                 jnp.ones((w_perm.shape[1] - nq,), F32)]).reshape(1, -1)
    w_vt = jnp.concatenate([kv(3), kv(5)], axis=1).T
    main, v_t, tail = _project(x, mod, w_perm, col_scale, n_main, w_vt)
    slopes = _alibi_slopes(hh) * LOG2E
    gate_blk = 2 * nkv // LANES

    cmp_spec = pl.BlockSpec((None, None, LANES, hd), lambda b, w, g: (b, w * gg + g, 0, 0))
    cmp_kv, cmp_kv_t = pl.pallas_call(
        functools.partial(_nsa_compress_kernel, n_cmp=n_cmp),
        grid=(bsz, 2, gg),
        in_specs=[pl.BlockSpec((None, seq, hd), lambda b, w, g: (b, 0, w * gg + g)),
                  pl.BlockSpec((None, NSA_CMP_LEN, hd), lambda b, w, g: (w, 0, 0)),
                  pl.BlockSpec((None, NSA_CMP_LEN * hd, hd), lambda b, w, g: (w, 0, 0)),
                  pl.BlockSpec((None, hd, hd), lambda b, w, g: (w, 0, 0))],
        out_specs=[cmp_spec, cmp_spec],
        out_shape=[jax.ShapeDtypeStruct((bsz, 2 * gg, LANES, hd), BF16)] * 2,
        compiler_params=_cparams(3),
        name="nsa_compress",
    )(tail, cmp_pos, cmp_w1.astype(BF16), cmp_w2.astype(BF16))

    pos_cend = jnp.pad(pos_f[:, NSA_CMP_LEN - 1::NSA_CMP_STRIDE], ((0, 0), (0, LANES - n_cmp)))
    pos_cend = pos_cend.reshape(bsz, LANES, 1)
    c_start = jnp.arange(LANES) * NSA_CMP_STRIDE
    j_start = jnp.arange(n_sel) * NSA_SEL_BLOCK
    overlap_t = ((c_start[None, :] < j_start[:, None] + NSA_SEL_BLOCK)
                 & (c_start[None, :] + NSA_CMP_LEN - 1 >= j_start[:, None])
                 & (jnp.arange(LANES)[None, :] < n_cmp)).astype(BF16)

    smem = pl.BlockSpec(memory_space=pltpu.SMEM)
    q_spec = pl.BlockSpec((None, tq, nq), lambda b, i: (b, i, 0))
    gate_spec = pl.BlockSpec((None, tq, LANES), lambda b, i: (b, i, gate_blk))
    pq_spec = pl.BlockSpec((None, None, 1, tq), lambda b, i: (b, i, 0, 0))
    sel_spec = pl.BlockSpec((None, gg, tq, LANES), lambda b, i: (b, 0, i, 0))
    cmp_all = pl.BlockSpec((None, 2 * gg, LANES, hd), lambda b, i: (b, 0, 0, 0))
    pos_q = pos_f.reshape(bsz, seq // tq, 1, tq)
    grid = (bsz, seq // tq)

    acc0, sel_bias = pl.pallas_call(
        functools.partial(_nsa_cmp_select_kernel, n_cmp=n_cmp, n_sel=n_sel),
        grid=grid,
        in_specs=[smem, q_spec, cmp_all, cmp_all, gate_spec, pq_spec,
                  pl.BlockSpec((None, LANES, 1), lambda b, i: (b, 0, 0)),
                  pl.BlockSpec((n_sel, LANES), lambda b, i: (0, 0))],
        out_specs=[q_spec, sel_spec],
        out_shape=[jax.ShapeDtypeStruct((bsz, seq, nq), F32),
                   jax.ShapeDtypeStruct((bsz, gg, seq, LANES), BF16)],
        compiler_params=_cparams(2),
        name="nsa_compressed_attention_select",
    )(slopes, main, cmp_kv, cmp_kv_t, tail, pos_q, pos_cend, overlap_t)

    return pl.pallas_call(
        _nsa_select_window_kernel,
        grid=grid,
        in_specs=[smem, q_spec, sel_spec,
                  pl.BlockSpec((None, seq, 2 * nkv), lambda b, i: (b, 0, nq // (2 * nkv))),
                  pl.BlockSpec((None, seq // ATT_TK, 2 * nkv, ATT_TK), lambda b, i: (b, 0, 0, 0)),
                  gate_spec,
                  pl.BlockSpec((None, seq, 1), lambda b, i: (b, 0, 0)),
                  pq_spec, q_spec],
        out_specs=q_spec,
        out_shape=jax.ShapeDtypeStruct((bsz, seq, nq), BF16),
        compiler_params=_cparams(2),
        name="nsa_selected_window_attention",
    )(slopes, main, sel_bias, main, v_t, tail, pos_col, pos_q, acc0)


def kernel(x, c, positions, mod_w, mod_b, ln_g, ln_b, ffn_w_in, ffn_conv_w, ffn_conv_b, ffn_w_out,
           mla_w_in, mla_q_norm, mla_w_uq, mla_kv_norm, mla_w_ukv, mla_w_o,
           moba_w_in, moba_w_o, nsa_w_in, nsa_cmp_pos, nsa_cmp_w1, nsa_cmp_w2, nsa_w_o,
           sb_w_in, sb_w_o):
    bsz, seq, d = x.shape
    depth = mod_w.shape[0]
    assert d == D_MODEL and seq % TOKEN_TILE == 0 and seq % ATT_TQ == 0
    alpha = float((2 * depth) ** 0.25)
    pos_f = positions.astype(F32)
    pos_col = pos_f.reshape(bsz, seq, 1)
    mod_all = _modulation(c, mod_w, mod_b)
    ffn_w_in_bf, ffn_w_out_bf = ffn_w_in.astype(BF16), ffn_w_out.astype(BF16)
    for i in range(depth):
        kind, j = i % N_MIXERS, i // N_MIXERS
        mod = mod_all[i]
        if kind == 0:
            o = _mla_mixer(x, mod, pos_col, mla_w_in[j], mla_q_norm[j], mla_w_uq[j], mla_kv_norm[j],
                           mla_w_ukv[j])
            w_o = mla_w_o[j]
        elif kind == 1:
            o = _moba_mixer(x, mod, pos_col, pos_f, moba_w_in[j])
            w_o = moba_w_o[j]
        elif kind == 2:
            o = _nsa_mixer(x, mod, pos_col, pos_f, nsa_w_in[j], nsa_cmp_pos[j], nsa_cmp_w1[j],
                           nsa_cmp_w2[j])
            w_o = nsa_w_o[j]
        else:
            o = _sb_mixer(x, mod, sb_w_in[j])
            w_o = sb_w_o[j]
        x = _mixer_out_ffn(o, w_o.astype(BF16), x, mod, ln_g[i], ln_b[i], ffn_w_in_bf, ffn_conv_w[i],
                           ffn_conv_b[i], ffn_w_out_bf, i, alpha)
    return x
```

```python
import functools

import jax
import jax.numpy as jnp
from jax import lax
from jax.experimental import pallas as pl
from jax.experimental.pallas import tpu as pltpu

F32 = jnp.float32
BF16 = jnp.bfloat16

D_MODEL = 1024
N_MIXERS = 4
MLA_HEADS, MLA_NOPE, MLA_ROPE, MLA_V = 8, 128, 64, 128
MLA_Q_RANK, MLA_KV_RANK = 256, 256
ROPE_BASE = 10000.0
MOBA_HEADS, MOBA_HD, MOBA_BLOCK, MOBA_TOPK = 8, 128, 256, 3
NSA_HEADS, NSA_GROUPS, NSA_HD = 8, 2, 128
NSA_R = NSA_HEADS // NSA_GROUPS
NSA_CMP_LEN, NSA_CMP_STRIDE, NSA_SEL_BLOCK, NSA_SEL_TOPN, NSA_WINDOW = 32, 16, 64, 16, 512
NSA_FORCE_BONUS = 100.0
SB_HEADS, SB_HD = 8, 128
D_FF = 2816
LN_EPS = 1e-5
RMS_EPS = 1e-6
NEG = -1e30
M_INIT = -1e29
LOW = -3e38
TINY = 1e-30
LOG2E = 1.4426950408889634
SB_DEAD_LOG2 = -150.0

LANES = 128
SUBLANES = 8
VMEM_LIMIT_BYTES = 56 * 1024 * 1024

TOKEN_TILE = 512
ATT_TQ = 256
ATT_TK = 256
HEADS_PER_STEP = 8
FF_CHUNK = 256
LN_ROW_BLOCKS = 2
PROJ_CHUNK = 512
MOD_COL_TILE = 3072


def _cparams(n_grid):
    return pltpu.CompilerParams(dimension_semantics=("arbitrary",) * n_grid,
                                vmem_limit_bytes=VMEM_LIMIT_BYTES)


def _dot(a, b):
    return jnp.dot(a, b, preferred_element_type=F32)


def _dot_nt(a, b):
    return lax.dot_general(a, b, (((1,), (1,)), ((), ())), preferred_element_type=F32)


def _layer_norm(z, g, b):
    mu = jnp.mean(z, axis=-1, keepdims=True)
    d = z - mu
    var = jnp.mean(d * d, axis=-1, keepdims=True)
    return d * lax.rsqrt(var + LN_EPS) * g + b


def _rms_norm(z, g):
    return z * lax.rsqrt(jnp.mean(z * z, axis=-1, keepdims=True) + RMS_EPS) * g


def _gelu_tanh(x):
    return 0.5 * x * (1.0 + jnp.tanh(0.7978845608028654 * (x + 0.044715 * (x * x * x))))


def _softmax_stats_t(s_t, m):
    m_new = jnp.maximum(m, jnp.max(s_t, axis=0, keepdims=True))
    return m_new, jnp.exp2(m - m_new), jnp.exp2((s_t - m_new).astype(BF16))


def _with_ones_rows(v_t):
    return jnp.concatenate([v_t, jnp.ones((2 * SUBLANES, v_t.shape[1]), v_t.dtype)], axis=0)


def _normalize_t(acc_t, hd):
    return acc_t[:hd, :] / jnp.maximum(acc_t[hd:hd + 1, :], TINY)


def _mod_kernel(c_ref, w_ref, b_ref, o_ref):
    c = c_ref[...]
    c_act = c * jax.nn.sigmoid(c)
    o_ref[...] = jnp.dot(c_act, w_ref[...], preferred_element_type=F32,
                         precision=lax.Precision.HIGHEST) + b_ref[...]


def _modulation(c, mod_w, mod_b):
    depth, d, n = mod_w.shape
    bsz = c.shape[0]
    tn = MOD_COL_TILE
    out = pl.pallas_call(
        _mod_kernel,
        grid=(depth, n // tn),
        in_specs=[pl.BlockSpec((bsz, d), lambda l, j: (0, 0)),
                  pl.BlockSpec((None, d, tn), lambda l, j: (l, 0, j)),
                  pl.BlockSpec((None, 1, tn), lambda l, j: (l, 0, j))],
        out_specs=pl.BlockSpec((None, bsz, tn), lambda l, j: (l, 0, j)),
        out_shape=jax.ShapeDtypeStruct((depth, bsz, n), F32),
        compiler_params=_cparams(2),
        name="modulation",
    )(c, mod_w, mod_b.reshape(depth, 1, n))
    return out.reshape(depth, bsz, 6, d)


def _proj_kernel(x_ref, mod_ref, w_ref, cs_ref, wt_ref, *out_refs, n_main, n_total):
    h = (x_ref[...] * (1.0 + mod_ref[1:2, :]) + mod_ref[0:1, :]).astype(BF16)
    for n0 in range(0, n_total, PROJ_CHUNK):
        n1 = min(n0 + PROJ_CHUNK, n_total)
        y = _dot(h, w_ref[:, n0:n1]) * cs_ref[:, n0:n1]
        if n0 < n_main:
            out_refs[0][:, n0:n1] = y.astype(BF16)
        else:
            out_refs[2][:, n0 - n_main:n1 - n_main] = y
    vt_ref = out_refs[1]
    for n0 in range(0, wt_ref.shape[0], PROJ_CHUNK):
        y_t = _dot_nt(wt_ref[n0:n0 + PROJ_CHUNK, :], h).astype(BF16)
        for t in range(vt_ref.shape[0]):
            vt_ref[t, n0:n0 + PROJ_CHUNK, :] = y_t[:, t * ATT_TK:(t + 1) * ATT_TK]


def _project(x, mod, w, col_scale, n_main, w_t):
    bsz, seq, d = x.shape
    n_total, n_t = w.shape[1], w_t.shape[0]
    assert n_main % PROJ_CHUNK == 0 and n_total % LANES == 0 and n_t % PROJ_CHUNK == 0
    tm = TOKEN_TILE
    out_shape = [jax.ShapeDtypeStruct((bsz, seq, n_main), BF16),
                 jax.ShapeDtypeStruct((bsz, seq // ATT_TK, n_t, ATT_TK), BF16)]
    out_specs = [pl.BlockSpec((None, tm, n_main), lambda b, i: (b, i, 0)),
                 pl.BlockSpec((None, tm // ATT_TK, n_t, ATT_TK), lambda b, i: (b, i, 0, 0))]
    if n_main < n_total:
        out_shape.append(jax.ShapeDtypeStruct((bsz, seq, n_total - n_main), F32))
        out_specs.append(pl.BlockSpec((None, tm, n_total - n_main), lambda b, i: (b, i, 0)))
    return pl.pallas_call(
        functools.partial(_proj_kernel, n_main=n_main, n_total=n_total),
        grid=(bsz, seq // tm),
        in_specs=[pl.BlockSpec((None, tm, d), lambda b, i: (b, i, 0)),
                  pl.BlockSpec((None, 6, d), lambda b, i: (b, 0, 0)),
                  pl.BlockSpec((d, n_total), lambda b, i: (0, 0)),
                  pl.BlockSpec((1, n_total), lambda b, i: (0, 0)),
                  pl.BlockSpec((n_t, d), lambda b, i: (0, 0))],
        out_specs=out_specs,
        out_shape=out_shape,
        compiler_params=_cparams(2),
        name="modulate_project",
    )(x, mod, w, col_scale, w_t)


def _mixer_out_ffn_kernel(o_ref, wmix_ref, x_ref, mod_ref, g1_ref, b1_ref, win_ref, cw_ref, cb_ref,
                          wo_ref, g2_ref, b2_ref, out_ref, a_scr, g_scr, halo_scr, *, alpha):
    tm = x_ref.shape[0]
    halo = halo_scr.shape[0]
    rb = tm // LN_ROW_BLOCKS
    x1 = jnp.concatenate(
        [_layer_norm(alpha * x_ref[r0:r0 + rb, :] + mod_ref[2:3, :] * _dot(o_ref[r0:r0 + rb, :], wmix_ref[...]),
                     g1_ref[...], b1_ref[...]) for r0 in range(0, tm, rb)], axis=0)
    shift, scale, gate = mod_ref[3:4, :], mod_ref[4:5, :], mod_ref[5:6, :]
    h = (x1 * (1.0 + scale) + shift).astype(BF16)

    @pl.when(pl.program_id(1) == 0)
    def _():
        halo_scr[...] = jnp.zeros_like(halo_scr)

    h_ext = jnp.concatenate([halo_scr[...], h], axis=0)
    halo_scr[...] = h[tm - halo:, :]
    chunks = [(c0, min(c0 + FF_CHUNK, D_FF)) for c0 in range(0, D_FF, FF_CHUNK)]

    def matmuls(i):
        c0, c1 = chunks[i]
        a_scr[i % 2, :, :c1 - c0] = _dot(h_ext, win_ref[:, c0:c1])
        return _dot(h, win_ref[:, D_FF + c0:D_FF + c1])

    b_next = matmuls(0)
    for i, (c0, c1) in enumerate(chunks):
        b = b_next
        if i + 1 < len(chunks):
            b_next = matmuls(i + 1)
        w = c1 - c0
        a = a_scr[i % 2, halo:halo + tm, :w]
        a_m1 = a_scr[i % 2, halo - 1:halo - 1 + tm, :w]
        a_m2 = a_scr[i % 2, halo - 2:halo - 2 + tm, :w]
        conv = (a * cw_ref[2:3, c0:c1] + a_m1 * cw_ref[1:2, c0:c1] + a_m2 * cw_ref[0:1, c0:c1]
                + cb_ref[:, c0:c1])
        g_scr[:, c0:c1] = (_gelu_tanh(conv) * b).astype(BF16)
    for r0 in range(0, tm, rb):
        y = _dot(g_scr[r0:r0 + rb, :], wo_ref[...])
        out_ref[r0:r0 + rb, :] = _layer_norm(alpha * x1[r0:r0 + rb, :] + gate * y, g2_ref[...], b2_ref[...])


def _mixer_out_ffn(o, w_mix, x, mod, ln_g, ln_b, w_in_all, conv_w, conv_b, w_out_all, layer, alpha):
    bsz, seq, d = x.shape
    k = o.shape[-1]
    tm = TOKEN_TILE
    halo = 2 * SUBLANES
    const = lambda shape: pl.BlockSpec(shape, lambda b, i: (0, 0), pipeline_mode=pl.Buffered(1))
    of_layer = lambda shape: pl.BlockSpec((None,) + shape, lambda b, i: (layer, 0, 0),
                                          pipeline_mode=pl.Buffered(1))
    return pl.pallas_call(
        functools.partial(_mixer_out_ffn_kernel, alpha=alpha),
        grid=(bsz, seq // tm),
        in_specs=[pl.BlockSpec((None, tm, k), lambda b, i: (b, i, 0)),
                  const((k, d)),
                  pl.BlockSpec((None, tm, d), lambda b, i: (b, i, 0)),
                  pl.BlockSpec((None, 6, d), lambda b, i: (b, 0, 0)),
                  const((1, d)), const((1, d)),
                  of_layer((d, 2 * D_FF)), const((3, D_FF)), const((1, D_FF)),
                  of_layer((D_FF, d)), const((1, d)), const((1, d))],
        out_specs=pl.BlockSpec((None, tm, d), lambda b, i: (b, i, 0)),
        out_shape=jax.ShapeDtypeStruct((bsz, seq, d), F32),
        scratch_shapes=[pltpu.VMEM((2, halo + tm, FF_CHUNK), F32), pltpu.VMEM((tm, D_FF), BF16),
                        pltpu.VMEM((halo, d), BF16)],
        compiler_params=_cparams(2),
        name="mixer_out_conv_ffn",
    )(o, w_mix, x, mod, ln_g[0].reshape(1, d), ln_b[0].reshape(1, d), w_in_all, conv_w,
      conv_b.reshape(1, D_FF), w_out_all, ln_g[1].reshape(1, d), ln_b[1].reshape(1, d))


def _mla_proj_kernel(x_ref, mod_ref, pos_ref, invf_ref, win_ref, qn_ref, kvn_ref, wuq_ref,
                     wuqr_ref, wuk_ref, wuvt_ref, q_ref, kn_ref, kr_ref, vt_ref, *, scale):
    h = (x_ref[...] * (1.0 + mod_ref[1:2, :]) + mod_ref[0:1, :]).astype(BF16)
    proj = _dot(h, win_ref[...])
    c_q = _rms_norm(proj[:, :MLA_Q_RANK], qn_ref[...]).astype(BF16)
    c_kv = _rms_norm(proj[:, MLA_Q_RANK:MLA_Q_RANK + MLA_KV_RANK], kvn_ref[...]).astype(BF16)
    ang = pos_ref[...] * invf_ref[...]
    cos, sin = jnp.cos(ang), jnp.sin(ang)
    r0 = MLA_Q_RANK + MLA_KV_RANK
    kr_ref[...] = (proj[:, r0:r0 + LANES] * cos + proj[:, r0 + LANES:r0 + 2 * LANES] * sin).astype(BF16)
    for hd in range(MLA_HEADS):
        qa = _dot(c_q, wuq_ref[:, hd * 2 * LANES:(hd + 1) * 2 * LANES])
        qr = _dot(c_q, wuqr_ref[:, hd * LANES:(hd + 1) * LANES])
        q_ref[:, hd * 2 * LANES:hd * 2 * LANES + LANES] = (qa[:, :LANES] * scale).astype(BF16)
        q_ref[:, hd * 2 * LANES + LANES:(hd + 1) * 2 * LANES] = (
            (qa[:, LANES:] * cos + qr * sin) * scale).astype(BF16)
    kn_ref[...] = _dot(c_kv, wuk_ref[...]).astype(BF16)
    v_t = _dot_nt(wuvt_ref[...], c_kv).astype(BF16)
    for t in range(vt_ref.shape[0]):
        vt_ref[t] = v_t[:, t * ATT_TK:(t + 1) * ATT_TK]


def _mla_attn_kernel(q_ref, kn_ref, kr_ref, vt_ref, o_ref):
    qi = pl.program_id(2)
    tq = q_ref.shape[0]
    hp = HEADS_PER_STEP

    def step(j, carry, masked):
        ks = pl.multiple_of(j * ATT_TK, ATT_TK)
        kr = kr_ref[pl.ds(ks, ATT_TK), :]
        def scores(h):
            k = jnp.concatenate([kn_ref[pl.ds(ks, ATT_TK), h * MLA_NOPE:(h + 1) * MLA_NOPE], kr], axis=1)
            s_t = _dot_nt(k, q_ref[:, h * 2 * LANES:(h + 1) * 2 * LANES])
            if masked:
                key = lax.broadcasted_iota(jnp.int32, s_t.shape, 0)
                qry = lax.broadcasted_iota(jnp.int32, s_t.shape, 1)
                s_t = jnp.where(key <= qry, s_t, NEG)
            return s_t

        s_all = [scores(h) for h in range(hp)]
        stats = [_softmax_stats_t(s_all[h], carry[h][0]) for h in range(hp)]
        out = []
        for h in range(hp):
            m_new, alpha, p = stats[h]
            v_t = _with_ones_rows(vt_ref[j, h * MLA_V:(h + 1) * MLA_V, :])
            out.append((m_new, alpha * carry[h][1] + _dot(v_t, p)))
        return tuple(out)

    init = tuple((jnp.full((1, tq), M_INIT, F32), jnp.zeros((MLA_V + 2 * SUBLANES, tq), F32))
                 for _ in range(hp))
    carry = lax.fori_loop(0, qi, lambda j, c: step(j, c, False), init)
    carry = step(qi, carry, True)
    for h in range(hp):
        o_ref[:, h * MLA_V:(h + 1) * MLA_V] = _normalize_t(carry[h][1], MLA_V).T.astype(BF16)


def _mla_mixer(x, mod, pos_col, w_in, q_norm, w_uq, kv_norm, w_ukv):
    bsz, seq, d = x.shape
    hh, dn, dr, dv = MLA_HEADS, MLA_NOPE, MLA_ROPE, MLA_V
    half = dr // 2
    r0 = MLA_Q_RANK + MLA_KV_RANK
    w_in, w_uq, w_ukv = w_in.astype(BF16), w_uq.astype(BF16), w_ukv.astype(BF16)
    zpad = lambda rows, n: jnp.zeros((rows, n), BF16)
    x1, x2 = w_in[:, r0:r0 + half], w_in[:, r0 + half:r0 + dr]
    w_in_ext = jnp.concatenate([w_in[:, :r0], x1, x2, zpad(d, LANES - dr),
                                -x2, x1, zpad(d, LANES - dr)], axis=1)
    wq = w_uq.reshape(MLA_Q_RANK, hh, dn + dr)
    q1, q2 = wq[:, :, dn:dn + half], wq[:, :, dn + half:]
    zq = jnp.zeros((MLA_Q_RANK, hh, LANES - dr), BF16)
    w_uq_main = jnp.concatenate([wq[:, :, :dn], q1, q2, zq], axis=2).reshape(MLA_Q_RANK, hh * 2 * LANES)
    w_uq_rot = jnp.concatenate([-q2, q1, zq], axis=2).reshape(MLA_Q_RANK, hh * LANES)
    wkv = w_ukv.reshape(MLA_KV_RANK, hh, dn + dv)
    w_uk = wkv[:, :, :dn].reshape(MLA_KV_RANK, hh * dn)
    w_uv_t = wkv[:, :, dn:].reshape(MLA_KV_RANK, hh * dv).T
    inv_freq = ROPE_BASE ** (-jnp.arange(half, dtype=F32) / half)
    invf = jnp.concatenate([inv_freq, inv_freq, jnp.zeros((LANES - dr,), F32)]).reshape(1, LANES)

    tm = TOKEN_TILE
    const = lambda b, i: (0, 0)
    tok = lambda n: pl.BlockSpec((None, tm, n), lambda b, i: (b, i, 0))
    q, kn, kr, v_t = pl.pallas_call(
        functools.partial(_mla_proj_kernel, scale=float((dn + dr) ** -0.5) * LOG2E),
        grid=(bsz, seq // tm),
        in_specs=[tok(d),
                  pl.BlockSpec((None, 6, d), lambda b, i: (b, 0, 0)),
                  tok(1),
                  pl.BlockSpec((1, LANES), const),
                  pl.BlockSpec(w_in_ext.shape, const),
                  pl.BlockSpec((1, MLA_Q_RANK), const),
                  pl.BlockSpec((1, MLA_KV_RANK), const),
                  pl.BlockSpec(w_uq_main.shape, const),
                  pl.BlockSpec(w_uq_rot.shape, const),
                  pl.BlockSpec(w_uk.shape, const),
                  pl.BlockSpec(w_uv_t.shape, const)],
        out_specs=[tok(hh * 2 * LANES), tok(hh * dn), tok(LANES),
                   pl.BlockSpec((None, tm // ATT_TK, hh * dv, ATT_TK), lambda b, i: (b, i, 0, 0))],
        out_shape=[jax.ShapeDtypeStruct((bsz, seq, hh * 2 * LANES), BF16),
                   jax.ShapeDtypeStruct((bsz, seq, hh * dn), BF16),
                   jax.ShapeDtypeStruct((bsz, seq, LANES), BF16),
                   jax.ShapeDtypeStruct((bsz, seq // ATT_TK, hh * dv, ATT_TK), BF16)],
        compiler_params=_cparams(2),
        name="mla_project",
    )(x, mod, pos_col, invf, w_in_ext, q_norm.reshape(1, -1), kv_norm.reshape(1, -1),
      w_uq_main, w_uq_rot, w_uk, w_uv_t)

    tq, hp = ATT_TQ, HEADS_PER_STEP
    return pl.pallas_call(
        _mla_attn_kernel,
        grid=(bsz, hh // hp, seq // tq),
        in_specs=[pl.BlockSpec((None, tq, hp * 2 * LANES), lambda b, h, i: (b, i, h)),
                  pl.BlockSpec((None, seq, hp * dn), lambda b, h, i: (b, 0, h)),
                  pl.BlockSpec((None, seq, LANES), lambda b, h, i: (b, 0, 0)),
                  pl.BlockSpec((None, seq // ATT_TK, hp * dv, ATT_TK), lambda b, h, i: (b, 0, h, 0))],
        out_specs=pl.BlockSpec((None, tq, hp * dv), lambda b, h, i: (b, i, h)),
        out_shape=jax.ShapeDtypeStruct((bsz, seq, hh * dv), BF16),
        compiler_params=_cparams(3),
        name="mla_attention",
    )(q, kn, kr, v_t)


def _moba_kernel(slope_ref, q_ref, k_ref, vt_ref, pk_ref, pq_ref, o_ref, kmean_ref, *, n_blocks):
    blk, hp, hd = MOBA_BLOCK, HEADS_PER_STEP, MOBA_HD
    hg = pl.program_id(1)
    qi = pl.program_id(2)

    @pl.when(qi == 0)
    def _():
        kmean_ref[...] = jnp.zeros_like(kmean_ref)
        for h in range(hp):
            for n in range(n_blocks):
                kmean_ref[h, n:n + 1, :] = jnp.mean(
                    k_ref[n * blk:(n + 1) * blk, h * hd:(h + 1) * hd].astype(F32), axis=0, keepdims=True)

    lane = lax.broadcasted_iota(jnp.int32, (blk, LANES), 1)
    nb_pad = -(-n_blocks // SUBLANES) * SUBLANES
    blk_row = lax.broadcasted_iota(jnp.int32, (nb_pad, blk), 0)
    past = blk_row < qi
    q_aug = []
    for h in range(hp):
        q = q_ref[:, h * hd:(h + 1) * hd]
        km = kmean_ref[h]
        km_hi = km.astype(BF16)
        km_lo = (km - km_hi.astype(F32)).astype(BF16)
        gate_t = (_dot_nt(km_hi, q) + _dot_nt(km_lo, q))[:nb_pad, :]
        g = jnp.where(past, gate_t, LOW)
        rank = jnp.zeros(g.shape, jnp.int32)
        for dlt in range(1, nb_pad):
            other = pltpu.roll(g, dlt, axis=0)
            rank = rank + jnp.where(blk_row >= dlt, (other >= g).astype(jnp.int32),
                                    (other > g).astype(jnp.int32))
        chosen_t = jnp.logical_or(jnp.logical_and(rank < MOBA_TOPK, past), blk_row == qi)
        bias_t = jnp.concatenate([jnp.where(chosen_t, 0.0, NEG), jnp.full((LANES - nb_pad, blk), NEG, F32)],
                                 axis=0)
        q_aug.append(jnp.concatenate([q, bias_t.T.astype(BF16)], axis=1))

    pq = pq_ref[...]

    def step(n, carry, masked):
        ks = pl.multiple_of(n * blk, blk)
        onehot = jnp.where(lane == n, 1.0, 0.0).astype(BF16)
        dist_t = jnp.abs(pk_ref[pl.ds(ks, blk), :] - pq)
        s_all = []
        for h in range(hp):
            k_aug = jnp.concatenate([k_ref[pl.ds(ks, blk), h * hd:(h + 1) * hd], onehot], axis=1)
            s_t = _dot_nt(k_aug, q_aug[h]) - slope_ref[hg * hp + h] * dist_t
            if masked:
                key = lax.broadcasted_iota(jnp.int32, s_t.shape, 0)
                qry = lax.broadcasted_iota(jnp.int32, s_t.shape, 1)
                s_t = jnp.where(key <= qry, s_t, NEG)
            s_all.append(s_t)
        stats = [_softmax_stats_t(s_all[h], carry[h][0]) for h in range(hp)]
        out = []
        for h in range(hp):
            m_new, alpha, p = stats[h]
            v_t = _with_ones_rows(vt_ref[n, h * hd:(h + 1) * hd, :])
            out.append((m_new, alpha * carry[h][1] + _dot(v_t, p)))
        return tuple(out)

    init = tuple((jnp.full((1, blk), M_INIT, F32), jnp.zeros((hd + 2 * SUBLANES, blk), F32))
                 for _ in range(hp))
    carry = lax.fori_loop(0, qi, lambda n, c: step(n, c, False), init)
    carry = step(qi, carry, True)
    for h in range(hp):
        o_ref[:, h * hd:(h + 1) * hd] = _normalize_t(carry[h][1], hd).T.astype(BF16)


def _alibi_slopes(n):
    return 2.0 ** (-8.0 * jnp.arange(1, n + 1, dtype=F32) / n)


def _moba_mixer(x, mod, pos_col, pos_f, w_in):
    bsz, seq, d = x.shape
    hh, hd, blk = MOBA_HEADS, MOBA_HD, MOBA_BLOCK
    assert seq % blk == 0 and seq // blk + MOBA_TOPK < LANES and blk == ATT_TK
    n_blocks = seq // blk
    hp = HEADS_PER_STEP
    nqk = 2 * hh * hd
    col_scale = jnp.concatenate([jnp.full((hh * hd,), hd ** -0.5 * LOG2E, F32),
                                 jnp.ones((hh * hd,), F32)]).reshape(1, -1)
    w_bf = w_in.astype(BF16)
    qk, v_t = _project(x, mod, w_bf[:, :nqk], col_scale, nqk, w_bf[:, nqk:].T)
    pos_row = pos_f.reshape(bsz, n_blocks, 1, blk)
    return pl.pallas_call(
        functools.partial(_moba_kernel, n_blocks=n_blocks),
        grid=(bsz, hh // hp, n_blocks),
        in_specs=[pl.BlockSpec(memory_space=pltpu.SMEM),
                  pl.BlockSpec((None, blk, hp * hd), lambda b, h, i: (b, i, h)),
                  pl.BlockSpec((None, seq, hp * hd), lambda b, h, i: (b, 0, hh // hp + h)),
                  pl.BlockSpec((None, n_blocks, hp * hd, blk), lambda b, h, i: (b, 0, h, 0)),
                  pl.BlockSpec((None, seq, 1), lambda b, h, i: (b, 0, 0)),
                  pl.BlockSpec((None, None, 1, blk), lambda b, h, i: (b, i, 0, 0))],
        out_specs=pl.BlockSpec((None, blk, hp * hd), lambda b, h, i: (b, i, h)),
        out_shape=jax.ShapeDtypeStruct((bsz, seq, hh * hd), BF16),
        scratch_shapes=[pltpu.VMEM((hp, LANES, hd), F32)],
        compiler_params=_cparams(3),
        name="moba_attention",
    )(_alibi_slopes(hh) * LOG2E, qk, qk, v_t, pos_col, pos_row)


def _sb_kernel(q_ref, k_ref, vt_ref, o_ref):
    qi = pl.program_id(2)
    tq, tk, hp, hd = q_ref.shape[0], ATT_TK, HEADS_PER_STEP, SB_HD
    rs = lax.broadcasted_iota(jnp.int32, (tk, tk), 0)
    cj = lax.broadcasted_iota(jnp.int32, (tk, tk), 1)
    after = jnp.where(cj > rs, 1.0, 0.0).astype(BF16)
    after2 = jnp.concatenate([after, after], axis=1)

    def step(j, carry, masked):
        ks = pl.multiple_of(j * tk, tk)
        z_all = [_dot_nt(k_ref[pl.ds(ks, tk), h * hd:(h + 1) * hd], q_ref[:, h * hd:(h + 1) * hd])
                 for h in range(hp)]
        if masked:
            key = lax.broadcasted_iota(jnp.int32, (tk, tq), 0)
            qry = lax.broadcasted_iota(jnp.int32, (tk, tq), 1)
            mask = key < qry
        parts = []
        for h in range(hp):
            z2 = z_all[h]
            log_beta = jnp.minimum(z2, 0.0) - jnp.log2(1.0 + jnp.exp2(-jnp.abs(z2)))
            log_1mb = log_beta - z2
            if masked:
                log_1mb = jnp.where(mask, log_1mb, 0.0)
            hi = log_1mb.astype(BF16)
            lo = (log_1mb - hi.astype(F32)).astype(BF16)
            parts.append((log_beta, log_1mb, jnp.concatenate([hi, lo], axis=0)))
        tails = [_dot(after2, parts[h][2]) for h in range(hp)]
        out = []
        for h in range(hp):
            c, acc_t = carry[h]
            log_beta, log_1mb = parts[h][0], parts[h][1]
            a = jnp.exp2(log_beta + tails[h] + c)
            if masked:
                a = jnp.where(mask, a, 0.0)
            acc_t = acc_t + _dot(vt_ref[j, h * hd:(h + 1) * hd, :], a.astype(BF16))
            out.append((c + jnp.sum(log_1mb, axis=0, keepdims=True), acc_t))
        return tuple(out)

    init = tuple((jnp.zeros((1, tq), F32), jnp.zeros((hd, tq), F32)) for _ in range(hp))
    carry = step(qi, init, True)

    def live(state):
        t, cr = state
        c_max = cr[0][0]
        for h in range(1, hp):
            c_max = jnp.maximum(c_max, cr[h][0])
        return jnp.logical_and(t < qi, jnp.max(c_max) > SB_DEAD_LOG2)

    _, carry = lax.while_loop(live, lambda st: (st[0] + 1, step(qi - 1 - st[0], st[1], False)),
                              (jnp.int32(0), carry))
    for h in range(hp):
        o_ref[:, h * hd:(h + 1) * hd] = carry[h][1].T.astype(BF16)


def _sb_mixer(x, mod, w_in):
    bsz, seq, d = x.shape
    hh, hd, hp = SB_HEADS, SB_HD, HEADS_PER_STEP
    nqk = 2 * hh * hd
    col_scale = jnp.concatenate([jnp.full((hh * hd,), hd ** -0.5 * LOG2E, F32),
                                 jnp.ones((hh * hd,), F32)]).reshape(1, -1)
    w_bf = w_in.astype(BF16)
    qk, v_t = _project(x, mod, w_bf[:, :nqk], col_scale, nqk, w_bf[:, nqk:].T)
    tq = ATT_TQ
    assert tq == ATT_TK
    return pl.pallas_call(
        _sb_kernel,
        grid=(bsz, hh // hp, seq // tq),
        in_specs=[pl.BlockSpec((None, tq, hp * hd), lambda b, h, i: (b, i, h)),
                  pl.BlockSpec((None, seq, hp * hd), lambda b, h, i: (b, 0, hh // hp + h)),
                  pl.BlockSpec((None, seq // ATT_TK, hp * hd, ATT_TK), lambda b, h, i: (b, 0, h, 0))],
        out_specs=pl.BlockSpec((None, tq, hp * hd), lambda b, h, i: (b, i, h)),
        out_shape=jax.ShapeDtypeStruct((bsz, seq, hh * hd), BF16),
        compiler_params=_cparams(3),
        name="stick_breaking_attention",
    )(qk, qk, v_t)


def _nsa_compress_kernel(t_ref, pe_ref, w1_ref, w2_ref, o_ref, ot_ref, *, n_cmp):
    st, half = NSA_CMP_STRIDE, NSA_CMP_LEN // NSA_CMP_STRIDE
    assert half == 2
    rows = t_ref.shape[0] // st
    xa, xb = [], []
    for r in range(st):
        xr = t_ref[pl.ds(r, rows, stride=st), :]
        xa.append((xr + pe_ref[r:r + 1, :]).astype(BF16))
        xb.append((xr + pe_ref[st + r:st + r + 1, :]).astype(BF16))
    k_half = st * NSA_HD
    pre_a = _dot(jnp.concatenate(xa, axis=1), w1_ref[:k_half, :])
    pre_b = _dot(jnp.concatenate(xb, axis=1), w1_ref[k_half:, :])
    pre = pre_a + pltpu.roll(pre_b, rows - 1, axis=0)
    out = _dot(_gelu_tanh(pre).astype(BF16), w2_ref[...])
    row = lax.broadcasted_iota(jnp.int32, out.shape, 0)
    out = jnp.where(row < n_cmp, out, 0.0)
    o_ref[...] = out.astype(BF16)
    ot_ref[...] = out.T.astype(BF16)


def _nsa_cmp_select_kernel(slope_ref, q_ref, kc_ref, vct_ref, gt_ref, pq_ref, pe_ref, ovlt_ref,
                           acc_ref, sel_ref, *, n_cmp, n_sel):
    qi = pl.program_id(1)
    tq, gg, rr, hd = q_ref.shape[0], NSA_GROUPS, NSA_R, NSA_HD
    shape = (LANES, tq)
    n_row = lax.broadcasted_iota(jnp.int32, shape, 0)
    qidx = qi * tq + lax.broadcasted_iota(jnp.int32, shape, 1)
    c_end = n_row * NSA_CMP_STRIDE + (NSA_CMP_LEN - 1)
    mask_c = jnp.logical_and(c_end <= qidx, n_row < n_cmp)
    dist_t = jnp.abs(pe_ref[...] - pq_ref[...])
    sig_t = jax.nn.sigmoid(gt_ref[...]).T
    ovl_t = ovlt_ref[...]

    heads = [(g, r) for g in range(gg) for r in range(rr)]
    s_all = [_dot_nt(kc_ref[g], q_ref[:, (g * rr + r) * hd:(g * rr + r + 1) * hd])
             - slope_ref[g * rr + r] * dist_t for g, r in heads]
    p_all = []
    for s_t in s_all:
        s_t = jnp.where(mask_c, s_t, NEG)
        p = jnp.where(mask_c, jnp.exp2(s_t - jnp.max(s_t, axis=0, keepdims=True)), 0.0)
        p_all.append((p / jnp.maximum(jnp.sum(p, axis=0, keepdims=True), TINY)).astype(BF16))
    imp_t = [jnp.zeros((n_sel, tq), F32) for _ in range(gg)]
    for (g, r), p in zip(heads, p_all):
        imp_t[g] = imp_t[g] + _dot(ovl_t, p)
        c = g * rr + r
        acc_ref[:, c * hd:(c + 1) * hd] = (sig_t[c:c + 1, :] * _dot(vct_ref[gg + g], p)).T

    sshape = (n_sel, tq)
    j_row = lax.broadcasted_iota(jnp.int32, sshape, 0)
    q_blk = (qi * tq + lax.broadcasted_iota(jnp.int32, sshape, 1)) >> 6
    assert NSA_SEL_BLOCK == 64
    forced = jnp.logical_or(j_row == 0, jnp.logical_or(j_row == q_blk, j_row == q_blk - 1))
    cand = j_row <= q_blk
    bonus = jnp.where(forced, NSA_FORCE_BONUS, 0.0)
    for g in range(gg):
        score = jnp.where(cand, imp_t[g] + bonus, NEG)
        rank = jnp.zeros(sshape, jnp.int32)
        for dlt in range(1, n_sel):
            other = pltpu.roll(score, dlt, axis=0)
            rank = rank + jnp.where(j_row >= dlt, (other >= score).astype(jnp.int32),
                                    (other > score).astype(jnp.int32))
        chosen_t = jnp.logical_and(rank < NSA_SEL_TOPN, cand)
        bias_t = jnp.concatenate([jnp.where(chosen_t, 0.0, NEG), jnp.full((LANES - n_sel, tq), NEG, F32)],
                                 axis=0)
        sel_ref[g] = bias_t.T.astype(BF16)


def _nsa_select_window_kernel(slope_ref, q_ref, sel_ref, k_ref, vt_ref, gt_ref, pk_ref, pq_ref, prev_ref,
                              out_ref):
    qi = pl.program_id(1)
    tq, tk, hd, gg, rr = q_ref.shape[0], ATT_TK, NSA_HD, NSA_GROUPS, NSA_R
    assert tq == tk and NSA_WINDOW % tk == 0
    lanes = rr * tq
    pq4 = jnp.concatenate([pq_ref[...]] * rr, axis=1)
    qidx = qi * tq + (lax.broadcasted_iota(jnp.int32, (tk, lanes), 1) & (tq - 1))
    krow = lax.broadcasted_iota(jnp.int32, (tk, lanes), 0)
    klane = lax.broadcasted_iota(jnp.int32, (tk, LANES), 1)
    kblk = lax.broadcasted_iota(jnp.int32, (tk, LANES), 0) >> 6
    per_tile = tk // NSA_SEL_BLOCK
    slope_rows = [jnp.concatenate([jnp.full((1, tq), slope_ref[g * rr + r], F32) for r in range(rr)], axis=1)
                  for g in range(gg)]
    q4 = [jnp.concatenate([q_ref[:, (g * rr + r) * hd:(g * rr + r + 1) * hd] for r in range(rr)], axis=0)
          for g in range(gg)]
    q_aug = [jnp.concatenate([q4[g], jnp.concatenate([sel_ref[g]] * rr, axis=0)], axis=1) for g in range(gg)]
    k_win0 = gg * hd

    def update(s_all, carry, v_row0, j):
        stats = [_softmax_stats_t(s_all[g], carry[g][0]) for g in range(gg)]
        out = []
        for g in range(gg):
            m_new, alpha, p = stats[g]
            v_t = _with_ones_rows(vt_ref[j, v_row0 + g * hd:v_row0 + (g + 1) * hd, :])
            out.append((m_new, alpha * carry[g][1] + _dot(v_t, p)))
        return tuple(out)

    def select_step(j, carry, masked):
        ks = pl.multiple_of(j * tk, tk)
        dist = jnp.abs(pk_ref[pl.ds(ks, tk), :] - pq4)
        onehot = jnp.where(klane == j * per_tile + kblk, 1.0, 0.0).astype(BF16)
        s_all = []
        for g in range(gg):
            k_aug = jnp.concatenate([k_ref[pl.ds(ks, tk), g * hd:(g + 1) * hd], onehot], axis=1)
            s_t = _dot_nt(k_aug, q_aug[g]) - slope_rows[g] * dist
            if masked:
                s_t = jnp.where(j * tk + krow <= qidx, s_t, NEG)
            s_all.append(s_t)
        return update(s_all, carry, 0, j)

    def init():
        return tuple((jnp.full((1, lanes), M_INIT, F32), jnp.zeros((hd + 2 * SUBLANES, lanes), F32))
                     for _ in range(gg))

    sel = lax.fori_loop(0, qi, lambda j, c: select_step(j, c, False), init())
    sel = select_step(qi, sel, True)

    win = init()
    n_band = NSA_WINDOW // tk + 1
    qloc = qidx - qi * tq
    for t in range(n_band):
        jt = qi - (n_band - 1) + t
        jc = jnp.maximum(jt, 0)
        ks = pl.multiple_of(jc * tk, tk)
        none = jnp.where(jt >= 0, 0, tk)
        if t == n_band - 1:
            mask = krow <= qloc
        elif t == 0:
            mask = krow > qloc + none
        else:
            mask = krow >= none
        dist = jnp.abs(pk_ref[pl.ds(ks, tk), :] - pq4)
        s_all = []
        for g in range(gg):
            s_t = _dot_nt(k_ref[pl.ds(ks, tk), k_win0 + g * hd:k_win0 + (g + 1) * hd], q4[g])
            s_all.append(jnp.where(mask, s_t - slope_rows[g] * dist, NEG))
        win = update(s_all, win, gg * hd, jc)

    sig_t = jax.nn.sigmoid(gt_ref[...]).T
    for g in range(gg):
        gate = lambda branch: jnp.concatenate(
            [sig_t[branch * NSA_HEADS + g * rr + r:branch * NSA_HEADS + g * rr + r + 1, :] for r in range(rr)],
            axis=1)
        o_t = gate(1) * _normalize_t(sel[g][1], hd) + gate(2) * _normalize_t(win[g][1], hd)
        for r in range(rr):
            c = g * rr + r
            out_ref[:, c * hd:(c + 1) * hd] = (prev_ref[:, c * hd:(c + 1) * hd]
                                                + o_t[:, r * tq:(r + 1) * tq].T).astype(BF16)


def _nsa_mixer(x, mod, pos_col, pos_f, w_in, cmp_pos, cmp_w1, cmp_w2):
    bsz, seq, d = x.shape
    hh, gg, hd = NSA_HEADS, NSA_GROUPS, NSA_HD
    n_cmp = (seq - NSA_CMP_LEN) // NSA_CMP_STRIDE + 1
    n_sel = seq // NSA_SEL_BLOCK
    tq = ATT_TQ
    assert seq % tq == 0 and seq // NSA_CMP_STRIDE == LANES and n_sel % SUBLANES == 0 and n_sel <= LANES
    assert NSA_SEL_TOPN <= n_sel
    nq, nkv = hh * hd, gg * hd
    w_bf = w_in.astype(BF16)
    kv = lambda t: w_bf[:, nq + t * nkv:nq + (t + 1) * nkv]
    n_gate = 3 * hh
    w_perm = jnp.concatenate([w_bf[:, :nq], kv(2), kv(4), kv(0), kv(1),
                              w_bf[:, nq + 6 * nkv:], jnp.zeros((d, LANES - n_gate), BF16)], axis=1)
    n_main = nq + 2 * nkv
    col_scale = jnp.concatenate([jnp.full((nq,), hd ** -0.5 * LOG2E, F32),
                                 jnp.ones((w_perm.shape[1] - nq,), F32)]).reshape(1, -1)
    w_vt = jnp.concatenate([kv(3), kv(5)], axis=1).T
    main, v_t, tail = _project(x, mod, w_perm, col_scale, n_main, w_vt)
    slopes = _alibi_slopes(hh) * LOG2E
    gate_blk = 2 * nkv // LANES

    cmp_spec = pl.BlockSpec((None, None, LANES, hd), lambda b, w, g: (b, w * gg + g, 0, 0))
    cmp_kv, cmp_kv_t = pl.pallas_call(
        functools.partial(_nsa_compress_kernel, n_cmp=n_cmp),
        grid=(bsz, 2, gg),
        in_specs=[pl.BlockSpec((None, seq, hd), lambda b, w, g: (b, 0, w * gg + g)),
                  pl.BlockSpec((None, NSA_CMP_LEN, hd), lambda b, w, g: (w, 0, 0)),
                  pl.BlockSpec((None, NSA_CMP_LEN * hd, hd), lambda b, w, g: (w, 0, 0)),
                  pl.BlockSpec((None, hd, hd), lambda b, w, g: (w, 0, 0))],
        out_specs=[cmp_spec, cmp_spec],
        out_shape=[jax.ShapeDtypeStruct((bsz, 2 * gg, LANES, hd), BF16)] * 2,
        compiler_params=_cparams(3),
        name="nsa_compress",
    )(tail, cmp_pos, cmp_w1.astype(BF16), cmp_w2.astype(BF16))

    pos_cend = jnp.pad(pos_f[:, NSA_CMP_LEN - 1::NSA_CMP_STRIDE], ((0, 0), (0, LANES - n_cmp)))
    pos_cend = pos_cend.reshape(bsz, LANES, 1)
    c_start = jnp.arange(LANES) * NSA_CMP_STRIDE
    j_start = jnp.arange(n_sel) * NSA_SEL_BLOCK
    overlap_t = ((c_start[None, :] < j_start[:, None] + NSA_SEL_BLOCK)
                 & (c_start[None, :] + NSA_CMP_LEN - 1 >= j_start[:, None])
                 & (jnp.arange(LANES)[None, :] < n_cmp)).astype(BF16)

    smem = pl.BlockSpec(memory_space=pltpu.SMEM)
    q_spec = pl.BlockSpec((None, tq, nq), lambda b, i: (b, i, 0))
    gate_spec = pl.BlockSpec((None, tq, LANES), lambda b, i: (b, i, gate_blk))
    pq_spec = pl.BlockSpec((None, None, 1, tq), lambda b, i: (b, i, 0, 0))
    sel_spec = pl.BlockSpec((None, gg, tq, LANES), lambda b, i: (b, 0, i, 0))
    cmp_all = pl.BlockSpec((None, 2 * gg, LANES, hd), lambda b, i: (b, 0, 0, 0))
    pos_q = pos_f.reshape(bsz, seq // tq, 1, tq)
    grid = (bsz, seq // tq)

    acc0, sel_bias = pl.pallas_call(
        functools.partial(_nsa_cmp_select_kernel, n_cmp=n_cmp, n_sel=n_sel),
        grid=grid,
        in_specs=[smem, q_spec, cmp_all, cmp_all, gate_spec, pq_spec,
                  pl.BlockSpec((None, LANES, 1), lambda b, i: (b, 0, 0)),
                  pl.BlockSpec((n_sel, LANES), lambda b, i: (0, 0))],
        out_specs=[q_spec, sel_spec],
        out_shape=[jax.ShapeDtypeStruct((bsz, seq, nq), F32),
                   jax.ShapeDtypeStruct((bsz, gg, seq, LANES), BF16)],
        compiler_params=_cparams(2),
        name="nsa_compressed_attention_select",
    )(slopes, main, cmp_kv, cmp_kv_t, tail, pos_q, pos_cend, overlap_t)

    return pl.pallas_call(
        _nsa_select_window_kernel,
        grid=grid,
        in_specs=[smem, q_spec, sel_spec,
                  pl.BlockSpec((None, seq, 2 * nkv), lambda b, i: (b, 0, nq // (2 * nkv))),
                  pl.BlockSpec((None, seq // ATT_TK, 2 * nkv, ATT_TK), lambda b, i: (b, 0, 0, 0)),
                  gate_spec,
                  pl.BlockSpec((None, seq, 1), lambda b, i: (b, 0, 0)),
                  pq_spec, q_spec],
        out_specs=q_spec,
        out_shape=jax.ShapeDtypeStruct((bsz, seq, nq), BF16),
        compiler_params=_cparams(2),
        name="nsa_selected_window_attention",
    )(slopes, main, sel_bias, main, v_t, tail, pos_col, pos_q, acc0)


def kernel(x, c, positions, mod_w, mod_b, ln_g, ln_b, ffn_w_in, ffn_conv_w, ffn_conv_b, ffn_w_out,
           mla_w_in, mla_q_norm, mla_w_uq, mla_kv_norm, mla_w_ukv, mla_w_o,
           moba_w_in, moba_w_o, nsa_w_in, nsa_cmp_pos, nsa_cmp_w1, nsa_cmp_w2, nsa_w_o,
           sb_w_in, sb_w_o):
    bsz, seq, d = x.shape
    depth = mod_w.shape[0]
    assert d == D_MODEL and seq % TOKEN_TILE == 0 and seq % ATT_TQ == 0
    alpha = float((2 * depth) ** 0.25)
    pos_f = positions.astype(F32)
    pos_col = pos_f.reshape(bsz, seq, 1)
    mod_all = _modulation(c, mod_w, mod_b)
    ffn_w_in_bf, ffn_w_out_bf = ffn_w_in.astype(BF16), ffn_w_out.astype(BF16)
    for i in range(depth):
        kind, j = i % N_MIXERS, i // N_MIXERS
        mod = mod_all[i]
        if kind == 0:
            o = _mla_mixer(x, mod, pos_col, mla_w_in[j], mla_q_norm[j], mla_w_uq[j], mla_kv_norm[j],
                           mla_w_ukv[j])
            w_o = mla_w_o[j]
        elif kind == 1:
            o = _moba_mixer(x, mod, pos_col, pos_f, moba_w_in[j])
            w_o = moba_w_o[j]
        elif kind == 2:
            o = _nsa_mixer(x, mod, pos_col, pos_f, nsa_w_in[j], nsa_cmp_pos[j], nsa_cmp_w1[j],
                           nsa_cmp_w2[j])
            w_o = nsa_w_o[j]
        else:
            o = _sb_mixer(x, mod, sb_w_in[j])
            w_o = sb_w_o[j]
        x = _mixer_out_ffn(o, w_o.astype(BF16), x, mod, ln_g[i], ln_b[i], ffn_w_in_bf, ffn_conv_w[i],
                           ffn_conv_b[i], ffn_w_out_bf, i, alpha)
    return x
```

```python
import functools

import jax
import jax.numpy as jnp
from jax import lax
from jax.experimental import pallas as pl
from jax.experimental.pallas import tpu as pltpu

F32 = jnp.float32
BF16 = jnp.bfloat16

D_MODEL = 1024
N_MIXERS = 4
MLA_HEADS, MLA_NOPE, MLA_ROPE, MLA_V = 8, 128, 64, 128
MLA_Q_RANK, MLA_KV_RANK = 256, 256
ROPE_BASE = 10000.0
MOBA_HEADS, MOBA_HD, MOBA_BLOCK, MOBA_TOPK = 8, 128, 256, 3
NSA_HEADS, NSA_GROUPS, NSA_HD = 8, 2, 128
NSA_R = NSA_HEADS // NSA_GROUPS
NSA_CMP_LEN, NSA_CMP_STRIDE, NSA_SEL_BLOCK, NSA_SEL_TOPN, NSA_WINDOW = 32, 16, 64, 16, 512
NSA_FORCE_BONUS = 100.0
SB_HEADS, SB_HD = 8, 128
D_FF = 2816
LN_EPS = 1e-5
RMS_EPS = 1e-6
NEG = -1e30
M_INIT = -1e29
LOW = -3e38
TINY = 1e-30
LOG2E = 1.4426950408889634
SB_DEAD_LOG2 = -150.0

LANES = 128
SUBLANES = 8
VMEM_LIMIT_BYTES = 56 * 1024 * 1024

TOKEN_TILE = 512
ATT_TQ = 256
ATT_TK = 256
HEADS_PER_STEP = 8
FF_CHUNK = 256
LN_ROW_BLOCKS = 2
PROJ_CHUNK = 512
MOD_COL_TILE = 3072


def _cparams(n_grid):
    return pltpu.CompilerParams(dimension_semantics=("arbitrary",) * n_grid,
                                vmem_limit_bytes=VMEM_LIMIT_BYTES)


def _dot(a, b):
    return jnp.dot(a, b, preferred_element_type=F32)


def _dot_nt(a, b):
    return lax.dot_general(a, b, (((1,), (1,)), ((), ())), preferred_element_type=F32)


def _layer_norm(z, g, b):
    mu = jnp.mean(z, axis=-1, keepdims=True)
    d = z - mu
    var = jnp.mean(d * d, axis=-1, keepdims=True)
    return d * lax.rsqrt(var + LN_EPS) * g + b


def _rms_norm(z, g):
    return z * lax.rsqrt(jnp.mean(z * z, axis=-1, keepdims=True) + RMS_EPS) * g


def _gelu_tanh(x):
    return 0.5 * x * (1.0 + jnp.tanh(0.7978845608028654 * (x + 0.044715 * (x * x * x))))


def _softmax_stats_t(s_t, m):
    m_new = jnp.maximum(m, jnp.max(s_t, axis=0, keepdims=True))
    return m_new, jnp.exp2(m - m_new), jnp.exp2((s_t - m_new).astype(BF16))


def _with_ones_rows(v_t):
    return jnp.concatenate([v_t, jnp.ones((2 * SUBLANES, v_t.shape[1]), v_t.dtype)], axis=0)


def _normalize_t(acc_t, hd):
    return acc_t[:hd, :] / jnp.maximum(acc_t[hd:hd + 1, :], TINY)


def _mod_kernel(c_ref, w_ref, b_ref, o_ref):
    c = c_ref[...]
    c_act = c * jax.nn.sigmoid(c)
    o_ref[...] = jnp.dot(c_act, w_ref[...], preferred_element_type=F32,
                         precision=lax.Precision.HIGHEST) + b_ref[...]


def _modulation(c, mod_w, mod_b):
    depth, d, n = mod_w.shape
    bsz = c.shape[0]
    tn = MOD_COL_TILE
    out = pl.pallas_call(
        _mod_kernel,
        grid=(depth, n // tn),
        in_specs=[pl.BlockSpec((bsz, d), lambda l, j: (0, 0)),
                  pl.BlockSpec((None, d, tn), lambda l, j: (l, 0, j)),
                  pl.BlockSpec((None, 1, tn), lambda l, j: (l, 0, j))],
        out_specs=pl.BlockSpec((None, bsz, tn), lambda l, j: (l, 0, j)),
        out_shape=jax.ShapeDtypeStruct((depth, bsz, n), F32),
        compiler_params=_cparams(2),
        name="modulation",
    )(c, mod_w, mod_b.reshape(depth, 1, n))
    return out.reshape(depth, bsz, 6, d)


def _proj_kernel(x_ref, mod_ref, w_ref, cs_ref, wt_ref, *out_refs, n_main, n_total):
    h = (x_ref[...] * (1.0 + mod_ref[1:2, :]) + mod_ref[0:1, :]).astype(BF16)
    for n0 in range(0, n_total, PROJ_CHUNK):
        n1 = min(n0 + PROJ_CHUNK, n_total)
        y = _dot(h, w_ref[:, n0:n1]) * cs_ref[:, n0:n1]
        if n0 < n_main:
            out_refs[0][:, n0:n1] = y.astype(BF16)
        else:
            out_refs[2][:, n0 - n_main:n1 - n_main] = y
    vt_ref = out_refs[1]
    for n0 in range(0, wt_ref.shape[0], PROJ_CHUNK):
        y_t = _dot_nt(wt_ref[n0:n0 + PROJ_CHUNK, :], h).astype(BF16)
        for t in range(vt_ref.shape[0]):
            vt_ref[t, n0:n0 + PROJ_CHUNK, :] = y_t[:, t * ATT_TK:(t + 1) * ATT_TK]


def _project(x, mod, w, col_scale, n_main, w_t):
    bsz, seq, d = x.shape
    n_total, n_t = w.shape[1], w_t.shape[0]
    assert n_main % PROJ_CHUNK == 0 and n_total % LANES == 0 and n_t % PROJ_CHUNK == 0
    tm = TOKEN_TILE
    out_shape = [jax.ShapeDtypeStruct((bsz, seq, n_main), BF16),
                 jax.ShapeDtypeStruct((bsz, seq // ATT_TK, n_t, ATT_TK), BF16)]
    out_specs = [pl.BlockSpec((None, tm, n_main), lambda b, i: (b, i, 0)),
                 pl.BlockSpec((None, tm // ATT_TK, n_t, ATT_TK), lambda b, i: (b, i, 0, 0))]
    if n_main < n_total:
        out_shape.append(jax.ShapeDtypeStruct((bsz, seq, n_total - n_main), F32))
        out_specs.append(pl.BlockSpec((None, tm, n_total - n_main), lambda b, i: (b, i, 0)))
    return pl.pallas_call(
        functools.partial(_proj_kernel, n_main=n_main, n_total=n_total),
        grid=(bsz, seq // tm),
        in_specs=[pl.BlockSpec((None, tm, d), lambda b, i: (b, i, 0)),
                  pl.BlockSpec((None, 6, d), lambda b, i: (b, 0, 0)),
                  pl.BlockSpec((d, n_total), lambda b, i: (0, 0)),
                  pl.BlockSpec((1, n_total), lambda b, i: (0, 0)),
                  pl.BlockSpec((n_t, d), lambda b, i: (0, 0))],
        out_specs=out_specs,
        out_shape=out_shape,
        compiler_params=_cparams(2),
        name="modulate_project",
    )(x, mod, w, col_scale, w_t)


def _mixer_out_ffn_kernel(o_ref, wmix_ref, x_ref, mod_ref, g1_ref, b1_ref, win_ref, cw_ref, cb_ref,
                          wo_ref, g2_ref, b2_ref, out_ref, a_scr, g_scr, halo_scr, *, alpha):
    tm = x_ref.shape[0]
    halo = halo_scr.shape[0]
    rb = tm // LN_ROW_BLOCKS
    x1 = jnp.concatenate(
        [_layer_norm(alpha * x_ref[r0:r0 + rb, :] + mod_ref[2:3, :] * _dot(o_ref[r0:r0 + rb, :], wmix_ref[...]),
                     g1_ref[...], b1_ref[...]) for r0 in range(0, tm, rb)], axis=0)
    shift, scale, gate = mod_ref[3:4, :], mod_ref[4:5, :], mod_ref[5:6, :]
    h = (x1 * (1.0 + scale) + shift).astype(BF16)

    @pl.when(pl.program_id(1) == 0)
    def _():
        halo_scr[...] = jnp.zeros_like(halo_scr)

    h_ext = jnp.concatenate([halo_scr[...], h], axis=0)
    halo_scr[...] = h[tm - halo:, :]
    chunks = [(c0, min(c0 + FF_CHUNK, D_FF)) for c0 in range(0, D_FF, FF_CHUNK)]

    def matmuls(i):
        c0, c1 = chunks[i]
        a_scr[i % 2, :, :c1 - c0] = _dot(h_ext, win_ref[:, c0:c1])
        return _dot(h, win_ref[:, D_FF + c0:D_FF + c1])

    b_next = matmuls(0)
    for i, (c0, c1) in enumerate(chunks):
        b = b_next
        if i + 1 < len(chunks):
            b_next = matmuls(i + 1)
        w = c1 - c0
        a = a_scr[i % 2, halo:halo + tm, :w]
        a_m1 = a_scr[i % 2, halo - 1:halo - 1 + tm, :w]
        a_m2 = a_scr[i % 2, halo - 2:halo - 2 + tm, :w]
        conv = (a * cw_ref[2:3, c0:c1] + a_m1 * cw_ref[1:2, c0:c1] + a_m2 * cw_ref[0:1, c0:c1]
                + cb_ref[:, c0:c1])
        g_scr[:, c0:c1] = (_gelu_tanh(conv) * b).astype(BF16)
    for r0 in range(0, tm, rb):
        y = _dot(g_scr[r0:r0 + rb, :], wo_ref[...])
        out_ref[r0:r0 + rb, :] = _layer_norm(alpha * x1[r0:r0 + rb, :] + gate * y, g2_ref[...], b2_ref[...])


def _mixer_out_ffn(o, w_mix, x, mod, ln_g, ln_b, w_in_all, conv_w, conv_b, w_out_all, layer, alpha):
    bsz, seq, d = x.shape
    k = o.shape[-1]
    tm = TOKEN_TILE
    halo = 2 * SUBLANES
    const = lambda shape: pl.BlockSpec(shape, lambda b, i: (0, 0), pipeline_mode=pl.Buffered(1))
    of_layer = lambda shape: pl.BlockSpec((None,) + shape, lambda b, i: (layer, 0, 0),
                                          pipeline_mode=pl.Buffered(1))
    return pl.pallas_call(
        functools.partial(_mixer_out_ffn_kernel, alpha=alpha),
        grid=(bsz, seq // tm),
        in_specs=[pl.BlockSpec((None, tm, k), lambda b, i: (b, i, 0)),
                  const((k, d)),
                  pl.BlockSpec((None, tm, d), lambda b, i: (b, i, 0)),
                  pl.BlockSpec((None, 6, d), lambda b, i: (b, 0, 0)),
                  const((1, d)), const((1, d)),
                  of_layer((d, 2 * D_FF)), const((3, D_FF)), const((1, D_FF)),
                  of_layer((D_FF, d)), const((1, d)), const((1, d))],
        out_specs=pl.BlockSpec((None, tm, d), lambda b, i: (b, i, 0)),
        out_shape=jax.ShapeDtypeStruct((bsz, seq, d), F32),
        scratch_shapes=[pltpu.VMEM((2, halo + tm, FF_CHUNK), F32), pltpu.VMEM((tm, D_FF), BF16),
                        pltpu.VMEM((halo, d), BF16)],
        compiler_params=_cparams(2),
        name="mixer_out_conv_ffn",
    )(o, w_mix, x, mod, ln_g[0].reshape(1, d), ln_b[0].reshape(1, d), w_in_all, conv_w,
      conv_b.reshape(1, D_FF), w_out_all, ln_g[1].reshape(1, d), ln_b[1].reshape(1, d))


def _rope_table_kernel(pos_ref, invf_ref, cos_ref, sin_ref):
    ang = pos_ref[...] * invf_ref[...]
    cos_ref[...] = jnp.cos(ang)
    sin_ref[...] = jnp.sin(ang)


def _rope_tables(pos_f, half):
    bsz, seq = pos_f.shape
    per_row = LANES // half
    rows = bsz * seq // per_row
    tr = min(rows, TOKEN_TILE)
    assert LANES % half == 0 and (bsz * seq) % per_row == 0 and rows % tr == 0
    pos_d = jnp.repeat(pos_f.reshape(rows, per_row), half, axis=1)
    inv_freq = ROPE_BASE ** (-jnp.arange(half, dtype=F32) / half)
    invf = jnp.tile(inv_freq, per_row).reshape(1, LANES)
    row_spec = pl.BlockSpec((tr, LANES), lambda i: (i, 0))
    cos_d, sin_d = pl.pallas_call(
        _rope_table_kernel,
        grid=(rows // tr,),
        in_specs=[row_spec, pl.BlockSpec((1, LANES), lambda i: (0, 0))],
        out_specs=[row_spec, row_spec],
        out_shape=[jax.ShapeDtypeStruct((rows, LANES), F32)] * 2,
        compiler_params=_cparams(1),
        name="rope_tables",
    )(pos_d, invf)
    return cos_d.reshape(bsz, seq, half), sin_d.reshape(bsz, seq, half)


def _mla_proj_kernel(x_ref, mod_ref, cos_ref, sin_ref, win_ref, qn_ref, kvn_ref, wuq_ref,
                     wuqr_ref, wuk_ref, wuvt_ref, q_ref, kn_ref, kr_ref, vt_ref, *, scale):
    h = (x_ref[...] * (1.0 + mod_ref[1:2, :]) + mod_ref[0:1, :]).astype(BF16)
    proj = _dot(h, win_ref[...])
    c_q = _rms_norm(proj[:, :MLA_Q_RANK], qn_ref[...]).astype(BF16)
    c_kv = _rms_norm(proj[:, MLA_Q_RANK:MLA_Q_RANK + MLA_KV_RANK], kvn_ref[...]).astype(BF16)
    cos, sin = cos_ref[...], sin_ref[...]
    r0 = MLA_Q_RANK + MLA_KV_RANK
    kr_ref[...] = (proj[:, r0:r0 + LANES] * cos + proj[:, r0 + LANES:r0 + 2 * LANES] * sin).astype(BF16)
    for hd in range(MLA_HEADS):
        qa = _dot(c_q, wuq_ref[:, hd * 2 * LANES:(hd + 1) * 2 * LANES])
        qr = _dot(c_q, wuqr_ref[:, hd * LANES:(hd + 1) * LANES])
        q_ref[:, hd * 2 * LANES:hd * 2 * LANES + LANES] = (qa[:, :LANES] * scale).astype(BF16)
        q_ref[:, hd * 2 * LANES + LANES:(hd + 1) * 2 * LANES] = (
            (qa[:, LANES:] * cos + qr * sin) * scale).astype(BF16)
    kn_ref[...] = _dot(c_kv, wuk_ref[...]).astype(BF16)
    v_t = _dot_nt(wuvt_ref[...], c_kv).astype(BF16)
    for t in range(vt_ref.shape[0]):
        vt_ref[t] = v_t[:, t * ATT_TK:(t + 1) * ATT_TK]


def _mla_attn_kernel(q_ref, kn_ref, kr_ref, vt_ref, o_ref):
    qi = pl.program_id(2)
    tq = q_ref.shape[0]
    hp = HEADS_PER_STEP

    def step(j, carry, masked):
        ks = pl.multiple_of(j * ATT_TK, ATT_TK)
        kr = kr_ref[pl.ds(ks, ATT_TK), :]
        def scores(h):
            k = jnp.concatenate([kn_ref[pl.ds(ks, ATT_TK), h * MLA_NOPE:(h + 1) * MLA_NOPE], kr], axis=1)
            s_t = _dot_nt(k, q_ref[:, h * 2 * LANES:(h + 1) * 2 * LANES])
            if masked:
                key = lax.broadcasted_iota(jnp.int32, s_t.shape, 0)
                qry = lax.broadcasted_iota(jnp.int32, s_t.shape, 1)
                s_t = jnp.where(key <= qry, s_t, NEG)
            return s_t

        s_all = [scores(h) for h in range(hp)]
        stats = [_softmax_stats_t(s_all[h], carry[h][0]) for h in range(hp)]
        out = []
        for h in range(hp):
            m_new, alpha, p = stats[h]
            v_t = _with_ones_rows(vt_ref[j, h * MLA_V:(h + 1) * MLA_V, :])
            out.append((m_new, alpha * carry[h][1] + _dot(v_t, p)))
        return tuple(out)

    init = tuple((jnp.full((1, tq), M_INIT, F32), jnp.zeros((MLA_V + 2 * SUBLANES, tq), F32))
                 for _ in range(hp))
    carry = lax.fori_loop(0, qi, lambda j, c: step(j, c, False), init)
    carry = step(qi, carry, True)
    for h in range(hp):
        o_ref[:, h * MLA_V:(h + 1) * MLA_V] = _normalize_t(carry[h][1], MLA_V).T.astype(BF16)


def _mla_mixer(x, mod, pos_f, w_in, q_norm, w_uq, kv_norm, w_ukv):
    bsz, seq, d = x.shape
    hh, dn, dr, dv = MLA_HEADS, MLA_NOPE, MLA_ROPE, MLA_V
    half = dr // 2
    r0 = MLA_Q_RANK + MLA_KV_RANK
    w_in, w_uq, w_ukv = w_in.astype(BF16), w_uq.astype(BF16), w_ukv.astype(BF16)
    zpad = lambda rows, n: jnp.zeros((rows, n), BF16)
    x1, x2 = w_in[:, r0:r0 + half], w_in[:, r0 + half:r0 + dr]
    w_in_ext = jnp.concatenate([w_in[:, :r0], x1, x2, zpad(d, LANES - dr),
                                -x2, x1, zpad(d, LANES - dr)], axis=1)
    wq = w_uq.reshape(MLA_Q_RANK, hh, dn + dr)
    q1, q2 = wq[:, :, dn:dn + half], wq[:, :, dn + half:]
    zq = jnp.zeros((MLA_Q_RANK, hh, LANES - dr), BF16)
    w_uq_main = jnp.concatenate([wq[:, :, :dn], q1, q2, zq], axis=2).reshape(MLA_Q_RANK, hh * 2 * LANES)
    w_uq_rot = jnp.concatenate([-q2, q1, zq], axis=2).reshape(MLA_Q_RANK, hh * LANES)
    wkv = w_ukv.reshape(MLA_KV_RANK, hh, dn + dv)
    w_uk = wkv[:, :, :dn].reshape(MLA_KV_RANK, hh * dn)
    w_uv_t = wkv[:, :, dn:].reshape(MLA_KV_RANK, hh * dv).T
    cos, sin = _rope_tables(pos_f, half)
    lane_pad = lambda t: jnp.concatenate([t, t, jnp.zeros((bsz, seq, LANES - dr), F32)], axis=-1)
    cos128, sin128 = lane_pad(cos), lane_pad(sin)

    tm = TOKEN_TILE
    const = lambda b, i: (0, 0)
    tok = lambda n: pl.BlockSpec((None, tm, n), lambda b, i: (b, i, 0))
    q, kn, kr, v_t = pl.pallas_call(
        functools.partial(_mla_proj_kernel, scale=float((dn + dr) ** -0.5) * LOG2E),
        grid=(bsz, seq // tm),
        in_specs=[tok(d),
                  pl.BlockSpec((None, 6, d), lambda b, i: (b, 0, 0)),
                  tok(LANES),
                  tok(LANES),
                  pl.BlockSpec(w_in_ext.shape, const),
                  pl.BlockSpec((1, MLA_Q_RANK), const),
                  pl.BlockSpec((1, MLA_KV_RANK), const),
                  pl.BlockSpec(w_uq_main.shape, const),
                  pl.BlockSpec(w_uq_rot.shape, const),
                  pl.BlockSpec(w_uk.shape, const),
                  pl.BlockSpec(w_uv_t.shape, const)],
        out_specs=[tok(hh * 2 * LANES), tok(hh * dn), tok(LANES),
                   pl.BlockSpec((None, tm // ATT_TK, hh * dv, ATT_TK), lambda b, i: (b, i, 0, 0))],
        out_shape=[jax.ShapeDtypeStruct((bsz, seq, hh * 2 * LANES), BF16),
                   jax.ShapeDtypeStruct((bsz, seq, hh * dn), BF16),
                   jax.ShapeDtypeStruct((bsz, seq, LANES), BF16),
                   jax.ShapeDtypeStruct((bsz, seq // ATT_TK, hh * dv, ATT_TK), BF16)],
        compiler_params=_cparams(2),
        name="mla_project",
    )(x, mod, cos128, sin128, w_in_ext, q_norm.reshape(1, -1), kv_norm.reshape(1, -1),
      w_uq_main, w_uq_rot, w_uk, w_uv_t)

    tq, hp = ATT_TQ, HEADS_PER_STEP
    return pl.pallas_call(
        _mla_attn_kernel,
        grid=(bsz, hh // hp, seq // tq),
        in_specs=[pl.BlockSpec((None, tq, hp * 2 * LANES), lambda b, h, i: (b, i, h)),
                  pl.BlockSpec((None, seq, hp * dn), lambda b, h, i: (b, 0, h)),
                  pl.BlockSpec((None, seq, LANES), lambda b, h, i: (b, 0, 0)),
                  pl.BlockSpec((None, seq // ATT_TK, hp * dv, ATT_TK), lambda b, h, i: (b, 0, h, 0))],
        out_specs=pl.BlockSpec((None, tq, hp * dv), lambda b, h, i: (b, i, h)),
        out_shape=jax.ShapeDtypeStruct((bsz, seq, hh * dv), BF16),
        compiler_params=_cparams(3),
        name="mla_attention",
    )(q, kn, kr, v_t)


def _moba_kernel(slope_ref, q_ref, k_ref, vt_ref, pk_ref, pq_ref, o_ref, kmean_ref, *, n_blocks):
    blk, hp, hd = MOBA_BLOCK, HEADS_PER_STEP, MOBA_HD
    hg = pl.program_id(1)
    qi = pl.program_id(2)

    @pl.when(qi == 0)
    def _():
        kmean_ref[...] = jnp.zeros_like(kmean_ref)
        for h in range(hp):
            for n in range(n_blocks):
                kmean_ref[h, n:n + 1, :] = jnp.mean(
                    k_ref[n * blk:(n + 1) * blk, h * hd:(h + 1) * hd].astype(F32), axis=0, keepdims=True)

    lane = lax.broadcasted_iota(jnp.int32, (blk, LANES), 1)
    nb_pad = -(-n_blocks // SUBLANES) * SUBLANES
    blk_row = lax.broadcasted_iota(jnp.int32, (nb_pad, blk), 0)
    past = blk_row < qi
    q_aug = []
    for h in range(hp):
        q = q_ref[:, h * hd:(h + 1) * hd]
        km = kmean_ref[h]
        km_hi = km.astype(BF16)
        km_lo = (km - km_hi.astype(F32)).astype(BF16)
        gate_t = (_dot_nt(km_hi, q) + _dot_nt(km_lo, q))[:nb_pad, :]
        g = jnp.where(past, gate_t, LOW)
        rank = jnp.zeros(g.shape, jnp.int32)
        for dlt in range(1, nb_pad):
            other = pltpu.roll(g, dlt, axis=0)
            rank = rank + jnp.where(blk_row >= dlt, (other >= g).astype(jnp.int32),
                                    (other > g).astype(jnp.int32))
        chosen_t = jnp.logical_or(jnp.logical_and(rank < MOBA_TOPK, past), blk_row == qi)
        bias_t = jnp.concatenate([jnp.where(chosen_t, 0.0, NEG), jnp.full((LANES - nb_pad, blk), NEG, F32)],
                                 axis=0)
        q_aug.append(jnp.concatenate([q, bias_t.T.astype(BF16)], axis=1))

    pq = pq_ref[...]

    def step(n, carry, masked):
        ks = pl.multiple_of(n * blk, blk)
        onehot = jnp.where(lane == n, 1.0, 0.0).astype(BF16)
        dist_t = jnp.abs(pk_ref[pl.ds(ks, blk), :] - pq)
        s_all = []
        for h in range(hp):
            k_aug = jnp.concatenate([k_ref[pl.ds(ks, blk), h * hd:(h + 1) * hd], onehot], axis=1)
            s_t = _dot_nt(k_aug, q_aug[h]) - slope_ref[hg * hp + h] * dist_t
            if masked:
                key = lax.broadcasted_iota(jnp.int32, s_t.shape, 0)
                qry = lax.broadcasted_iota(jnp.int32, s_t.shape, 1)
                s_t = jnp.where(key <= qry, s_t, NEG)
            s_all.append(s_t)
        stats = [_softmax_stats_t(s_all[h], carry[h][0]) for h in range(hp)]
        out = []
        for h in range(hp):
            m_new, alpha, p = stats[h]
            v_t = _with_ones_rows(vt_ref[n, h * hd:(h + 1) * hd, :])
            out.append((m_new, alpha * carry[h][1] + _dot(v_t, p)))
        return tuple(out)

    init = tuple((jnp.full((1, blk), M_INIT, F32), jnp.zeros((hd + 2 * SUBLANES, blk), F32))
                 for _ in range(hp))
    carry = lax.fori_loop(0, qi, lambda n, c: step(n, c, False), init)
    carry = step(qi, carry, True)
    for h in range(hp):
        o_ref[:, h * hd:(h + 1) * hd] = _normalize_t(carry[h][1], hd).T.astype(BF16)


def _alibi_slopes(n):
    return 2.0 ** (-8.0 * jnp.arange(1, n + 1, dtype=F32) / n)


def _moba_mixer(x, mod, pos_col, pos_f, w_in):
    bsz, seq, d = x.shape
    hh, hd, blk = MOBA_HEADS, MOBA_HD, MOBA_BLOCK
    assert seq % blk == 0 and seq // blk + MOBA_TOPK < LANES and blk == ATT_TK
    n_blocks = seq // blk
    hp = HEADS_PER_STEP
    nqk = 2 * hh * hd
    col_scale = jnp.concatenate([jnp.full((hh * hd,), hd ** -0.5 * LOG2E, F32),
                                 jnp.ones((hh * hd,), F32)]).reshape(1, -1)
    w_bf = w_in.astype(BF16)
    qk, v_t = _project(x, mod, w_bf[:, :nqk], col_scale, nqk, w_bf[:, nqk:].T)
    pos_row = pos_f.reshape(bsz, n_blocks, 1, blk)
    return pl.pallas_call(
        functools.partial(_moba_kernel, n_blocks=n_blocks),
        grid=(bsz, hh // hp, n_blocks),
        in_specs=[pl.BlockSpec(memory_space=pltpu.SMEM),
                  pl.BlockSpec((None, blk, hp * hd), lambda b, h, i: (b, i, h)),
                  pl.BlockSpec((None, seq, hp * hd), lambda b, h, i: (b, 0, hh // hp + h)),
                  pl.BlockSpec((None, n_blocks, hp * hd, blk), lambda b, h, i: (b, 0, h, 0)),
                  pl.BlockSpec((None, seq, 1), lambda b, h, i: (b, 0, 0)),
                  pl.BlockSpec((None, None, 1, blk), lambda b, h, i: (b, i, 0, 0))],
        out_specs=pl.BlockSpec((None, blk, hp * hd), lambda b, h, i: (b, i, h)),
        out_shape=jax.ShapeDtypeStruct((bsz, seq, hh * hd), BF16),
        scratch_shapes=[pltpu.VMEM((hp, LANES, hd), F32)],
        compiler_params=_cparams(3),
        name="moba_attention",
    )(_alibi_slopes(hh) * LOG2E, qk, qk, v_t, pos_col, pos_row)


def _sb_kernel(q_ref, k_ref, vt_ref, o_ref):
    qi = pl.program_id(2)
    tq, tk, hp, hd = q_ref.shape[0], ATT_TK, HEADS_PER_STEP, SB_HD
    rs = lax.broadcasted_iota(jnp.int32, (tk, tk), 0)
    cj = lax.broadcasted_iota(jnp.int32, (tk, tk), 1)
    after = jnp.where(cj > rs, 1.0, 0.0).astype(BF16)
    after2 = jnp.concatenate([after, after], axis=1)

    def step(j, carry, masked):
        ks = pl.multiple_of(j * tk, tk)
        z_all = [_dot_nt(k_ref[pl.ds(ks, tk), h * hd:(h + 1) * hd], q_ref[:, h * hd:(h + 1) * hd])
                 for h in range(hp)]
        if masked:
            key = lax.broadcasted_iota(jnp.int32, (tk, tq), 0)
            qry = lax.broadcasted_iota(jnp.int32, (tk, tq), 1)
            mask = key < qry
        parts = []
        for h in range(hp):
            z2 = z_all[h]
            log_beta = jnp.minimum(z2, 0.0) - jnp.log2(1.0 + jnp.exp2(-jnp.abs(z2)))
            log_1mb = log_beta - z2
            if masked:
                log_1mb = jnp.where(mask, log_1mb, 0.0)
            hi = log_1mb.astype(BF16)
            lo = (log_1mb - hi.astype(F32)).astype(BF16)
            parts.append((log_beta, log_1mb, jnp.concatenate([hi, lo], axis=0)))
        tails = [_dot(after2, parts[h][2]) for h in range(hp)]
        out = []
        for h in range(hp):
            c, acc_t = carry[h]
            log_beta, log_1mb = parts[h][0], parts[h][1]
            a = jnp.exp2(log_beta + tails[h] + c)
            if masked:
                a = jnp.where(mask, a, 0.0)
            acc_t = acc_t + _dot(vt_ref[j, h * hd:(h + 1) * hd, :], a.astype(BF16))
            out.append((c + jnp.sum(log_1mb, axis=0, keepdims=True), acc_t))
        return tuple(out)

    init = tuple((jnp.zeros((1, tq), F32), jnp.zeros((hd, tq), F32)) for _ in range(hp))
    carry = step(qi, init, True)

    def live(state):
        t, cr = state
        c_max = cr[0][0]
        for h in range(1, hp):
            c_max = jnp.maximum(c_max, cr[h][0])
        return jnp.logical_and(t < qi, jnp.max(c_max) > SB_DEAD_LOG2)

    _, carry = lax.while_loop(live, lambda st: (st[0] + 1, step(qi - 1 - st[0], st[1], False)),
                              (jnp.int32(0), carry))
    for h in range(hp):
        o_ref[:, h * hd:(h + 1) * hd] = carry[h][1].T.astype(BF16)


def _sb_mixer(x, mod, w_in):
    bsz, seq, d = x.shape
    hh, hd, hp = SB_HEADS, SB_HD, HEADS_PER_STEP
    nqk = 2 * hh * hd
    col_scale = jnp.concatenate([jnp.full((hh * hd,), hd ** -0.5 * LOG2E, F32),
                                 jnp.ones((hh * hd,), F32)]).reshape(1, -1)
    w_bf = w_in.astype(BF16)
    qk, v_t = _project(x, mod, w_bf[:, :nqk], col_scale, nqk, w_bf[:, nqk:].T)
    tq = ATT_TQ
    assert tq == ATT_TK
    return pl.pallas_call(
        _sb_kernel,
        grid=(bsz, hh // hp, seq // tq),
        in_specs=[pl.BlockSpec((None, tq, hp * hd), lambda b, h, i: (b, i, h)),
                  pl.BlockSpec((None, seq, hp * hd), lambda b, h, i: (b, 0, hh // hp + h)),
                  pl.BlockSpec((None, seq // ATT_TK, hp * hd, ATT_TK), lambda b, h, i: (b, 0, h, 0))],
        out_specs=pl.BlockSpec((None, tq, hp * hd), lambda b, h, i: (b, i, h)),
        out_shape=jax.ShapeDtypeStruct((bsz, seq, hh * hd), BF16),
        compiler_params=_cparams(3),
        name="stick_breaking_attention",
    )(qk, qk, v_t)


def _nsa_compress_kernel(t_ref, pe_ref, w1_ref, w2_ref, o_ref, ot_ref, *, n_cmp):
    st, half = NSA_CMP_STRIDE, NSA_CMP_LEN // NSA_CMP_STRIDE
    assert half == 2
    rows = t_ref.shape[0] // st
    xa, xb = [], []
    for r in range(st):
        xr = t_ref[pl.ds(r, rows, stride=st), :]
        xa.append((xr + pe_ref[r:r + 1, :]).astype(BF16))
        xb.append((xr + pe_ref[st + r:st + r + 1, :]).astype(BF16))
    k_half = st * NSA_HD
    pre_a = _dot(jnp.concatenate(xa, axis=1), w1_ref[:k_half, :])
    pre_b = _dot(jnp.concatenate(xb, axis=1), w1_ref[k_half:, :])
    pre = pre_a + pltpu.roll(pre_b, rows - 1, axis=0)
    out = _dot(_gelu_tanh(pre).astype(BF16), w2_ref[...])
    row = lax.broadcasted_iota(jnp.int32, out.shape, 0)
    out = jnp.where(row < n_cmp, out, 0.0)
    o_ref[...] = out.astype(BF16)
    ot_ref[...] = out.T.astype(BF16)


def _nsa_cmp_select_kernel(slope_ref, q_ref, kc_ref, vct_ref, gt_ref, pq_ref, pe_ref, ovlt_ref,
                           acc_ref, sel_ref, *, n_cmp, n_sel):
    qi = pl.program_id(1)
    tq, gg, rr, hd = q_ref.shape[0], NSA_GROUPS, NSA_R, NSA_HD
    shape = (LANES, tq)
    n_row = lax.broadcasted_iota(jnp.int32, shape, 0)
    qidx = qi * tq + lax.broadcasted_iota(jnp.int32, shape, 1)
    c_end = n_row * NSA_CMP_STRIDE + (NSA_CMP_LEN - 1)
    mask_c = jnp.logical_and(c_end <= qidx, n_row < n_cmp)
    dist_t = jnp.abs(pe_ref[...] - pq_ref[...])
    sig_t = jax.nn.sigmoid(gt_ref[...]).T
    ovl_t = ovlt_ref[...]

    heads = [(g, r) for g in range(gg) for r in range(rr)]
    s_all = [_dot_nt(kc_ref[g], q_ref[:, (g * rr + r) * hd:(g * rr + r + 1) * hd])
             - slope_ref[g * rr + r] * dist_t for g, r in heads]
    p_all = []
    for s_t in s_all:
        s_t = jnp.where(mask_c, s_t, NEG)
        p = jnp.where(mask_c, jnp.exp2(s_t - jnp.max(s_t, axis=0, keepdims=True)), 0.0)
        p_all.append((p / jnp.maximum(jnp.sum(p, axis=0, keepdims=True), TINY)).astype(BF16))
    imp_t = [jnp.zeros((n_sel, tq), F32) for _ in range(gg)]
    for (g, r), p in zip(heads, p_all):
        imp_t[g] = imp_t[g] + _dot(ovl_t, p)
        c = g * rr + r
        acc_ref[:, c * hd:(c + 1) * hd] = (sig_t[c:c + 1, :] * _dot(vct_ref[gg + g], p)).T

    sshape = (n_sel, tq)
    j_row = lax.broadcasted_iota(jnp.int32, sshape, 0)
    q_blk = (qi * tq + lax.broadcasted_iota(jnp.int32, sshape, 1)) >> 6
    assert NSA_SEL_BLOCK == 64
    forced = jnp.logical_or(j_row == 0, jnp.logical_or(j_row == q_blk, j_row == q_blk - 1))
    cand = j_row <= q_blk
    bonus = jnp.where(forced, NSA_FORCE_BONUS, 0.0)
    for g in range(gg):
        score = jnp.where(cand, imp_t[g] + bonus, NEG)
        rank = jnp.zeros(sshape, jnp.int32)
        for dlt in range(1, n_sel):
            other = pltpu.roll(score, dlt, axis=0)
            rank = rank + jnp.where(j_row >= dlt, (other >= score).astype(jnp.int32),
                                    (other > score).astype(jnp.int32))
        chosen_t = jnp.logical_and(rank < NSA_SEL_TOPN, cand)
        bias_t = jnp.concatenate([jnp.where(chosen_t, 0.0, NEG), jnp.full((LANES - n_sel, tq), NEG, F32)],
                                 axis=0)
        sel_ref[g] = bias_t.T.astype(BF16)


def _nsa_select_window_kernel(slope_ref, q_ref, sel_ref, k_ref, vt_ref, gt_ref, pk_ref, pq_ref, prev_ref,
                              out_ref):
    qi = pl.program_id(1)
    tq, tk, hd, gg, rr = q_ref.shape[0], ATT_TK, NSA_HD, NSA_GROUPS, NSA_R
    assert tq == tk and NSA_WINDOW % tk == 0
    lanes = rr * tq
    pq4 = jnp.concatenate([pq_ref[...]] * rr, axis=1)
    qidx = qi * tq + (lax.broadcasted_iota(jnp.int32, (tk, lanes), 1) & (tq - 1))
    krow = lax.broadcasted_iota(jnp.int32, (tk, lanes), 0)
    klane = lax.broadcasted_iota(jnp.int32, (tk, LANES), 1)
    kblk = lax.broadcasted_iota(jnp.int32, (tk, LANES), 0) >> 6
    per_tile = tk // NSA_SEL_BLOCK
    slope_rows = [jnp.concatenate([jnp.full((1, tq), slope_ref[g * rr + r], F32) for r in range(rr)], axis=1)
                  for g in range(gg)]
    q4 = [jnp.concatenate([q_ref[:, (g * rr + r) * hd:(g * rr + r + 1) * hd] for r in range(rr)], axis=0)
          for g in range(gg)]
    q_aug = [jnp.concatenate([q4[g], jnp.concatenate([sel_ref[g]] * rr, axis=0)], axis=1) for g in range(gg)]
    k_win0 = gg * hd

    def update(s_all, carry, v_row0, j):
        stats = [_softmax_stats_t(s_all[g], carry[g][0]) for g in range(gg)]
        out = []
        for g in range(gg):
            m_new, alpha, p = stats[g]
            v_t = _with_ones_rows(vt_ref[j, v_row0 + g * hd:v_row0 + (g + 1) * hd, :])
            out.append((m_new, alpha * carry[g][1] + _dot(v_t, p)))
        return tuple(out)

    def select_step(j, carry, masked):
        ks = pl.multiple_of(j * tk, tk)
        dist = jnp.abs(pk_ref[pl.ds(ks, tk), :] - pq4)
        onehot = jnp.where(klane == j * per_tile + kblk, 1.0, 0.0).astype(BF16)
        s_all = []
        for g in range(gg):
            k_aug = jnp.concatenate([k_ref[pl.ds(ks, tk), g * hd:(g + 1) * hd], onehot], axis=1)
            s_t = _dot_nt(k_aug, q_aug[g]) - slope_rows[g] * dist
            if masked:
                s_t = jnp.where(j * tk + krow <= qidx, s_t, NEG)
            s_all.append(s_t)
        return update(s_all, carry, 0, j)

    def init():
        return tuple((jnp.full((1, lanes), M_INIT, F32), jnp.zeros((hd + 2 * SUBLANES, lanes), F32))
                     for _ in range(gg))

    sel = lax.fori_loop(0, qi, lambda j, c: select_step(j, c, False), init())
    sel = select_step(qi, sel, True)

    win = init()
    n_band = NSA_WINDOW // tk + 1
    qloc = qidx - qi * tq
    for t in range(n_band):
        jt = qi - (n_band - 1) + t
        jc = jnp.maximum(jt, 0)
        ks = pl.multiple_of(jc * tk, tk)
        none = jnp.where(jt >= 0, 0, tk)
        if t == n_band - 1:
            mask = krow <= qloc
        elif t == 0:
            mask = krow > qloc + none
        else:
            mask = krow >= none
        dist = jnp.abs(pk_ref[pl.ds(ks, tk), :] - pq4)
        s_all = []
        for g in range(gg):
            s_t = _dot_nt(k_ref[pl.ds(ks, tk), k_win0 + g * hd:k_win0 + (g + 1) * hd], q4[g])
            s_all.append(jnp.where(mask, s_t - slope_rows[g] * dist, NEG))
        win = update(s_all, win, gg * hd, jc)

    sig_t = jax.nn.sigmoid(gt_ref[...]).T
    for g in range(gg):
        gate = lambda branch: jnp.concatenate(
            [sig_t[branch * NSA_HEADS + g * rr + r:branch * NSA_HEADS + g * rr + r + 1, :] for r in range(rr)],
            axis=1)
        o_t = gate(1) * _normalize_t(sel[g][1], hd) + gate(2) * _normalize_t(win[g][1], hd)
        for r in range(rr):
            c = g * rr + r
            out_ref[:, c * hd:(c + 1) * hd] = (prev_ref[:, c * hd:(c + 1) * hd]
                                                + o_t[:, r * tq:(r + 1) * tq].T).astype(BF16)


def _nsa_mixer(x, mod, pos_col, pos_f, w_in, cmp_pos, cmp_w1, cmp_w2):
    bsz, seq, d = x.shape
    hh, gg, hd = NSA_HEADS, NSA_GROUPS, NSA_HD
    n_cmp = (seq - NSA_CMP_LEN) // NSA_CMP_STRIDE + 1
    n_sel = seq // NSA_SEL_BLOCK
    tq = ATT_TQ
    assert seq % tq == 0 and seq // NSA_CMP_STRIDE == LANES and n_sel % SUBLANES == 0 and n_sel <= LANES
    assert NSA_SEL_TOPN <= n_sel
    nq, nkv = hh * hd, gg * hd
    w_bf = w_in.astype(BF16)
    kv = lambda t: w_bf[:, nq + t * nkv:nq + (t + 1) * nkv]
    n_gate = 3 * hh
    w_perm = jnp.concatenate([w_bf[:, :nq], kv(2), kv(4), kv(0), kv(1),
                              w_bf[:, nq + 6 * nkv:], jnp.zeros((d, LANES - n_gate), BF16)], axis=1)
    n_main = nq + 2 * nkv
    col_scale = jnp.concatenate([jnp.full((nq,), hd ** -0.5 * LOG2E, F32),
                                 jnp.ones((w_perm.shape[1] - nq,), F32)]).reshape(1, -1)
    w_vt = jnp.concatenate([kv(3), kv(5)], axis=1).T
    main, v_t, tail = _project(x, mod, w_perm, col_scale, n_main, w_vt)
    slopes = _alibi_slopes(hh) * LOG2E
    gate_blk = 2 * nkv // LANES

    cmp_spec = pl.BlockSpec((None, None, LANES, hd), lambda b, w, g: (b, w * gg + g, 0, 0))
    cmp_kv, cmp_kv_t = pl.pallas_call(
        functools.partial(_nsa_compress_kernel, n_cmp=n_cmp),
        grid=(bsz, 2, gg),
        in_specs=[pl.BlockSpec((None, seq, hd), lambda b, w, g: (b, 0, w * gg + g)),
                  pl.BlockSpec((None, NSA_CMP_LEN, hd), lambda b, w, g: (w, 0, 0)),
                  pl.BlockSpec((None, NSA_CMP_LEN * hd, hd), lambda b, w, g: (w, 0, 0)),
                  pl.BlockSpec((None, hd, hd), lambda b, w, g: (w, 0, 0))],
        out_specs=[cmp_spec, cmp_spec],
        out_shape=[jax.ShapeDtypeStruct((bsz, 2 * gg, LANES, hd), BF16)] * 2,
        compiler_params=_cparams(3),
        name="nsa_compress",
    )(tail, cmp_pos, cmp_w1.astype(BF16), cmp_w2.astype(BF16))

    pos_cend = jnp.pad(pos_f[:, NSA_CMP_LEN - 1::NSA_CMP_STRIDE], ((0, 0), (0, LANES - n_cmp)))
    pos_cend = pos_cend.reshape(bsz, LANES, 1)
    c_start = jnp.arange(LANES) * NSA_CMP_STRIDE
    j_start = jnp.arange(n_sel) * NSA_SEL_BLOCK
    overlap_t = ((c_start[None, :] < j_start[:, None] + NSA_SEL_BLOCK)
                 & (c_start[None, :] + NSA_CMP_LEN - 1 >= j_start[:, None])
                 & (jnp.arange(LANES)[None, :] < n_cmp)).astype(BF16)

    smem = pl.BlockSpec(memory_space=pltpu.SMEM)
    q_spec = pl.BlockSpec((None, tq, nq), lambda b, i: (b, i, 0))
    gate_spec = pl.BlockSpec((None, tq, LANES), lambda b, i: (b, i, gate_blk))
    pq_spec = pl.BlockSpec((None, None, 1, tq), lambda b, i: (b, i, 0, 0))
    sel_spec = pl.BlockSpec((None, gg, tq, LANES), lambda b, i: (b, 0, i, 0))
    cmp_all = pl.BlockSpec((None, 2 * gg, LANES, hd), lambda b, i: (b, 0, 0, 0))
    pos_q = pos_f.reshape(bsz, seq // tq, 1, tq)
    grid = (bsz, seq // tq)

    acc0, sel_bias = pl.pallas_call(
        functools.partial(_nsa_cmp_select_kernel, n_cmp=n_cmp, n_sel=n_sel),
        grid=grid,
        in_specs=[smem, q_spec, cmp_all, cmp_all, gate_spec, pq_spec,
                  pl.BlockSpec((None, LANES, 1), lambda b, i: (b, 0, 0)),
                  pl.BlockSpec((n_sel, LANES), lambda b, i: (0, 0))],
        out_specs=[q_spec, sel_spec],
        out_shape=[jax.ShapeDtypeStruct((bsz, seq, nq), F32),
                   jax.ShapeDtypeStruct((bsz, gg, seq, LANES), BF16)],
        compiler_params=_cparams(2),
        name="nsa_compressed_attention_select",
    )(slopes, main, cmp_kv, cmp_kv_t, tail, pos_q, pos_cend, overlap_t)

    return pl.pallas_call(
        _nsa_select_window_kernel,
        grid=grid,
        in_specs=[smem, q_spec, sel_spec,
                  pl.BlockSpec((None, seq, 2 * nkv), lambda b, i: (b, 0, nq // (2 * nkv))),
                  pl.BlockSpec((None, seq // ATT_TK, 2 * nkv, ATT_TK), lambda b, i: (b, 0, 0, 0)),
                  gate_spec,
                  pl.BlockSpec((None, seq, 1), lambda b, i: (b, 0, 0)),
                  pq_spec, q_spec],
        out_specs=q_spec,
        out_shape=jax.ShapeDtypeStruct((bsz, seq, nq), BF16),
        compiler_params=_cparams(2),
        name="nsa_selected_window_attention",
    )(slopes, main, sel_bias, main, v_t, tail, pos_col, pos_q, acc0)


def kernel(x, c, positions, mod_w, mod_b, ln_g, ln_b, ffn_w_in, ffn_conv_w, ffn_conv_b, ffn_w_out,
           mla_w_in, mla_q_norm, mla_w_uq, mla_kv_norm, mla_w_ukv, mla_w_o,
           moba_w_in, moba_w_o, nsa_w_in, nsa_cmp_pos, nsa_cmp_w1, nsa_cmp_w2, nsa_w_o,
           sb_w_in, sb_w_o):
    bsz, seq, d = x.shape
    depth = mod_w.shape[0]
    assert d == D_MODEL and seq % TOKEN_TILE == 0 and seq % ATT_TQ == 0
    alpha = float((2 * depth) ** 0.25)
    pos_f = positions.astype(F32)
    pos_col = pos_f.reshape(bsz, seq, 1)
    mod_all = _modulation(c, mod_w, mod_b)
    ffn_w_in_bf, ffn_w_out_bf = ffn_w_in.astype(BF16), ffn_w_out.astype(BF16)
    for i in range(depth):
        kind, j = i % N_MIXERS, i // N_MIXERS
        mod = mod_all[i]
        if kind == 0:
            o = _mla_mixer(x, mod, pos_f, mla_w_in[j], mla_q_norm[j], mla_w_uq[j], mla_kv_norm[j],
                           mla_w_ukv[j])
            w_o = mla_w_o[j]
        elif kind == 1:
            o = _moba_mixer(x, mod, pos_col, pos_f, moba_w_in[j])
            w_o = moba_w_o[j]
        elif kind == 2:
            o = _nsa_mixer(x, mod, pos_col, pos_f, nsa_w_in[j], nsa_cmp_pos[j], nsa_cmp_w1[j],
                           nsa_cmp_w2[j])
            w_o = nsa_w_o[j]
        else:
            o = _sb_mixer(x, mod, sb_w_in[j])
            w_o = sb_w_o[j]
        x = _mixer_out_ffn(o, w_o.astype(BF16), x, mod, ln_g[i], ln_b[i], ffn_w_in_bf, ffn_conv_w[i],
                           ffn_conv_b[i], ffn_w_out_bf, i, alpha)
    return x
```

```python
import functools

import jax
import jax.numpy as jnp
from jax import lax
from jax.experimental import pallas as pl
from jax.experimental.pallas import tpu as pltpu

F32 = jnp.float32
BF16 = jnp.bfloat16

D_MODEL = 1024
N_MIXERS = 4
MLA_HEADS, MLA_NOPE, MLA_ROPE, MLA_V = 8, 128, 64, 128
MLA_Q_RANK, MLA_KV_RANK = 256, 256
ROPE_BASE = 10000.0
MOBA_HEADS, MOBA_HD, MOBA_BLOCK, MOBA_TOPK = 8, 128, 256, 3
NSA_HEADS, NSA_GROUPS, NSA_HD = 8, 2, 128
NSA_R = NSA_HEADS // NSA_GROUPS
NSA_CMP_LEN, NSA_CMP_STRIDE, NSA_SEL_BLOCK, NSA_SEL_TOPN, NSA_WINDOW = 32, 16, 64, 16, 512
NSA_FORCE_BONUS = 100.0
SB_HEADS, SB_HD = 8, 128
D_FF = 2816
LN_EPS = 1e-5
RMS_EPS = 1e-6
NEG = -1e30
M_INIT = -1e29
LOW = -3e38
TINY = 1e-30
LOG2E = 1.4426950408889634
SB_DEAD_LOG2 = -150.0

LANES = 128
SUBLANES = 8
VMEM_LIMIT_BYTES = 56 * 1024 * 1024

TOKEN_TILE = 512
ATT_TQ = 256
ATT_TK = 256
HEADS_PER_STEP = 8
FF_CHUNK = 256
LN_ROW_BLOCKS = 2
PROJ_CHUNK = 512
MOD_COL_TILE = 3072


def _cparams(n_grid):
    return pltpu.CompilerParams(dimension_semantics=("arbitrary",) * n_grid,
                                vmem_limit_bytes=VMEM_LIMIT_BYTES)


def _dot(a, b):
    return jnp.dot(a, b, preferred_element_type=F32)


def _dot_nt(a, b):
    return lax.dot_general(a, b, (((1,), (1,)), ((), ())), preferred_element_type=F32)


def _layer_norm(z, g, b):
    mu = jnp.mean(z, axis=-1, keepdims=True)
    d = z - mu
    var = jnp.mean(d * d, axis=-1, keepdims=True)
    return d * lax.rsqrt(var + LN_EPS) * g + b


def _rms_norm(z, g):
    return z * lax.rsqrt(jnp.mean(z * z, axis=-1, keepdims=True) + RMS_EPS) * g


def _gelu_tanh(x):
    return 0.5 * x * (1.0 + jnp.tanh(0.7978845608028654 * (x + 0.044715 * (x * x * x))))


def _softmax_stats_t(s_t, m):
    m_new = jnp.maximum(m, jnp.max(s_t, axis=0, keepdims=True))
    return m_new, jnp.exp2(m - m_new), jnp.exp2((s_t - m_new).astype(BF16))


def _with_ones_rows(v_t):
    return jnp.concatenate([v_t, jnp.ones((2 * SUBLANES, v_t.shape[1]), v_t.dtype)], axis=0)


def _normalize_t(acc_t, hd):
    return acc_t[:hd, :] / jnp.maximum(acc_t[hd:hd + 1, :], TINY)


def _mod_kernel(c_ref, w_ref, b_ref, o_ref):
    c = c_ref[...]
    c_act = c * jax.nn.sigmoid(c)
    rows = c.shape[0]
    c_hi = c_act.astype(BF16).astype(F32)
    c_mid = (c_act - c_hi).astype(BF16).astype(F32)
    c_lo = (c_act - c_hi - c_mid).astype(BF16).astype(F32)
    c3 = jnp.concatenate([c_hi, c_mid, c_lo], axis=0).astype(BF16)
    w = w_ref[...]
    w_hi = w.astype(BF16)
    w_lo = (w - w_hi.astype(F32)).astype(BF16)
    p = _dot(c3, w_hi) + _dot(c3, w_lo)
    o_ref[...] = p[:rows, :] + p[rows:2 * rows, :] + p[2 * rows:, :] + b_ref[...]


def _modulation(c, mod_w, mod_b):
    depth, d, n = mod_w.shape
    bsz = c.shape[0]
    tn = MOD_COL_TILE
    out = pl.pallas_call(
        _mod_kernel,
        grid=(depth, n // tn),
        in_specs=[pl.BlockSpec((bsz, d), lambda l, j: (0, 0)),
                  pl.BlockSpec((None, d, tn), lambda l, j: (l, 0, j)),
                  pl.BlockSpec((None, 1, tn), lambda l, j: (l, 0, j))],
        out_specs=pl.BlockSpec((None, bsz, tn), lambda l, j: (l, 0, j)),
        out_shape=jax.ShapeDtypeStruct((depth, bsz, n), F32),
        compiler_params=_cparams(2),
        name="modulation",
    )(c, mod_w, mod_b.reshape(depth, 1, n))
    return out.reshape(depth, bsz, 6, d)


def _proj_kernel(x_ref, mod_ref, w_ref, cs_ref, wt_ref, *out_refs, n_main, n_total):
    h = (x_ref[...] * (1.0 + mod_ref[1:2, :]) + mod_ref[0:1, :]).astype(BF16)
    for n0 in range(0, n_total, PROJ_CHUNK):
        n1 = min(n0 + PROJ_CHUNK, n_total)
        y = _dot(h, w_ref[:, n0:n1]) * cs_ref[:, n0:n1]
        if n0 < n_main:
            out_refs[0][:, n0:n1] = y.astype(BF16)
        else:
            out_refs[2][:, n0 - n_main:n1 - n_main] = y
    vt_ref = out_refs[1]
    for n0 in range(0, wt_ref.shape[0], PROJ_CHUNK):
        y_t = _dot_nt(wt_ref[n0:n0 + PROJ_CHUNK, :], h).astype(BF16)
        for t in range(vt_ref.shape[0]):
            vt_ref[t, n0:n0 + PROJ_CHUNK, :] = y_t[:, t * ATT_TK:(t + 1) * ATT_TK]


def _project(x, mod, w, col_scale, n_main, w_t):
    bsz, seq, d = x.shape
    n_total, n_t = w.shape[1], w_t.shape[0]
    assert n_main % PROJ_CHUNK == 0 and n_total % LANES == 0 and n_t % PROJ_CHUNK == 0
    tm = TOKEN_TILE
    out_shape = [jax.ShapeDtypeStruct((bsz, seq, n_main), BF16),
                 jax.ShapeDtypeStruct((bsz, seq // ATT_TK, n_t, ATT_TK), BF16)]
    out_specs = [pl.BlockSpec((None, tm, n_main), lambda b, i: (b, i, 0)),
                 pl.BlockSpec((None, tm // ATT_TK, n_t, ATT_TK), lambda b, i: (b, i, 0, 0))]
    if n_main < n_total:
        out_shape.append(jax.ShapeDtypeStruct((bsz, seq, n_total - n_main), F32))
        out_specs.append(pl.BlockSpec((None, tm, n_total - n_main), lambda b, i: (b, i, 0)))
    return pl.pallas_call(
        functools.partial(_proj_kernel, n_main=n_main, n_total=n_total),
        grid=(bsz, seq // tm),
        in_specs=[pl.BlockSpec((None, tm, d), lambda b, i: (b, i, 0)),
                  pl.BlockSpec((None, 6, d), lambda b, i: (b, 0, 0)),
                  pl.BlockSpec((d, n_total), lambda b, i: (0, 0)),
                  pl.BlockSpec((1, n_total), lambda b, i: (0, 0)),
                  pl.BlockSpec((n_t, d), lambda b, i: (0, 0))],
        out_specs=out_specs,
        out_shape=out_shape,
        compiler_params=_cparams(2),
        name="modulate_project",
    )(x, mod, w, col_scale, w_t)


def _mixer_out_ffn_kernel(o_ref, wmix_ref, x_ref, mod_ref, g1_ref, b1_ref, win_ref, cw_ref, cb_ref,
                          wo_ref, g2_ref, b2_ref, out_ref, a_scr, g_scr, halo_scr, *, alpha):
    tm = x_ref.shape[0]
    halo = halo_scr.shape[0]
    rb = tm // LN_ROW_BLOCKS
    x1 = jnp.concatenate(
        [_layer_norm(alpha * x_ref[r0:r0 + rb, :] + mod_ref[2:3, :] * _dot(o_ref[r0:r0 + rb, :], wmix_ref[...]),
                     g1_ref[...], b1_ref[...]) for r0 in range(0, tm, rb)], axis=0)
    shift, scale, gate = mod_ref[3:4, :], mod_ref[4:5, :], mod_ref[5:6, :]
    h = (x1 * (1.0 + scale) + shift).astype(BF16)

    @pl.when(pl.program_id(1) == 0)
    def _():
        halo_scr[...] = jnp.zeros_like(halo_scr)

    h_ext = jnp.concatenate([halo_scr[...], h], axis=0)
    halo_scr[...] = h[tm - halo:, :]
    chunks = [(c0, min(c0 + FF_CHUNK, D_FF)) for c0 in range(0, D_FF, FF_CHUNK)]

    def matmuls(i):
        c0, c1 = chunks[i]
        a_scr[i % 2, :, :c1 - c0] = _dot(h_ext, win_ref[:, c0:c1])
        return _dot(h, win_ref[:, D_FF + c0:D_FF + c1])

    b_next = matmuls(0)
    for i, (c0, c1) in enumerate(chunks):
        b = b_next
        if i + 1 < len(chunks):
            b_next = matmuls(i + 1)
        w = c1 - c0
        a = a_scr[i % 2, halo:halo + tm, :w]
        a_m1 = a_scr[i % 2, halo - 1:halo - 1 + tm, :w]
        a_m2 = a_scr[i % 2, halo - 2:halo - 2 + tm, :w]
        conv = (a * cw_ref[2:3, c0:c1] + a_m1 * cw_ref[1:2, c0:c1] + a_m2 * cw_ref[0:1, c0:c1]
                + cb_ref[:, c0:c1])
        g_scr[:, c0:c1] = (_gelu_tanh(conv) * b).astype(BF16)
    for r0 in range(0, tm, rb):
        y = _dot(g_scr[r0:r0 + rb, :], wo_ref[...])
        out_ref[r0:r0 + rb, :] = _layer_norm(alpha * x1[r0:r0 + rb, :] + gate * y, g2_ref[...], b2_ref[...])


def _mixer_out_ffn(o, w_mix, x, mod, ln_g, ln_b, w_in_all, conv_w, conv_b, w_out_all, layer, alpha):
    bsz, seq, d = x.shape
    k = o.shape[-1]
    tm = TOKEN_TILE
    halo = 2 * SUBLANES
    const = lambda shape: pl.BlockSpec(shape, lambda b, i: (0, 0), pipeline_mode=pl.Buffered(1))
    of_layer = lambda shape: pl.BlockSpec((None,) + shape, lambda b, i: (layer, 0, 0),
                                          pipeline_mode=pl.Buffered(1))
    return pl.pallas_call(
        functools.partial(_mixer_out_ffn_kernel, alpha=alpha),
        grid=(bsz, seq // tm),
        in_specs=[pl.BlockSpec((None, tm, k), lambda b, i: (b, i, 0)),
                  const((k, d)),
                  pl.BlockSpec((None, tm, d), lambda b, i: (b, i, 0)),
                  pl.BlockSpec((None, 6, d), lambda b, i: (b, 0, 0)),
                  const((1, d)), const((1, d)),
                  of_layer((d, 2 * D_FF)), const((3, D_FF)), const((1, D_FF)),
                  of_layer((D_FF, d)), const((1, d)), const((1, d))],
        out_specs=pl.BlockSpec((None, tm, d), lambda b, i: (b, i, 0)),
        out_shape=jax.ShapeDtypeStruct((bsz, seq, d), F32),
        scratch_shapes=[pltpu.VMEM((2, halo + tm, FF_CHUNK), F32), pltpu.VMEM((tm, D_FF), BF16),
                        pltpu.VMEM((halo, d), BF16)],
        compiler_params=_cparams(2),
        name="mixer_out_conv_ffn",
    )(o, w_mix, x, mod, ln_g[0].reshape(1, d), ln_b[0].reshape(1, d), w_in_all, conv_w,
      conv_b.reshape(1, D_FF), w_out_all, ln_g[1].reshape(1, d), ln_b[1].reshape(1, d))


def _rope_table_kernel(pos_ref, invf_ref, cos_ref, sin_ref):
    ang = pos_ref[...] * invf_ref[...]
    cos_ref[...] = jnp.cos(ang)
    sin_ref[...] = jnp.sin(ang)


def _rope_tables(pos_f, half):
    bsz, seq = pos_f.shape
    per_row = LANES // half
    rows = bsz * seq // per_row
    tr = min(rows, TOKEN_TILE)
    assert LANES % half == 0 and (bsz * seq) % per_row == 0 and rows % tr == 0
    pos_d = jnp.repeat(pos_f.reshape(rows, per_row), half, axis=1)
    inv_freq = ROPE_BASE ** (-jnp.arange(half, dtype=F32) / half)
    invf = jnp.tile(inv_freq, per_row).reshape(1, LANES)
    row_spec = pl.BlockSpec((tr, LANES), lambda i: (i, 0))
    cos_d, sin_d = pl.pallas_call(
        _rope_table_kernel,
        grid=(rows // tr,),
        in_specs=[row_spec, pl.BlockSpec((1, LANES), lambda i: (0, 0))],
        out_specs=[row_spec, row_spec],
        out_shape=[jax.ShapeDtypeStruct((rows, LANES), F32)] * 2,
        compiler_params=_cparams(1),
        name="rope_tables",
    )(pos_d, invf)
    return cos_d.reshape(bsz, seq, half), sin_d.reshape(bsz, seq, half)


def _mla_proj_kernel(x_ref, mod_ref, cos_ref, sin_ref, win_ref, qn_ref, kvn_ref, wuq_ref,
                     wuqr_ref, wuk_ref, wuvt_ref, q_ref, kn_ref, kr_ref, vt_ref, *, scale):
    h = (x_ref[...] * (1.0 + mod_ref[1:2, :]) + mod_ref[0:1, :]).astype(BF16)
    proj = _dot(h, win_ref[...])
    c_q = _rms_norm(proj[:, :MLA_Q_RANK], qn_ref[...]).astype(BF16)
    c_kv = _rms_norm(proj[:, MLA_Q_RANK:MLA_Q_RANK + MLA_KV_RANK], kvn_ref[...]).astype(BF16)
    cos, sin = cos_ref[...], sin_ref[...]
    r0 = MLA_Q_RANK + MLA_KV_RANK
    kr_ref[...] = (proj[:, r0:r0 + LANES] * cos + proj[:, r0 + LANES:r0 + 2 * LANES] * sin).astype(BF16)
    for hd in range(MLA_HEADS):
        qa = _dot(c_q, wuq_ref[:, hd * 2 * LANES:(hd + 1) * 2 * LANES])
        qr = _dot(c_q, wuqr_ref[:, hd * LANES:(hd + 1) * LANES])
        q_ref[:, hd * 2 * LANES:hd * 2 * LANES + LANES] = (qa[:, :LANES] * scale).astype(BF16)
        q_ref[:, hd * 2 * LANES + LANES:(hd + 1) * 2 * LANES] = (
            (qa[:, LANES:] * cos + qr * sin) * scale).astype(BF16)
    kn_ref[...] = _dot(c_kv, wuk_ref[...]).astype(BF16)
    v_t = _dot_nt(wuvt_ref[...], c_kv).astype(BF16)
    for t in range(vt_ref.shape[0]):
        vt_ref[t] = v_t[:, t * ATT_TK:(t + 1) * ATT_TK]


def _mla_attn_kernel(q_ref, kn_ref, kr_ref, vt_ref, o_ref):
    qi = pl.program_id(2)
    tq = q_ref.shape[0]
    hp = HEADS_PER_STEP

    def step(j, carry, masked):
        ks = pl.multiple_of(j * ATT_TK, ATT_TK)
        kr = kr_ref[pl.ds(ks, ATT_TK), :]
        def scores(h):
            k = jnp.concatenate([kn_ref[pl.ds(ks, ATT_TK), h * MLA_NOPE:(h + 1) * MLA_NOPE], kr], axis=1)
            s_t = _dot_nt(k, q_ref[:, h * 2 * LANES:(h + 1) * 2 * LANES])
            if masked:
                key = lax.broadcasted_iota(jnp.int32, s_t.shape, 0)
                qry = lax.broadcasted_iota(jnp.int32, s_t.shape, 1)
                s_t = jnp.where(key <= qry, s_t, NEG)
            return s_t

        s_all = [scores(h) for h in range(hp)]
        stats = [_softmax_stats_t(s_all[h], carry[h][0]) for h in range(hp)]
        out = []
        for h in range(hp):
            m_new, alpha, p = stats[h]
            v_t = _with_ones_rows(vt_ref[j, h * MLA_V:(h + 1) * MLA_V, :])
            out.append((m_new, alpha * carry[h][1] + _dot(v_t, p)))
        return tuple(out)

    init = tuple((jnp.full((1, tq), M_INIT, F32), jnp.zeros((MLA_V + 2 * SUBLANES, tq), F32))
                 for _ in range(hp))
    carry = lax.fori_loop(0, qi, lambda j, c: step(j, c, False), init)
    carry = step(qi, carry, True)
    for h in range(hp):
        o_ref[:, h * MLA_V:(h + 1) * MLA_V] = _normalize_t(carry[h][1], MLA_V).T.astype(BF16)


def _mla_mixer(x, mod, pos_f, w_in, q_norm, w_uq, kv_norm, w_ukv):
    bsz, seq, d = x.shape
    hh, dn, dr, dv = MLA_HEADS, MLA_NOPE, MLA_ROPE, MLA_V
    half = dr // 2
    r0 = MLA_Q_RANK + MLA_KV_RANK
    w_in, w_uq, w_ukv = w_in.astype(BF16), w_uq.astype(BF16), w_ukv.astype(BF16)
    zpad = lambda rows, n: jnp.zeros((rows, n), BF16)
    x1, x2 = w_in[:, r0:r0 + half], w_in[:, r0 + half:r0 + dr]
    w_in_ext = jnp.concatenate([w_in[:, :r0], x1, x2, zpad(d, LANES - dr),
                                -x2, x1, zpad(d, LANES - dr)], axis=1)
    wq = w_uq.reshape(MLA_Q_RANK, hh, dn + dr)
    q1, q2 = wq[:, :, dn:dn + half], wq[:, :, dn + half:]
    zq = jnp.zeros((MLA_Q_RANK, hh, LANES - dr), BF16)
    w_uq_main = jnp.concatenate([wq[:, :, :dn], q1, q2, zq], axis=2).reshape(MLA_Q_RANK, hh * 2 * LANES)
    w_uq_rot = jnp.concatenate([-q2, q1, zq], axis=2).reshape(MLA_Q_RANK, hh * LANES)
    wkv = w_ukv.reshape(MLA_KV_RANK, hh, dn + dv)
    w_uk = wkv[:, :, :dn].reshape(MLA_KV_RANK, hh * dn)
    w_uv_t = wkv[:, :, dn:].reshape(MLA_KV_RANK, hh * dv).T
    cos, sin = _rope_tables(pos_f, half)
    lane_pad = lambda t: jnp.concatenate([t, t, jnp.zeros((bsz, seq, LANES - dr), F32)], axis=-1)
    cos128, sin128 = lane_pad(cos), lane_pad(sin)

    tm = TOKEN_TILE
    const = lambda b, i: (0, 0)
    tok = lambda n: pl.BlockSpec((None, tm, n), lambda b, i: (b, i, 0))
    q, kn, kr, v_t = pl.pallas_call(
        functools.partial(_mla_proj_kernel, scale=float((dn + dr) ** -0.5) * LOG2E),
        grid=(bsz, seq // tm),
        in_specs=[tok(d),
                  pl.BlockSpec((None, 6, d), lambda b, i: (b, 0, 0)),
                  tok(LANES),
                  tok(LANES),
                  pl.BlockSpec(w_in_ext.shape, const),
                  pl.BlockSpec((1, MLA_Q_RANK), const),
                  pl.BlockSpec((1, MLA_KV_RANK), const),
                  pl.BlockSpec(w_uq_main.shape, const),
                  pl.BlockSpec(w_uq_rot.shape, const),
                  pl.BlockSpec(w_uk.shape, const),
                  pl.BlockSpec(w_uv_t.shape, const)],
        out_specs=[tok(hh * 2 * LANES), tok(hh * dn), tok(LANES),
                   pl.BlockSpec((None, tm // ATT_TK, hh * dv, ATT_TK), lambda b, i: (b, i, 0, 0))],
        out_shape=[jax.ShapeDtypeStruct((bsz, seq, hh * 2 * LANES), BF16),
                   jax.ShapeDtypeStruct((bsz, seq, hh * dn), BF16),
                   jax.ShapeDtypeStruct((bsz, seq, LANES), BF16),
                   jax.ShapeDtypeStruct((bsz, seq // ATT_TK, hh * dv, ATT_TK), BF16)],
        compiler_params=_cparams(2),
        name="mla_project",
    )(x, mod, cos128, sin128, w_in_ext, q_norm.reshape(1, -1), kv_norm.reshape(1, -1),
      w_uq_main, w_uq_rot, w_uk, w_uv_t)

    tq, hp = ATT_TQ, HEADS_PER_STEP
    return pl.pallas_call(
        _mla_attn_kernel,
        grid=(bsz, hh // hp, seq // tq),
        in_specs=[pl.BlockSpec((None, tq, hp * 2 * LANES), lambda b, h, i: (b, i, h)),
                  pl.BlockSpec((None, seq, hp * dn), lambda b, h, i: (b, 0, h)),
                  pl.BlockSpec((None, seq, LANES), lambda b, h, i: (b, 0, 0)),
                  pl.BlockSpec((None, seq // ATT_TK, hp * dv, ATT_TK), lambda b, h, i: (b, 0, h, 0))],
        out_specs=pl.BlockSpec((None, tq, hp * dv), lambda b, h, i: (b, i, h)),
        out_shape=jax.ShapeDtypeStruct((bsz, seq, hh * dv), BF16),
        compiler_params=_cparams(3),
        name="mla_attention",
    )(q, kn, kr, v_t)


def _moba_kernel(slope_ref, q_ref, k_ref, vt_ref, pk_ref, pq_ref, o_ref, kmean_ref, *, n_blocks):
    blk, hp, hd = MOBA_BLOCK, HEADS_PER_STEP, MOBA_HD
    hg = pl.program_id(1)
    qi = pl.program_id(2)

    @pl.when(qi == 0)
    def _():
        kmean_ref[...] = jnp.zeros_like(kmean_ref)
        for h in range(hp):
            for n in range(n_blocks):
                kmean_ref[h, n:n + 1, :] = jnp.mean(
                    k_ref[n * blk:(n + 1) * blk, h * hd:(h + 1) * hd].astype(F32), axis=0, keepdims=True)

    lane = lax.broadcasted_iota(jnp.int32, (blk, LANES), 1)
    nb_pad = -(-n_blocks // SUBLANES) * SUBLANES
    blk_row = lax.broadcasted_iota(jnp.int32, (nb_pad, blk), 0)
    past = blk_row < qi
    q_aug = []
    for h in range(hp):
        q = q_ref[:, h * hd:(h + 1) * hd]
        km = kmean_ref[h]
        km_hi = km.astype(BF16)
        km_lo = (km - km_hi.astype(F32)).astype(BF16)
        gate_t = (_dot_nt(km_hi, q) + _dot_nt(km_lo, q))[:nb_pad, :]
        g = jnp.where(past, gate_t, LOW)
        rank = jnp.zeros(g.shape, jnp.int32)
        for dlt in range(1, nb_pad):
            other = pltpu.roll(g, dlt, axis=0)
            rank = rank + jnp.where(blk_row >= dlt, (other >= g).astype(jnp.int32),
                                    (other > g).astype(jnp.int32))
        chosen_t = jnp.logical_or(jnp.logical_and(rank < MOBA_TOPK, past), blk_row == qi)
        bias_t = jnp.concatenate([jnp.where(chosen_t, 0.0, NEG), jnp.full((LANES - nb_pad, blk), NEG, F32)],
                                 axis=0)
        q_aug.append(jnp.concatenate([q, bias_t.T.astype(BF16)], axis=1))

    pq = pq_ref[...]

    def step(n, carry, masked):
        ks = pl.multiple_of(n * blk, blk)
        onehot = jnp.where(lane == n, 1.0, 0.0).astype(BF16)
        dist_t = jnp.abs(pk_ref[pl.ds(ks, blk), :] - pq)
        s_all = []
        for h in range(hp):
            k_aug = jnp.concatenate([k_ref[pl.ds(ks, blk), h * hd:(h + 1) * hd], onehot], axis=1)
            s_t = _dot_nt(k_aug, q_aug[h]) - slope_ref[hg * hp + h] * dist_t
            if masked:
                key = lax.broadcasted_iota(jnp.int32, s_t.shape, 0)
                qry = lax.broadcasted_iota(jnp.int32, s_t.shape, 1)
                s_t = jnp.where(key <= qry, s_t, NEG)
            s_all.append(s_t)
        stats = [_softmax_stats_t(s_all[h], carry[h][0]) for h in range(hp)]
        out = []
        for h in range(hp):
            m_new, alpha, p = stats[h]
            v_t = _with_ones_rows(vt_ref[n, h * hd:(h + 1) * hd, :])
            out.append((m_new, alpha * carry[h][1] + _dot(v_t, p)))
        return tuple(out)

    init = tuple((jnp.full((1, blk), M_INIT, F32), jnp.zeros((hd + 2 * SUBLANES, blk), F32))
                 for _ in range(hp))
    carry = lax.fori_loop(0, qi, lambda n, c: step(n, c, False), init)
    carry = step(qi, carry, True)
    for h in range(hp):
        o_ref[:, h * hd:(h + 1) * hd] = _normalize_t(carry[h][1], hd).T.astype(BF16)


def _alibi_slopes(n):
    return 2.0 ** (-8.0 * jnp.arange(1, n + 1, dtype=F32) / n)


def _moba_mixer(x, mod, pos_col, pos_f, w_in):
    bsz, seq, d = x.shape
    hh, hd, blk = MOBA_HEADS, MOBA_HD, MOBA_BLOCK
    assert seq % blk == 0 and seq // blk + MOBA_TOPK < LANES and blk == ATT_TK
    n_blocks = seq // blk
    hp = HEADS_PER_STEP
    nqk = 2 * hh * hd
    col_scale = jnp.concatenate([jnp.full((hh * hd,), hd ** -0.5 * LOG2E, F32),
                                 jnp.ones((hh * hd,), F32)]).reshape(1, -1)
    w_bf = w_in.astype(BF16)
    qk, v_t = _project(x, mod, w_bf[:, :nqk], col_scale, nqk, w_bf[:, nqk:].T)
    pos_row = pos_f.reshape(bsz, n_blocks, 1, blk)
    return pl.pallas_call(
        functools.partial(_moba_kernel, n_blocks=n_blocks),
        grid=(bsz, hh // hp, n_blocks),
        in_specs=[pl.BlockSpec(memory_space=pltpu.SMEM),
                  pl.BlockSpec((None, blk, hp * hd), lambda b, h, i: (b, i, h)),
                  pl.BlockSpec((None, seq, hp * hd), lambda b, h, i: (b, 0, hh // hp + h)),
                  pl.BlockSpec((None, n_blocks, hp * hd, blk), lambda b, h, i: (b, 0, h, 0)),
                  pl.BlockSpec((None, seq, 1), lambda b, h, i: (b, 0, 0)),
                  pl.BlockSpec((None, None, 1, blk), lambda b, h, i: (b, i, 0, 0))],
        out_specs=pl.BlockSpec((None, blk, hp * hd), lambda b, h, i: (b, i, h)),
        out_shape=jax.ShapeDtypeStruct((bsz, seq, hh * hd), BF16),
        scratch_shapes=[pltpu.VMEM((hp, LANES, hd), F32)],
        compiler_params=_cparams(3),
        name="moba_attention",
    )(_alibi_slopes(hh) * LOG2E, qk, qk, v_t, pos_col, pos_row)


def _sb_kernel(q_ref, k_ref, vt_ref, o_ref):
    qi = pl.program_id(2)
    tq, tk, hp, hd = q_ref.shape[0], ATT_TK, HEADS_PER_STEP, SB_HD
    rs = lax.broadcasted_iota(jnp.int32, (tk, tk), 0)
    cj = lax.broadcasted_iota(jnp.int32, (tk, tk), 1)
    after = jnp.where(cj > rs, 1.0, 0.0).astype(BF16)
    after2 = jnp.concatenate([after, after], axis=1)

    def step(j, carry, masked):
        ks = pl.multiple_of(j * tk, tk)
        z_all = [_dot_nt(k_ref[pl.ds(ks, tk), h * hd:(h + 1) * hd], q_ref[:, h * hd:(h + 1) * hd])
                 for h in range(hp)]
        if masked:
            key = lax.broadcasted_iota(jnp.int32, (tk, tq), 0)
            qry = lax.broadcasted_iota(jnp.int32, (tk, tq), 1)
            mask = key < qry
        parts = []
        for h in range(hp):
            z2 = z_all[h]
            log_beta = jnp.minimum(z2, 0.0) - jnp.log2(1.0 + jnp.exp2(-jnp.abs(z2)))
            log_1mb = log_beta - z2
            if masked:
                log_1mb = jnp.where(mask, log_1mb, 0.0)
            hi = log_1mb.astype(BF16)
            lo = (log_1mb - hi.astype(F32)).astype(BF16)
            parts.append((log_beta, log_1mb, jnp.concatenate([hi, lo], axis=0)))
        tails = [_dot(after2, parts[h][2]) for h in range(hp)]
        out = []
        for h in range(hp):
            c, acc_t = carry[h]
            log_beta, log_1mb = parts[h][0], parts[h][1]
            a = jnp.exp2(log_beta + tails[h] + c)
            if masked:
                a = jnp.where(mask, a, 0.0)
            acc_t = acc_t + _dot(vt_ref[j, h * hd:(h + 1) * hd, :], a.astype(BF16))
            out.append((c + jnp.sum(log_1mb, axis=0, keepdims=True), acc_t))
        return tuple(out)

    init = tuple((jnp.zeros((1, tq), F32), jnp.zeros((hd, tq), F32)) for _ in range(hp))
    carry = step(qi, init, True)

    def live(state):
        t, cr = state
        c_max = cr[0][0]
        for h in range(1, hp):
            c_max = jnp.maximum(c_max, cr[h][0])
        return jnp.logical_and(t < qi, jnp.max(c_max) > SB_DEAD_LOG2)

    _, carry = lax.while_loop(live, lambda st: (st[0] + 1, step(qi - 1 - st[0], st[1], False)),
                              (jnp.int32(0), carry))
    for h in range(hp):
        o_ref[:, h * hd:(h + 1) * hd] = carry[h][1].T.astype(BF16)


def _sb_mixer(x, mod, w_in):
    bsz, seq, d = x.shape
    hh, hd, hp = SB_HEADS, SB_HD, HEADS_PER_STEP
    nqk = 2 * hh * hd
    col_scale = jnp.concatenate([jnp.full((hh * hd,), hd ** -0.5 * LOG2E, F32),
                                 jnp.ones((hh * hd,), F32)]).reshape(1, -1)
    w_bf = w_in.astype(BF16)
    qk, v_t = _project(x, mod, w_bf[:, :nqk], col_scale, nqk, w_bf[:, nqk:].T)
    tq = ATT_TQ
    assert tq == ATT_TK
    return pl.pallas_call(
        _sb_kernel,
        grid=(bsz, hh // hp, seq // tq),
        in_specs=[pl.BlockSpec((None, tq, hp * hd), lambda b, h, i: (b, i, h)),
                  pl.BlockSpec((None, seq, hp * hd), lambda b, h, i: (b, 0, hh // hp + h)),
                  pl.BlockSpec((None, seq // ATT_TK, hp * hd, ATT_TK), lambda b, h, i: (b, 0, h, 0))],
        out_specs=pl.BlockSpec((None, tq, hp * hd), lambda b, h, i: (b, i, h)),
        out_shape=jax.ShapeDtypeStruct((bsz, seq, hh * hd), BF16),
        compiler_params=_cparams(3),
        name="stick_breaking_attention",
    )(qk, qk, v_t)


def _nsa_compress_kernel(t_ref, pe_ref, w1_ref, w2_ref, o_ref, ot_ref, *, n_cmp):
    st, half = NSA_CMP_STRIDE, NSA_CMP_LEN // NSA_CMP_STRIDE
    assert half == 2
    rows = t_ref.shape[0] // st
    xa, xb = [], []
    for r in range(st):
        xr = t_ref[pl.ds(r, rows, stride=st), :]
        xa.append((xr + pe_ref[r:r + 1, :]).astype(BF16))
        xb.append((xr + pe_ref[st + r:st + r + 1, :]).astype(BF16))
    k_half = st * NSA_HD
    pre_a = _dot(jnp.concatenate(xa, axis=1), w1_ref[:k_half, :])
    pre_b = _dot(jnp.concatenate(xb, axis=1), w1_ref[k_half:, :])
    pre = pre_a + pltpu.roll(pre_b, rows - 1, axis=0)
    out = _dot(_gelu_tanh(pre).astype(BF16), w2_ref[...])
    row = lax.broadcasted_iota(jnp.int32, out.shape, 0)
    out = jnp.where(row < n_cmp, out, 0.0)
    o_ref[...] = out.astype(BF16)
    ot_ref[...] = out.T.astype(BF16)


def _nsa_cmp_select_kernel(slope_ref, q_ref, kc_ref, vct_ref, gt_ref, pq_ref, pe_ref, ovlt_ref,
                           acc_ref, sel_ref, *, n_cmp, n_sel):
    qi = pl.program_id(1)
    tq, gg, rr, hd = q_ref.shape[0], NSA_GROUPS, NSA_R, NSA_HD
    shape = (LANES, tq)
    n_row = lax.broadcasted_iota(jnp.int32, shape, 0)
    qidx = qi * tq + lax.broadcasted_iota(jnp.int32, shape, 1)
    c_end = n_row * NSA_CMP_STRIDE + (NSA_CMP_LEN - 1)
    mask_c = jnp.logical_and(c_end <= qidx, n_row < n_cmp)
    dist_t = jnp.abs(pe_ref[...] - pq_ref[...])
    sig_t = jax.nn.sigmoid(gt_ref[...]).T
    ovl_t = ovlt_ref[...]

    heads = [(g, r) for g in range(gg) for r in range(rr)]
    s_all = [_dot_nt(kc_ref[g], q_ref[:, (g * rr + r) * hd:(g * rr + r + 1) * hd])
             - slope_ref[g * rr + r] * dist_t for g, r in heads]
    p_all = []
    for s_t in s_all:
        s_t = jnp.where(mask_c, s_t, NEG)
        p = jnp.where(mask_c, jnp.exp2(s_t - jnp.max(s_t, axis=0, keepdims=True)), 0.0)
        p_all.append((p / jnp.maximum(jnp.sum(p, axis=0, keepdims=True), TINY)).astype(BF16))
    imp_t = [jnp.zeros((n_sel, tq), F32) for _ in range(gg)]
    for (g, r), p in zip(heads, p_all):
        imp_t[g] = imp_t[g] + _dot(ovl_t, p)
        c = g * rr + r
        acc_ref[:, c * hd:(c + 1) * hd] = (sig_t[c:c + 1, :] * _dot(vct_ref[gg + g], p)).T

    sshape = (n_sel, tq)
    j_row = lax.broadcasted_iota(jnp.int32, sshape, 0)
    q_blk = (qi * tq + lax.broadcasted_iota(jnp.int32, sshape, 1)) >> 6
    assert NSA_SEL_BLOCK == 64
    forced = jnp.logical_or(j_row == 0, jnp.logical_or(j_row == q_blk, j_row == q_blk - 1))
    cand = j_row <= q_blk
    bonus = jnp.where(forced, NSA_FORCE_BONUS, 0.0)
    for g in range(gg):
        score = jnp.where(cand, imp_t[g] + bonus, NEG)
        rank = jnp.zeros(sshape, jnp.int32)
        for dlt in range(1, n_sel):
            other = pltpu.roll(score, dlt, axis=0)
            rank = rank + jnp.where(j_row >= dlt, (other >= score).astype(jnp.int32),
                                    (other > score).astype(jnp.int32))
        chosen_t = jnp.logical_and(rank < NSA_SEL_TOPN, cand)
        bias_t = jnp.concatenate([jnp.where(chosen_t, 0.0, NEG), jnp.full((LANES - n_sel, tq), NEG, F32)],
                                 axis=0)
        sel_ref[g] = bias_t.T.astype(BF16)


def _nsa_select_window_kernel(slope_ref, q_ref, sel_ref, k_ref, vt_ref, gt_ref, pk_ref, pq_ref, prev_ref,
                              out_ref):
    qi = pl.program_id(1)
    tq, tk, hd, gg, rr = q_ref.shape[0], ATT_TK, NSA_HD, NSA_GROUPS, NSA_R
    assert tq == tk and NSA_WINDOW % tk == 0
    lanes = rr * tq
    pq4 = jnp.concatenate([pq_ref[...]] * rr, axis=1)
    qidx = qi * tq + (lax.broadcasted_iota(jnp.int32, (tk, lanes), 1) & (tq - 1))
    krow = lax.broadcasted_iota(jnp.int32, (tk, lanes), 0)
    klane = lax.broadcasted_iota(jnp.int32, (tk, LANES), 1)
    kblk = lax.broadcasted_iota(jnp.int32, (tk, LANES), 0) >> 6
    per_tile = tk // NSA_SEL_BLOCK
    slope_rows = [jnp.concatenate([jnp.full((1, tq), slope_ref[g * rr + r], F32) for r in range(rr)], axis=1)
                  for g in range(gg)]
    q4 = [jnp.concatenate([q_ref[:, (g * rr + r) * hd:(g * rr + r + 1) * hd] for r in range(rr)], axis=0)
          for g in range(gg)]
    q_aug = [jnp.concatenate([q4[g], jnp.concatenate([sel_ref[g]] * rr, axis=0)], axis=1) for g in range(gg)]
    k_win0 = gg * hd

    def update(s_all, carry, v_row0, j):
        stats = [_softmax_stats_t(s_all[g], carry[g][0]) for g in range(gg)]
        out = []
        for g in range(gg):
            m_new, alpha, p = stats[g]
            v_t = _with_ones_rows(vt_ref[j, v_row0 + g * hd:v_row0 + (g + 1) * hd, :])
            out.append((m_new, alpha * carry[g][1] + _dot(v_t, p)))
        return tuple(out)

    def select_step(j, carry, masked):
        ks = pl.multiple_of(j * tk, tk)
        dist = jnp.abs(pk_ref[pl.ds(ks, tk), :] - pq4)
        onehot = jnp.where(klane == j * per_tile + kblk, 1.0, 0.0).astype(BF16)
        s_all = []
        for g in range(gg):
            k_aug = jnp.concatenate([k_ref[pl.ds(ks, tk), g * hd:(g + 1) * hd], onehot], axis=1)
            s_t = _dot_nt(k_aug, q_aug[g]) - slope_rows[g] * dist
            if masked:
                s_t = jnp.where(j * tk + krow <= qidx, s_t, NEG)
            s_all.append(s_t)
        return update(s_all, carry, 0, j)

    def init():
        return tuple((jnp.full((1, lanes), M_INIT, F32), jnp.zeros((hd + 2 * SUBLANES, lanes), F32))
                     for _ in range(gg))

    sel = lax.fori_loop(0, qi, lambda j, c: select_step(j, c, False), init())
    sel = select_step(qi, sel, True)

    win = init()
    n_band = NSA_WINDOW // tk + 1
    qloc = qidx - qi * tq
    for t in range(n_band):
        jt = qi - (n_band - 1) + t
        jc = jnp.maximum(jt, 0)
        ks = pl.multiple_of(jc * tk, tk)
        none = jnp.where(jt >= 0, 0, tk)
        if t == n_band - 1:
            mask = krow <= qloc
        elif t == 0:
            mask = krow > qloc + none
        else:
            mask = krow >= none
        dist = jnp.abs(pk_ref[pl.ds(ks, tk), :] - pq4)
        s_all = []
        for g in range(gg):
            s_t = _dot_nt(k_ref[pl.ds(ks, tk), k_win0 + g * hd:k_win0 + (g + 1) * hd], q4[g])
            s_all.append(jnp.where(mask, s_t - slope_rows[g] * dist, NEG))
        win = update(s_all, win, gg * hd, jc)

    sig_t = jax.nn.sigmoid(gt_ref[...]).T
    for g in range(gg):
        gate = lambda branch: jnp.concatenate(
            [sig_t[branch * NSA_HEADS + g * rr + r:branch * NSA_HEADS + g * rr + r + 1, :] for r in range(rr)],
            axis=1)
        o_t = gate(1) * _normalize_t(sel[g][1], hd) + gate(2) * _normalize_t(win[g][1], hd)
        for r in range(rr):
            c = g * rr + r
            out_ref[:, c * hd:(c + 1) * hd] = (prev_ref[:, c * hd:(c + 1) * hd]
                                                + o_t[:, r * tq:(r + 1) * tq].T).astype(BF16)


def _nsa_mixer(x, mod, pos_col, pos_f, w_in, cmp_pos, cmp_w1, cmp_w2):
    bsz, seq, d = x.shape
    hh, gg, hd = NSA_HEADS, NSA_GROUPS, NSA_HD
    n_cmp = (seq - NSA_CMP_LEN) // NSA_CMP_STRIDE + 1
    n_sel = seq // NSA_SEL_BLOCK
    tq = ATT_TQ
    assert seq % tq == 0 and seq // NSA_CMP_STRIDE == LANES and n_sel % SUBLANES == 0 and n_sel <= LANES
    assert NSA_SEL_TOPN <= n_sel
    nq, nkv = hh * hd, gg * hd
    w_bf = w_in.astype(BF16)
    kv = lambda t: w_bf[:, nq + t * nkv:nq + (t + 1) * nkv]
    n_gate = 3 * hh
    w_perm = jnp.concatenate([w_bf[:, :nq], kv(2), kv(4), kv(0), kv(1),
                              w_bf[:, nq + 6 * nkv:], jnp.zeros((d, LANES - n_gate), BF16)], axis=1)
    n_main = nq + 2 * nkv
    col_scale = jnp.concatenate([jnp.full((nq,), hd ** -0.5 * LOG2E, F32),
                                 jnp.ones((w_perm.shape[1] - nq,), F32)]).reshape(1, -1)
    w_vt = jnp.concatenate([kv(3), kv(5)], axis=1).T
    main, v_t, tail = _project(x, mod, w_perm, col_scale, n_main, w_vt)
    slopes = _alibi_slopes(hh) * LOG2E
    gate_blk = 2 * nkv // LANES

    cmp_spec = pl.BlockSpec((None, None, LANES, hd), lambda b, w, g: (b, w * gg + g, 0, 0))
    cmp_kv, cmp_kv_t = pl.pallas_call(
        functools.partial(_nsa_compress_kernel, n_cmp=n_cmp),
        grid=(bsz, 2, gg),
        in_specs=[pl.BlockSpec((None, seq, hd), lambda b, w, g: (b, 0, w * gg + g)),
                  pl.BlockSpec((None, NSA_CMP_LEN, hd), lambda b, w, g: (w, 0, 0)),
                  pl.BlockSpec((None, NSA_CMP_LEN * hd, hd), lambda b, w, g: (w, 0, 0)),
                  pl.BlockSpec((None, hd, hd), lambda b, w, g: (w, 0, 0))],
        out_specs=[cmp_spec, cmp_spec],
        out_shape=[jax.ShapeDtypeStruct((bsz, 2 * gg, LANES, hd), BF16)] * 2,
        compiler_params=_cparams(3),
        name="nsa_compress",
    )(tail, cmp_pos, cmp_w1.astype(BF16), cmp_w2.astype(BF16))

    pos_cend = jnp.pad(pos_f[:, NSA_CMP_LEN - 1::NSA_CMP_STRIDE], ((0, 0), (0, LANES - n_cmp)))
    pos_cend = pos_cend.reshape(bsz, LANES, 1)
    c_start = jnp.arange(LANES) * NSA_CMP_STRIDE
    j_start = jnp.arange(n_sel) * NSA_SEL_BLOCK
    overlap_t = ((c_start[None, :] < j_start[:, None] + NSA_SEL_BLOCK)
                 & (c_start[None, :] + NSA_CMP_LEN - 1 >= j_start[:, None])
                 & (jnp.arange(LANES)[None, :] < n_cmp)).astype(BF16)

    smem = pl.BlockSpec(memory_space=pltpu.SMEM)
    q_spec = pl.BlockSpec((None, tq, nq), lambda b, i: (b, i, 0))
    gate_spec = pl.BlockSpec((None, tq, LANES), lambda b, i: (b, i, gate_blk))
    pq_spec = pl.BlockSpec((None, None, 1, tq), lambda b, i: (b, i, 0, 0))
    sel_spec = pl.BlockSpec((None, gg, tq, LANES), lambda b, i: (b, 0, i, 0))
    cmp_all = pl.BlockSpec((None, 2 * gg, LANES, hd), lambda b, i: (b, 0, 0, 0))
    pos_q = pos_f.reshape(bsz, seq // tq, 1, tq)
    grid = (bsz, seq // tq)

    acc0, sel_bias = pl.pallas_call(
        functools.partial(_nsa_cmp_select_kernel, n_cmp=n_cmp, n_sel=n_sel),
        grid=grid,
        in_specs=[smem, q_spec, cmp_all, cmp_all, gate_spec, pq_spec,
                  pl.BlockSpec((None, LANES, 1), lambda b, i: (b, 0, 0)),
                  pl.BlockSpec((n_sel, LANES), lambda b, i: (0, 0))],
        out_specs=[q_spec, sel_spec],
        out_shape=[jax.ShapeDtypeStruct((bsz, seq, nq), F32),
                   jax.ShapeDtypeStruct((bsz, gg, seq, LANES), BF16)],
        compiler_params=_cparams(2),
        name="nsa_compressed_attention_select",
    )(slopes, main, cmp_kv, cmp_kv_t, tail, pos_q, pos_cend, overlap_t)

    return pl.pallas_call(
        _nsa_select_window_kernel,
        grid=grid,
        in_specs=[smem, q_spec, sel_spec,
                  pl.BlockSpec((None, seq, 2 * nkv), lambda b, i: (b, 0, nq // (2 * nkv))),
                  pl.BlockSpec((None, seq // ATT_TK, 2 * nkv, ATT_TK), lambda b, i: (b, 0, 0, 0)),
                  gate_spec,
                  pl.BlockSpec((None, seq, 1), lambda b, i: (b, 0, 0)),
                  pq_spec, q_spec],
        out_specs=q_spec,
        out_shape=jax.ShapeDtypeStruct((bsz, seq, nq), BF16),
        compiler_params=_cparams(2),
        name="nsa_selected_window_attention",
    )(slopes, main, sel_bias, main, v_t, tail, pos_col, pos_q, acc0)


def kernel(x, c, positions, mod_w, mod_b, ln_g, ln_b, ffn_w_in, ffn_conv_w, ffn_conv_b, ffn_w_out,
           mla_w_in, mla_q_norm, mla_w_uq, mla_kv_norm, mla_w_ukv, mla_w_o,
           moba_w_in, moba_w_o, nsa_w_in, nsa_cmp_pos, nsa_cmp_w1, nsa_cmp_w2, nsa_w_o,
           sb_w_in, sb_w_o):
    bsz, seq, d = x.shape
    depth = mod_w.shape[0]
    assert d == D_MODEL and seq % TOKEN_TILE == 0 and seq % ATT_TQ == 0
    alpha = float((2 * depth) ** 0.25)
    pos_f = positions.astype(F32)
    pos_col = pos_f.reshape(bsz, seq, 1)
    mod_all = _modulation(c, mod_w, mod_b)
    ffn_w_in_bf, ffn_w_out_bf = ffn_w_in.astype(BF16), ffn_w_out.astype(BF16)
    for i in range(depth):
        kind, j = i % N_MIXERS, i // N_MIXERS
        mod = mod_all[i]
        if kind == 0:
            o = _mla_mixer(x, mod, pos_f, mla_w_in[j], mla_q_norm[j], mla_w_uq[j], mla_kv_norm[j],
                           mla_w_ukv[j])
            w_o = mla_w_o[j]
        elif kind == 1:
            o = _moba_mixer(x, mod, pos_col, pos_f, moba_w_in[j])
            w_o = moba_w_o[j]
        elif kind == 2:
            o = _nsa_mixer(x, mod, pos_col, pos_f, nsa_w_in[j], nsa_cmp_pos[j], nsa_cmp_w1[j],
                           nsa_cmp_w2[j])
            w_o = nsa_w_o[j]
        else:
            o = _sb_mixer(x, mod, sb_w_in[j])
            w_o = sb_w_o[j]
        x = _mixer_out_ffn(o, w_o.astype(BF16), x, mod, ln_g[i], ln_b[i], ffn_w_in_bf, ffn_conv_w[i],
                           ffn_conv_b[i], ffn_w_out_bf, i, alpha)
    return x
```
